```python
import math
import jax, jax.numpy as jnp
from jax import lax
import numpy as np

D_MODEL = 2048
BATCH = 4
SEQ = 2048
DEPTH = 4
DEC_BATCH = 8
DEC_SEQ = 8
PAST_LEN = 16384
PAGE_SIZE = 128

A_HEADS = 4
A_DK = 128
A_DV = 2 * A_DK
B_HEADS = 4
B_DK = 64
B_DV = 128
GLA_GATE_RANK = 16
GLA_TAU = 16.0
C_HEADS = 4
C_DK = 64
C_DV = 128
ROPE_BASE = 10000.0
CHUNK = 64
Q_BLOCK = 128
D_FF = 5504
N_MOD = 9
EPS = 1e-6

IN_WIDTHS = (A_HEADS * 2 * A_DK, A_HEADS * 2 * A_DK, A_HEADS * A_DV,
             B_HEADS * B_DK, B_HEADS * B_DK, B_HEADS * B_DV, B_HEADS * B_DV, GLA_GATE_RANK,
             C_HEADS * C_DK, C_HEADS * C_DK, C_HEADS * C_DV, C_HEADS * C_DV,
             D_MODEL, D_MODEL, D_MODEL)
N_IN = sum(IN_WIDTHS)

kernel_name = 'hybrid_diffattn_gla_retnet_macaron_adaln_step'


def rms(x):
    xf = x.astype(jnp.float32)
    return (xf * lax.rsqrt(jnp.mean(xf * xf, axis=-1, keepdims=True) + EPS)).astype(x.dtype)


def modulate(x, shift, scale):
    return rms(x) * (1.0 + scale[:, None, :]) + shift[:, None, :]


def swiglu(h, w_up, w_down):
    a, b = jnp.split(h @ w_up, 2, axis=-1)
    return (jax.nn.silu(a) * b) @ w_down


def split_in(u):
    idx, acc = [], 0
    for w in IN_WIDTHS[:-1]:
        acc += w
        idx.append(acc)
    return jnp.split(u, idx, axis=-1)


def rotary(x, pos):
    half = x.shape[-1] // 2
    freq = 1.0 / (ROPE_BASE ** jnp.linspace(0.0, 1.0, half, dtype=jnp.float32))
    ang = pos.astype(jnp.float32)[:, None] * freq[None, :]
    cos = jnp.cos(ang)[None, :, None, :]
    sin = jnp.sin(ang)[None, :, None, :]
    xf = x.astype(jnp.float32)
    x1, x2 = xf[..., :half], xf[..., half:]
    return jnp.concatenate([x1 * cos - x2 * sin, x2 * cos + x1 * sin], axis=-1).astype(x.dtype)


def diff_attention(q, k, v, q_pos, k_pos, lam):
    nb, tq = q.shape[0], q.shape[1]
    qb = min(Q_BLOCK, tq)
    n_blk = -(-tq // qb)
    pad = n_blk * qb - tq
    q = jnp.pad(q, ((0, 0), (0, pad), (0, 0), (0, 0), (0, 0)))
    q_pos = jnp.pad(q_pos, (0, pad), mode='edge')
    q_blocks = q.reshape((nb, n_blk, qb) + q.shape[2:]).swapaxes(0, 1)
    p_blocks = q_pos.reshape(n_blk, qb)
    scale = A_DK ** -0.5
    neg = jnp.finfo(jnp.float32).min

    def one_block(args):
        q_blk, p_blk = args
        s = jnp.einsum('bqhnd,bkhnd->nbhqk', q_blk, k, preferred_element_type=jnp.float32) * scale
        s = jnp.where(k_pos[None, :] <= p_blk[:, None], s, neg)
        p = jax.nn.softmax(s, axis=-1)
        p = p[0] - lam * p[1]
        return jnp.einsum('bhqk,bkhd->bqhd', p.astype(v.dtype), v)

    o = lax.map(one_block, (q_blocks, p_blocks))
    o = o.swapaxes(0, 1).reshape((nb, n_blk * qb) + o.shape[3:])
    return o[:, :tq]


def gated_linear_recurrence(q, k, v, log_a, s0):
    nb, T, H, _ = q.shape
    dv = v.shape[-1]
    C = min(CHUNK, T)
    n = -(-T // C)
    pad = n * C - T

    def to_chunks(a):
        a = jnp.pad(a.astype(jnp.float32), ((0, 0), (0, pad), (0, 0), (0, 0)))
        return a.reshape(nb, n, C, H, a.shape[-1]).swapaxes(0, 1)

    qc, kc, vc, ac = to_chunks(q), to_chunks(k), to_chunks(v), to_chunks(log_a)
    causal = jnp.tril(jnp.ones((C, C), dtype=bool))

    def step(S, inp):
        qi, ki, vi, ai = inp
        b = jnp.cumsum(ai, axis=1)
        b_last = b[:, -1]
        q_t = qi * jnp.exp(b)
        k_t = ki * jnp.exp(-b)
        attn = jnp.where(causal, jnp.einsum('bihd,bjhd->bhij', q_t, k_t), 0.0)
        o = jnp.einsum('bihd,bhde->bihe', q_t, S) + jnp.einsum('bhij,bjhe->bihe', attn, vi)
        k_end = ki * jnp.exp(b_last[:, None] - b)
        S = jnp.exp(b_last)[..., None] * S + jnp.einsum('bjhd,bjhe->bhde', k_end, vi)
        return S, o

    S, o = lax.scan(step, s0.astype(jnp.float32), (qc, kc, vc, ac))
    o = o.swapaxes(0, 1).reshape(nb, n * C, H, dv)[:, :T]
    return o, S


def trunk_layer(x, c, pos, past_k, past_v, s_gla, s_ret, layer_idx,
                w_ada, b_ada, w_ffn1_up, w_ffn1_down, w_in,
                lambda_q1, lambda_k1, lambda_q2, lambda_k2, attn_subln,
                w_gla_gate, b_gla_gate, gla_norm,
                w_branch_a, w_branch_b, w_branch_c, w_out,
                w_ffn2_up, w_ffn2_down):
    nb, T, _ = x.shape
    mod = (jax.nn.silu(c) @ w_ada + b_ada).reshape(nb, N_MOD, D_MODEL)
    sh1, sc1, g1, sh2, sc2, g2, sh3, sc3, g3 = (mod[:, i] for i in range(N_MOD))

    x = x + 0.5 * (1.0 + g1)[:, None, :] * swiglu(modulate(x, sh1, sc1), w_ffn1_up, w_ffn1_down)

    h = modulate(x, sh2, sc2)
    (a_q, a_k, a_v, b_q, b_k, b_v, b_r, b_gl,
     c_q, c_k, c_v, c_g, m_a, m_b, m_c) = split_in(h @ w_in)

    a_q = a_q.reshape(nb, T, A_HEADS, 2, A_DK)
    a_k = a_k.reshape(nb, T, A_HEADS, 2 * A_DK)
    a_v = a_v.reshape(nb, T, A_HEADS, A_DV)
    if past_k is None:
        k_all, v_all, k_pos = a_k, a_v, pos
    else:
        k_all = jnp.concatenate([past_k, a_k], axis=1)
        v_all = jnp.concatenate([past_v, a_v], axis=1)
        k_pos = jnp.arange(k_all.shape[1], dtype=pos.dtype)
    lam_init = 0.8 - 0.6 * math.exp(-0.3 * layer_idx)
    lam = (jnp.exp(jnp.sum(lambda_q1.astype(jnp.float32) * lambda_k1.astype(jnp.float32)))
           - jnp.exp(jnp.sum(lambda_q2.astype(jnp.float32) * lambda_k2.astype(jnp.float32))) + lam_init)
    o_a = diff_attention(a_q, k_all.reshape(k_all.shape[:3] + (2, A_DK)), v_all, pos, k_pos, lam)
    o_a = (rms(o_a) * attn_subln * (1.0 - lam_init)).reshape(nb, T, A_HEADS * A_DV)

    gq = b_q.reshape(nb, T, B_HEADS, B_DK) * (B_DK ** -0.5)
    gk = b_k.reshape(nb, T, B_HEADS, B_DK)
    gv = b_v.reshape(nb, T, B_HEADS, B_DV)
    g_log = jax.nn.log_sigmoid((b_gl @ w_gla_gate + b_gla_gate).astype(jnp.float32)) / GLA_TAU
    o_b, s_gla_new = gated_linear_recurrence(gq, gk, gv, g_log.reshape(nb, T, B_HEADS, B_DK), s_gla)
    o_b = (rms(o_b) * gla_norm).astype(x.dtype) * jax.nn.silu(b_r.reshape(nb, T, B_HEADS, B_DV))
    o_b = o_b.reshape(nb, T, B_HEADS * B_DV)

    rq = rotary(c_q.reshape(nb, T, C_HEADS, C_DK), pos)
    rk = rotary(c_k.reshape(nb, T, C_HEADS, C_DK), pos) * (C_DK ** -0.5)
    rv = c_v.reshape(nb, T, C_HEADS, C_DV)
    log_gamma = jnp.log(1.0 - jnp.power(2.0, -5.0 - jnp.arange(C_HEADS, dtype=jnp.float32)))
    r_log = jnp.broadcast_to(log_gamma[None, None, :, None], (nb, T, C_HEADS, C_DK))
    o_c, s_ret_new = gated_linear_recurrence(rq, rk, rv, r_log, s_ret)
    o_c = rms(o_c).astype(x.dtype) * jax.nn.silu(c_g.reshape(nb, T, C_HEADS, C_DV))
    o_c = o_c.reshape(nb, T, C_HEADS * C_DV)

    merged = (jax.nn.sigmoid(m_a) * (o_a @ w_branch_a)
              + jax.nn.sigmoid(m_b) * (o_b @ w_branch_b)
              + jax.nn.sigmoid(m_c) * (o_c @ w_branch_c))
    x = x + (1.0 + g2)[:, None, :] * (merged @ w_out)

    x = x + 0.5 * (1.0 + g3)[:, None, :] * swiglu(modulate(x, sh3, sc3), w_ffn2_up, w_ffn2_down)
    return x, a_k, a_v, s_gla_new.astype(s_gla.dtype), s_ret_new.astype(s_ret.dtype)


def setup_inputs(seed: int = 0) -> dict:
    key = jax.random.key(seed)
    ks = jax.random.split(key, 32)

    def nrm(k, shape, s):
        return jax.random.normal(k, shape, jnp.float32) * s

    n_pages = PAST_LEN // PAGE_SIZE
    n_used = DEC_BATCH * n_pages
    n_pool = n_used + max(1, n_used // 4)
    page_table = jax.random.permutation(ks[0], n_pool)[:n_used].reshape(DEC_BATCH, n_pages).astype(jnp.int32)
    L, D = DEPTH, D_MODEL
    return {
        'x_prompt': nrm(ks[1], (BATCH, SEQ, D), 1.0),
        'x_sample': nrm(ks[2], (DEC_BATCH, DEC_SEQ, D), 1.0),
        'cache_k': nrm(ks[3], (L, n_pool, PAGE_SIZE, A_HEADS, 2 * A_DK), 1.0),
        'cache_v': nrm(ks[4], (L, n_pool, PAGE_SIZE, A_HEADS, A_DV), 1.0),
        'state_gla': nrm(ks[5], (L, DEC_BATCH, B_HEADS, B_DK, B_DV), 0.3),
        'state_ret': nrm(ks[6], (L, DEC_BATCH, C_HEADS, C_DK, C_DV), 0.3),
        'page_table': page_table,
        'c_prompt': nrm(ks[7], (BATCH, D), 1.0),
        'c_sample': nrm(ks[8], (DEC_BATCH, D), 1.0),
        'w_ada': nrm(ks[9], (L, D, N_MOD * D), 0.1 * D ** -0.5),
        'b_ada': nrm(ks[10], (L, N_MOD * D), 0.01),
        'w_ffn1_up': nrm(ks[11], (L, D, 2 * D_FF), D ** -0.5),
        'w_ffn1_down': nrm(ks[12], (L, D_FF, D), D_FF ** -0.5),
        'w_in': nrm(ks[13], (L, D, N_IN), D ** -0.5),
        'lambda_q1': nrm(ks[14], (L, A_DK), 0.1),
        'lambda_k1': nrm(ks[15], (L, A_DK), 0.1),
        'lambda_q2': nrm(ks[16], (L, A_DK), 0.1),
        'lambda_k2': nrm(ks[17], (L, A_DK), 0.1),
        'attn_subln': 1.0 + nrm(ks[18], (L, A_DV), 0.02),
        'w_gla_gate': nrm(ks[19], (L, GLA_GATE_RANK, B_HEADS * B_DK), GLA_GATE_RANK ** -0.5),
        'b_gla_gate': nrm(ks[20], (L, B_HEADS * B_DK), 0.01),
        'gla_norm': 1.0 + nrm(ks[21], (L, B_DV), 0.02),
        'w_branch_a': nrm(ks[22], (L, A_HEADS * A_DV, D), (A_HEADS * A_DV) ** -0.5),
        'w_branch_b': nrm(ks[23], (L, B_HEADS * B_DV, D), (B_HEADS * B_DV) ** -0.5),
        'w_branch_c': nrm(ks[24], (L, C_HEADS * C_DV, D), (C_HEADS * C_DV) ** -0.5),
        'w_out': nrm(ks[25], (L, D, D), D ** -0.5),
        'w_ffn2_up': nrm(ks[26], (L, D, 2 * D_FF), D ** -0.5),
        'w_ffn2_down': nrm(ks[27], (L, D_FF, D), D_FF ** -0.5),
        'final_norm': 1.0 + nrm(ks[28], (D,), 0.02),
    }


def reference(x_prompt, x_sample, cache_k, cache_v, state_gla, state_ret, page_table, c_prompt, c_sample,
              w_ada, b_ada, w_ffn1_up, w_ffn1_down, w_in, lambda_q1, lambda_k1, lambda_q2, lambda_k2,
              attn_subln, w_gla_gate, b_gla_gate, gla_norm, w_branch_a, w_branch_b, w_branch_c, w_out,
              w_ffn2_up, w_ffn2_down, final_norm):
    n_dec, n_pages = page_table.shape
    past_len = n_pages * PAGE_SIZE
    pos_prompt = jnp.arange(x_prompt.shape[1], dtype=jnp.int32)
    pos_sample = past_len + jnp.arange(x_sample.shape[1], dtype=jnp.int32)
    zero_gla = jnp.zeros((x_prompt.shape[0], B_HEADS, B_DK, B_DV), state_gla.dtype)
    zero_ret = jnp.zeros((x_prompt.shape[0], C_HEADS, C_DK, C_DV), state_ret.dtype)
    hp, hs = x_prompt, x_sample
    kp_l, vp_l, ks_l, vs_l, gp_l, gs_l, rp_l, rs_l = [], [], [], [], [], [], [], []
    for l in range(DEPTH):
        def run(x, c, pos, pk, pv, sg, sr):
            return trunk_layer(x, c, pos, pk, pv, sg, sr, l,
                               w_ada[l], b_ada[l], w_ffn1_up[l], w_ffn1_down[l], w_in[l],
                               lambda_q1[l], lambda_k1[l], lambda_q2[l], lambda_k2[l], attn_subln[l],
                               w_gla_gate[l], b_gla_gate[l], gla_norm[l],
                               w_branch_a[l], w_branch_b[l], w_branch_c[l], w_out[l],
                               w_ffn2_up[l], w_ffn2_down[l])
        hp, k_new, v_new, g_new, r_new = run(hp, c_prompt, pos_prompt, None, None, zero_gla, zero_ret)
        kp_l.append(k_new); vp_l.append(v_new); gp_l.append(g_new); rp_l.append(r_new)
        past_k = cache_k[l, page_table].reshape(n_dec, past_len, A_HEADS, 2 * A_DK)
        past_v = cache_v[l, page_table].reshape(n_dec, past_len, A_HEADS, A_DV)
        hs, k_new, v_new, g_new, r_new = run(hs, c_sample, pos_sample, past_k, past_v, state_gla[l], state_ret[l])
        ks_l.append(k_new); vs_l.append(v_new); gs_l.append(g_new); rs_l.append(r_new)
    y_prompt = rms(hp) * final_norm
    y_sample = rms(hs) * final_norm
    k_prompt = jnp.stack(kp_l)
    v_prompt = jnp.stack(vp_l)
    k_sample = jnp.stack(ks_l)
    v_sample = jnp.stack(vs_l)
    gla_prompt = jnp.stack(gp_l)
    gla_sample = jnp.stack(gs_l)
    ret_prompt = jnp.stack(rp_l)
    ret_sample = jnp.stack(rs_l)
    return (y_prompt, y_sample, k_prompt, v_prompt, k_sample, v_sample, gla_prompt, gla_sample, ret_prompt, ret_sample)
```

```python
import functools
import math
from typing import NamedTuple

import jax
import jax.numpy as jnp
from jax import lax
from jax.experimental import pallas as pl
from jax.experimental.pallas import tpu as pltpu

EPS = 1e-6
CHUNK = 64
GLA_TAU = 16.0
ROPE_BASE = 10000.0
N_MOD = 9
BF16 = jnp.bfloat16
F32 = jnp.float32
NEG = float(jnp.finfo(jnp.float32).min)

V7X_VMEM_BYTES = 64 * 2**20
VMEM_CEILING = V7X_VMEM_BYTES - 6 * 2**20
SUBLANES = 8
PAGES_PER_STEP = 4


def _params(sem, vmem_bytes):
    return pltpu.CompilerParams(
        dimension_semantics=sem,
        vmem_limit_bytes=int(min(VMEM_CEILING, max(vmem_bytes, 16 * 2**20))))


def _largest_divisor(n, cands):
    for c in cands:
        if n % c == 0:
            return c
    raise ValueError(f"no tile in {cands} divides {n}")


def _nbytes(shape, dtype):
    return math.prod(shape) * jnp.dtype(dtype).itemsize


class Group(NamedTuple):
    mod: jax.Array
    per_row: bool
    rows: int
    rows_per_batch: int
    tm: int


def _mod_spec(g, l, i, d_model, tn, mn):
    nb = d_model // tn
    if g.per_row:
        return pl.BlockSpec((None, g.tm, tn), lambda *a: (l, mn(*a)[0], i * nb + mn(*a)[1]))
    blocks_per_batch = g.rows_per_batch // g.tm
    return pl.BlockSpec((None, None, 1, tn),
                        lambda *a: (l, mn(*a)[0] // blocks_per_batch, 0, i * nb + mn(*a)[1]))


def _ada_kernel(c_ref, w_ref, b_ref, o_ref):
    c = c_ref[...]
    s = (c * jax.nn.sigmoid(c)).astype(BF16)
    o_ref[...] = jnp.dot(s, w_ref[...].astype(BF16), preferred_element_type=F32) + b_ref[...]


def _ada_all(c_rows, w_ada, b_ada):
    n_layers, d_model, n = w_ada.shape
    rows = c_rows.shape[0]
    tn = _largest_divisor(n, (1024, 512, 256, 128))
    vmem = 2 * (_nbytes((d_model, tn), F32) + _nbytes((rows, d_model), F32)) + _nbytes((d_model, tn), F32)
    return pl.pallas_call(
        _ada_kernel,
        grid=(n_layers, n // tn),
        in_specs=[pl.BlockSpec((rows, d_model), lambda l, j: (0, 0)),
                  pl.BlockSpec((None, d_model, tn), lambda l, j: (l, 0, j)),
                  pl.BlockSpec((None, 1, tn), lambda l, j: (l, 0, j))],
        out_specs=pl.BlockSpec((None, rows, tn), lambda l, j: (l, 0, j)),
        out_shape=jax.ShapeDtypeStruct((n_layers, rows, n), F32),
        compiler_params=_params(("arbitrary", "arbitrary"), vmem + 8 * 2**20),
        name="adaln",
    )(c_rows, w_ada, b_ada.reshape(n_layers, 1, n))


def _rmsmod_kernel(x_ref, sc_ref, sh_ref, o_ref):
    x = x_ref[...]
    r = lax.rsqrt(jnp.mean(x * x, axis=-1, keepdims=True) + EPS)
    o_ref[...] = ((x * r) * (1.0 + sc_ref[...]) + sh_ref[...]).astype(o_ref.dtype)


def _rmsmod(g, x, l, i_shift, i_scale):
    m, d = x.shape
    tm = min(g.tm, 512)
    gg = g._replace(tm=tm)
    mn = lambda i: (i, 0)
    return pl.pallas_call(
        _rmsmod_kernel,
        grid=(m // tm,),
        in_specs=[pl.BlockSpec((tm, d), lambda i: (i, 0)),
                  _mod_spec(gg, l, i_scale, d, d, mn),
                  _mod_spec(gg, l, i_shift, d, d, mn)],
        out_specs=pl.BlockSpec((tm, d), lambda i: (i, 0)),
        out_shape=jax.ShapeDtypeStruct((m, d), BF16),
        compiler_params=_params(("arbitrary",), 6 * _nbytes((tm, d), F32) + 8 * 2**20),
        name="rmsmod",
    )(x, g.mod, g.mod)


def _rmsgain_kernel(x_ref, g_ref, o_ref):
    x = x_ref[...]
    r = lax.rsqrt(jnp.mean(x * x, axis=-1, keepdims=True) + EPS)
    o_ref[...] = (x * r) * g_ref[...]


def _rmsgain(x, gain, tm):
    m, d = x.shape
    return pl.pallas_call(
        _rmsgain_kernel,
        grid=(m // tm,),
        in_specs=[pl.BlockSpec((tm, d), lambda i: (i, 0)),
                  pl.BlockSpec((1, d), lambda i: (0, 0))],
        out_specs=pl.BlockSpec((tm, d), lambda i: (i, 0)),
        out_shape=jax.ShapeDtypeStruct((m, d), F32),
        compiler_params=_params(("arbitrary",), 6 * _nbytes((tm, d), F32) + 8 * 2**20),
        name="final_norm",
    )(x, gain.reshape(1, d))


def _cast_weight(w_ref, wb_ref):
    k = w_ref.shape[0]
    rb = _largest_divisor(k, (512, 256, 128, 64, 32, 16))

    def body(i, c):
        r = pl.multiple_of(i * rb, rb)
        wb_ref[pl.ds(r, rb), :] = w_ref[pl.ds(r, rb), :].astype(BF16)
        return c

    lax.fori_loop(0, k // rb, body, 0)


def _cast_on_first_row_tile(pairs):
    @pl.when(pl.program_id(1) == 0)
    def _():
        for w_ref, wb_ref in pairs:
            _cast_weight(w_ref, wb_ref)


def _mm_plain_kernel(x_ref, w_ref, o_ref, wb_ref, *, act):
    _cast_on_first_row_tile([(w_ref, wb_ref)])
    acc = jnp.dot(x_ref[...], wb_ref[...], preferred_element_type=F32)
    if act == "sigmoid":
        acc = jax.nn.sigmoid(acc)
    o_ref[...] = acc.astype(o_ref.dtype)


def _mm_plain(x, w, l, col_off, ncols, tm, out_dtype, act=None):
    m, k = x.shape
    if ncols % 128 == 0:
        tn = _largest_divisor(math.gcd(ncols, col_off) if col_off else ncols, (1024, 512, 256, 128))
    else:
        assert col_off == 0 and ncols == w.shape[2]
        tn = ncols
    ob = col_off // tn
    vmem = (2 * (_nbytes((tm, k), BF16) + _nbytes((k, tn), F32) + _nbytes((tm, tn), out_dtype))
            + _nbytes((k, tn), BF16) + 2 * _nbytes((tm, tn), F32))
    return pl.pallas_call(
        functools.partial(_mm_plain_kernel, act=act),
        grid=(ncols // tn, m // tm),
        in_specs=[pl.BlockSpec((tm, k), lambda j, i: (i, 0)),
                  pl.BlockSpec((None, k, tn), lambda j, i: (l, 0, ob + j))],
        out_specs=pl.BlockSpec((tm, tn), lambda j, i: (i, j)),
        out_shape=jax.ShapeDtypeStruct((m, ncols), out_dtype),
        scratch_shapes=[pltpu.VMEM((k, tn), BF16)],
        compiler_params=_params(("arbitrary", "arbitrary"), vmem + 4 * 2**20),
        name="mm_plain",
    )(x, w)


def _mm_swiglu_kernel(x_ref, wa_ref, wg_ref, o_ref, wab_ref, wgb_ref):
    _cast_on_first_row_tile([(wa_ref, wab_ref), (wg_ref, wgb_ref)])
    x = x_ref[...]
    a = jnp.dot(x, wab_ref[...], preferred_element_type=F32)
    b = jnp.dot(x, wgb_ref[...], preferred_element_type=F32)
    o_ref[...] = ((a * jax.nn.sigmoid(a)) * b).astype(o_ref.dtype)


def _mm_swiglu(x, w_up, l, tm):
    m, k = x.shape
    f = w_up.shape[2] // 2
    tn = 512 if f >= 512 else f
    nt = pl.cdiv(f, tn)
    off = lambda j, base=0: pl.multiple_of(base + jnp.minimum(j * tn, f - tn), 128)
    vmem = (2 * (_nbytes((tm, k), BF16) + 2 * _nbytes((k, tn), F32) + _nbytes((tm, tn), BF16))
            + 2 * _nbytes((k, tn), BF16) + 4 * _nbytes((tm, tn), F32))
    return pl.pallas_call(
        _mm_swiglu_kernel,
        grid=(nt, m // tm),
        in_specs=[pl.BlockSpec((tm, k), lambda j, i: (i, 0)),
                  pl.BlockSpec((pl.Squeezed(), pl.Element(k), pl.Element(tn)), lambda j, i: (l, 0, off(j))),
                  pl.BlockSpec((pl.Squeezed(), pl.Element(k), pl.Element(tn)), lambda j, i: (l, 0, off(j, f)))],
        out_specs=pl.BlockSpec((pl.Element(tm), pl.Element(tn)), lambda j, i: (i * tm, off(j))),
        out_shape=jax.ShapeDtypeStruct((m, f), BF16),
        scratch_shapes=[pltpu.VMEM((k, tn), BF16), pltpu.VMEM((k, tn), BF16)],
        compiler_params=_params(("arbitrary", "arbitrary"), vmem + 4 * 2**20),
        name="mm_swiglu",
    )(x, w_up, w_up)


def _mm_res_kernel(x_ref, w_ref, res_ref, g_ref, o_ref, wb_ref, *, half):
    _cast_on_first_row_tile([(w_ref, wb_ref)])
    acc = jnp.dot(x_ref[...], wb_ref[...], preferred_element_type=F32)
    gate = 1.0 + g_ref[...]
    if half:
        gate = 0.5 * gate
    o_ref[...] = res_ref[...] + gate * acc


def _mm_res(g, x, w, l, res, i_gate, half, tm, tn):
    m, k = x.shape
    d = w.shape[2]
    gg = g._replace(tm=tm)
    vmem = (2 * (_nbytes((tm, k), BF16) + _nbytes((k, tn), F32) + 2 * _nbytes((tm, tn), F32))
            + _nbytes((k, tn), BF16) + 2 * _nbytes((tm, tn), F32))
    return pl.pallas_call(
        functools.partial(_mm_res_kernel, half=half),
        grid=(d // tn, m // tm),
        in_specs=[pl.BlockSpec((tm, k), lambda j, i: (i, 0)),
                  pl.BlockSpec((None, k, tn), lambda j, i: (l, 0, j)),
                  pl.BlockSpec((tm, tn), lambda j, i: (i, j)),
                  _mod_spec(gg, l, i_gate, d, tn, lambda j, i: (i, j))],
        out_specs=pl.BlockSpec((tm, tn), lambda j, i: (i, j)),
        out_shape=jax.ShapeDtypeStruct((m, d), F32),
        scratch_shapes=[pltpu.VMEM((k, tn), BF16)],
        compiler_params=_params(("arbitrary", "arbitrary"), vmem + 4 * 2**20),
        name="mm_res",
    )(x, w, res, g.mod)


def _mm_merge_kernel(xa_ref, xb_ref, xc_ref, wa_ref, wb_ref, wc_ref, ga_ref, gb_ref, gc_ref,
                     o_ref, wab_ref, wbb_ref, wcb_ref):
    _cast_on_first_row_tile([(wa_ref, wab_ref), (wb_ref, wbb_ref), (wc_ref, wcb_ref)])
    acc = ga_ref[...].astype(F32) * jnp.dot(xa_ref[...], wab_ref[...], preferred_element_type=F32)
    acc = acc + gb_ref[...].astype(F32) * jnp.dot(xb_ref[...], wbb_ref[...], preferred_element_type=F32)
    acc = acc + gc_ref[...].astype(F32) * jnp.dot(xc_ref[...], wcb_ref[...], preferred_element_type=F32)
    o_ref[...] = acc.astype(o_ref.dtype)


def _mm_merge(xa, xb, xc, wa, wb, wc, l, gates, tm):
    m = xa.shape[0]
    d = wa.shape[2]
    tn = _largest_divisor(d, (512, 256, 128))
    nb = d // tn
    ks =(xa.shape[1], xb.shape[1], xc.shape[1])
    vmem = sum(2 * (_nbytes((tm, k), BF16) + _nbytes((k, tn), F32)) + _nbytes((k, tn), BF16) for k in ks)
    vmem += 2 * 4 * _nbytes((tm, tn), BF16) + 4 * _nbytes((tm, tn), F32)
    x_spec = lambda k: pl.BlockSpec((tm, k), lambda j, i: (i, 0))
    w_spec = lambda k: pl.BlockSpec((None, k, tn), lambda j, i: (l, 0, j))
    g_spec = lambda s: pl.BlockSpec((tm, tn), lambda j, i: (i, s * nb + j))
    return pl.pallas_call(
        _mm_merge_kernel,
        grid=(nb, m // tm),
        in_specs=[x_spec(ks[0]), x_spec(ks[1]), x_spec(ks[2]),
                  w_spec(ks[0]), w_spec(ks[1]), w_spec(ks[2]),
                  g_spec(0), g_spec(1), g_spec(2)],
        out_specs=pl.BlockSpec((tm, tn), lambda j, i: (i, j)),
        out_shape=jax.ShapeDtypeStruct((m, d), BF16),
        scratch_shapes=[pltpu.VMEM((k, tn), BF16) for k in ks],
        compiler_params=_params(("arbitrary", "arbitrary"), vmem + 4 * 2**20),
        name="mm_merge",
    )(xa, xb, xc, wa, wb, wc, gates, gates, gates)


def _lambda_value(lam_ref, lam_init):
    lv = lam_ref[...]
    e1 = jnp.exp(jnp.sum(lv[0:1] * lv[1:2], axis=-1, keepdims=True))
    e2 = jnp.exp(jnp.sum(lv[2:3] * lv[3:4], axis=-1, keepdims=True))
    return e1 - e2 + lam_init


def _subln(o, sub_ref, lam_init):
    r = lax.rsqrt(jnp.mean(o * o, axis=-1, keepdims=True) + EPS)
    return ((o * r) * sub_ref[...]) * (1.0 - lam_init)


def _attn_prompt_kernel(lam_ref, sub_ref, q_ref, k_ref, v_ref, o_ref, kb_ref, vb_ref, *, tq, dk, lam_init):
    t = q_ref.shape[0]
    scale = dk ** -0.5
    kb_ref[...] = k_ref[...].astype(BF16)
    vb_ref[...] = v_ref[...].astype(BF16)
    lam = _lambda_value(lam_ref, lam_init)
    dims = (((1,), (1,)), ((), ()))
    for qi in range(t // tq):
        n_keys = (qi + 1) * tq
        q = q_ref[qi * tq:(qi + 1) * tq, :].astype(BF16)
        row = qi * tq + lax.broadcasted_iota(jnp.int32, (tq, n_keys), 0)
        col = lax.broadcasted_iota(jnp.int32, (tq, n_keys), 1)
        visible = col <= row
        probs = []
        for n in range(2):
            s = lax.dot_general(q[:, n * dk:(n + 1) * dk], kb_ref[0:n_keys, n * dk:(n + 1) * dk], dims,
                                preferred_element_type=F32) * scale
            s = jnp.where(visible, s, NEG)
            e = jnp.exp(s - jnp.max(s, axis=-1, keepdims=True))
            probs.append(e * (1.0 / jnp.sum(e, axis=-1, keepdims=True)))
        p = probs[0] - lam * probs[1]
        o = jnp.dot(p.astype(BF16), vb_ref[0:n_keys, :], preferred_element_type=F32)
        o_ref[qi * tq:(qi + 1) * tq, :] = _subln(o, sub_ref, lam_init).astype(o_ref.dtype)


def _attn_prompt(q, k, v, lam_vecs, subln, l, nb, t, heads, dk, dv, lam_init):
    tq = _largest_divisor(t, (256, 128, 64, 32, 16, 8))
    blk = lambda w: pl.BlockSpec((t, w), lambda b, h: (b, h))
    vmem = 2 * (2 * _nbytes((t, 2 * dk), F32) + _nbytes((t, dv), F32) + _nbytes((t, dv), BF16))
    vmem += _nbytes((t, 2 * dk), BF16) + _nbytes((t, dv), BF16) + 8 * _nbytes((tq, t), F32)
    return pl.pallas_call(
        functools.partial(_attn_prompt_kernel, tq=tq, dk=dk, lam_init=lam_init),
        grid=(nb, heads),
        in_specs=[pl.BlockSpec((None, 4, dk), lambda b, h: (l, 0, 0)),
                  pl.BlockSpec((None, 1, dv), lambda b, h: (l, 0, 0)),
                  blk(2 * dk), blk(2 * dk), blk(dv)],
        out_specs=blk(dv),
        out_shape=jax.ShapeDtypeStruct((nb * t, heads * dv), BF16),
        scratch_shapes=[pltpu.VMEM((t, 2 * dk), BF16), pltpu.VMEM((t, dv), BF16)],
        compiler_params=_params(("arbitrary", "arbitrary"), vmem + 8 * 2**20),
        name="attn_prompt",
    )(lam_vecs, subln, q, k, v)


def _attn_sample_kernel(pt_ref, lam_ref, sub_ref, q_ref, kn_ref, vn_ref, *rest,
                        n_pages, heads, dk, lam_init):
    k_refs = rest[:n_pages]
    v_refs = rest[n_pages:2 * n_pages]
    o_ref, m_ref, l_ref, acc_ref, mask_ref = rest[2 * n_pages:]
    p = pl.program_id(1)
    scale = dk ** -0.5
    nq = q_ref.shape[0]
    dims = (((1,), (1,)), ((), ()))
    q = q_ref[...].astype(BF16)

    def scores(keys):
        s = [lax.dot_general(q[:, n * dk:(n + 1) * dk], keys[:, n * dk:(n + 1) * dk], dims,
                             preferred_element_type=F32) for n in range(2)]
        return jnp.concatenate(s, axis=0) * scale

    @pl.when(p == 0)
    def _():
        n_cols = mask_ref.shape[1]
        r = lax.broadcasted_iota(jnp.int32, (2 * nq, n_cols), 0)
        c = lax.broadcasted_iota(jnp.int32, (2 * nq, n_cols), 1)
        mask_ref[...] = jnp.where((r % heads) == (c % heads), 1.0, 0.0)
        r = lax.broadcasted_iota(jnp.int32, (2 * nq, nq), 0) % nq
        c = lax.broadcasted_iota(jnp.int32, (2 * nq, nq), 1)
        ok = ((r % heads) == (c % heads)) & ((c // heads) <= (r // heads))
        s = jnp.where(ok, scores(kn_ref[...].astype(BF16)), NEG)
        m = jnp.max(s, axis=-1, keepdims=True)
        e = jnp.exp(s - m)
        m_ref[...] = m
        l_ref[...] = jnp.sum(e, axis=-1, keepdims=True)
        acc_ref[...] = jnp.dot(e.astype(BF16), vn_ref[...].astype(BF16), preferred_element_type=F32)

    rows = k_refs[0].shape[0] * k_refs[0].shape[1]
    keys = jnp.concatenate([r[...].reshape(rows, 2 * dk).astype(BF16) for r in k_refs], axis=0)
    vals = jnp.concatenate([r[...].reshape(rows, v_refs[0].shape[2]).astype(BF16) for r in v_refs], axis=0)
    s = jnp.where(mask_ref[...] > 0.5, scores(keys), NEG)
    m_old = m_ref[...]
    m_new = jnp.maximum(m_old, jnp.max(s, axis=-1, keepdims=True))
    alpha = jnp.exp(m_old - m_new)
    e = jnp.exp(s - m_new)
    m_ref[...] = m_new
    l_ref[...] = alpha * l_ref[...] + jnp.sum(e, axis=-1, keepdims=True)
    acc_ref[...] = alpha * acc_ref[...] + jnp.dot(e.astype(BF16), vals, preferred_element_type=F32)

    @pl.when(p == pl.num_programs(1) - 1)
    def _():
        o = acc_ref[...] * (1.0 / l_ref[...])
        o = o[0:nq] - _lambda_value(lam_ref, lam_init) * o[nq:2 * nq]
        o_ref[...] = _subln(o, sub_ref, lam_init).astype(o_ref.dtype)


def _attn_sample(q, kn, vn, cache_k, cache_v, page_table, lam_vecs, subln, l, lam_init):
    nb, nq, _ = q.shape
    _, _, page, heads, dk2 = cache_k.shape
    dv = cache_v.shape[4]
    dk = dk2 // 2
    n_used = page_table.shape[1]
    g = _largest_divisor(n_used, (PAGES_PER_STEP, 2, 1))
    page_spec = lambda i, w: pl.BlockSpec((None, None, page, heads, w),
                                          lambda b, p, pt: (l, pt[b, p * g + i], 0, 0, 0))
    row_spec = lambda w: pl.BlockSpec((None, nq, w), lambda b, p, pt: (b, 0, 0))
    vmem = 2 * g * 2 * (_nbytes((page, SUBLANES, max(dk2, dv)), F32))
    vmem += 6 * _nbytes((2 * nq, g * page * heads), F32) + 2 * _nbytes((g * page * heads, dk2 + dv), BF16)
    return pl.pallas_call(
        functools.partial(_attn_sample_kernel, n_pages=g, heads=heads, dk=dk, lam_init=lam_init),
        grid_spec=pltpu.PrefetchScalarGridSpec(
            num_scalar_prefetch=1,
            grid=(nb, n_used // g),
            in_specs=[pl.BlockSpec((None, 4, dk), lambda b, p, pt: (l, 0, 0)),
                      pl.BlockSpec((None, 1, dv), lambda b, p, pt: (l, 0, 0)),
                      row_spec(dk2), row_spec(dk2), row_spec(dv)]
                     + [page_spec(i, dk2) for i in range(g)] + [page_spec(i, dv) for i in range(g)],
            out_specs=row_spec(dv),
            scratch_shapes=[pltpu.VMEM((2 * nq, 1), F32), pltpu.VMEM((2 * nq, 1), F32),
                            pltpu.VMEM((2 * nq, dv), F32), pltpu.VMEM((2 * nq, g * page * heads), F32)]),
        out_shape=jax.ShapeDtypeStruct((nb, nq, dv), BF16),
        compiler_params=_params(("arbitrary", "arbitrary"), vmem + 8 * 2**20),
        name="attn_sample",
    )(page_table, lam_vecs, subln, q, kn, vn, *([cache_k] * g), *([cache_v] * g))


def _split3(a):
    hi = a.astype(BF16)
    r1 = a - hi.astype(F32)
    mid = r1.astype(BF16)
    lo = (r1 - mid.astype(F32)).astype(BF16)
    return hi, mid, lo


def _exact_dot(sel, a, dims):
    out = None
    for part in _split3(a):
        d = lax.dot_general(sel, part, dims, preferred_element_type=F32)
        out = d if out is None else out + d
    return out


def _rec_kernel(*refs, chunk, heads, dk, dv, is_gla, has_state):
    it = iter(refs)
    q_ref, k_ref, v_ref, r_ref = next(it), next(it), next(it), next(it)
    if is_gla:
        gl_ref, wg_ref, bg_ref, norm_ref = next(it), next(it), next(it), next(it)
    else:
        cos_ref, sin_ref, loga_ref = next(it), next(it), next(it)
    s0_ref = next(it) if has_state else None
    o_ref, s_out_ref, s_ref = next(it), next(it), next(it)
    t = pl.program_id(1)
    rows = q_ref.shape[0]
    n_chunks = rows // chunk

    @pl.when(t == 0)
    def _():
        s_ref[...] = s0_ref[...] if has_state else jnp.zeros(s_ref.shape, F32)

    q, k, v = q_ref[...], k_ref[...], v_ref[...]
    if is_gla:
        z = jnp.dot(gl_ref[...].astype(BF16), wg_ref[...].astype(BF16), preferred_element_type=F32) + bg_ref[...]
        a = jax.nn.log_sigmoid(z) / GLA_TAU
        q = q * (dk ** -0.5)
    else:
        a = jnp.broadcast_to(loga_ref[...], q.shape)
        half = dk // 2
        width = q.shape[1]
        lane = lax.broadcasted_iota(jnp.int32, q.shape, 1)
        first_half = (lane % dk) < half
        cos, sin = cos_ref[...], sin_ref[...]

        def rot(x):
            other = jnp.where(first_half, pltpu.roll(x, width - half, 1), pltpu.roll(x, half, 1))
            return x * cos + other * sin

        q = rot(q)
        k = rot(k) * (dk ** -0.5)

    ri = lax.broadcasted_iota(jnp.int32, (rows, rows), 0)
    ci = lax.broadcasted_iota(jnp.int32, (rows, rows), 1)
    same_chunk = (ri // chunk) == (ci // chunk)
    causal = same_chunk & (ci <= ri)
    mm = (((1,), (0,)), ((), ()))
    b = _exact_dot(jnp.where(causal, 1.0, 0.0).astype(BF16), a, mm)
    b_last = _exact_dot(jnp.where(same_chunk, 1.0, 0.0).astype(BF16), a, mm)
    q_t = q * jnp.exp(b)
    k_t = k * jnp.exp(-b)
    k_end = k * jnp.exp(b_last - b)
    ones = jnp.ones((chunk, dv), BF16)
    tt = (((0,), (0,)), ((), ()))
    nt = (((1,), (1,)), ((), ()))
    outs = []
    for h in range(heads):
        sk = slice(h * dk, (h + 1) * dk)
        sv = slice(h * dv, (h + 1) * dv)
        qh, kh, keh, vh = q_t[:, sk].astype(BF16), k_t[:, sk].astype(BF16), k_end[:, sk].astype(BF16), v[:, sv].astype(BF16)
        attn = jnp.where(causal, lax.dot_general(qh, kh, nt, preferred_element_type=F32), 0.0)
        o = jnp.dot(attn.astype(BF16), vh, preferred_element_type=F32)
        state = s_ref[h]
        inter = []
        for c in range(n_chunks):
            rs = slice(c * chunk, (c + 1) * chunk)
            inter.append(jnp.dot(qh[rs], state.astype(BF16), preferred_element_type=F32))
            decay = jnp.exp(_exact_dot_t(a[rs, sk], ones, tt))
            state = decay * state + lax.dot_general(keh[rs], vh[rs], tt, preferred_element_type=F32)
        s_ref[h] = state
        o = o + (inter[0] if n_chunks == 1 else jnp.concatenate(inter, axis=0))
        o = o * lax.rsqrt(jnp.mean(o * o, axis=-1, keepdims=True) + EPS)
        if is_gla:
            o = o * norm_ref[...]
        gate = r_ref[:, sv]
        outs.append(o * (gate * jax.nn.sigmoid(gate)))
    o_ref[...] = jnp.concatenate(outs, axis=1).astype(o_ref.dtype)

    @pl.when(t == pl.num_programs(1) - 1)
    def _():
        s_out_ref[...] = s_ref[...]


def _exact_dot_t(a, ones, dims):
    out = None
    for part in _split3(a):
        d = lax.dot_general(part, ones, dims, preferred_element_type=F32)
        out = d if out is None else out + d
    return out


def _recurrence(seg, nb, t, heads, dk, dv, l, *, gla=None, ret=None, state=None):
    is_gla = gla is not None
    chunk = min(CHUNK, t)
    rows = _largest_divisor(t, (4 * chunk, 2 * chunk, chunk))
    nblk = t // rows
    wq, wv = heads * dk, heads * dv
    assert wv == 2 * wq
    row_blk = lambda w, j: pl.BlockSpec((rows, w), lambda b, i: (b * nblk + i, j))
    in_specs = [row_blk(wq, 0), row_blk(wq, 1), row_blk(wv, 1), row_blk(wv, 2)]
    args = [seg, seg, seg, seg]
    if is_gla:
        gl, w_gate, b_gate, norm = gla
        rank = gl.shape[1]
        in_specs += [pl.BlockSpec((rows, rank), lambda b, i: (b * nblk + i, 0)),
                     pl.BlockSpec((None, rank, wq), lambda b, i: (l, 0, 0)),
                     pl.BlockSpec((None, 1, wq), lambda b, i: (l, 0, 0)),
                     pl.BlockSpec((None, 1, dv), lambda b, i: (l, 0, 0))]
        args += [gl, w_gate, b_gate, norm]
    else:
        cos, sin, loga = ret
        in_specs += [pl.BlockSpec((rows, wq), lambda b, i: (i, 0)),
                     pl.BlockSpec((rows, wq), lambda b, i: (i, 0)),
                     pl.BlockSpec((1, wq), lambda b, i: (0, 0))]
        args += [cos, sin, loga]
    if state is not None:
        in_specs.append(pl.BlockSpec((None, None, heads, dk, dv), lambda b, i: (l, b, 0, 0, 0)))
        args.append(state)
    vmem = 2 * (2 * _nbytes((rows, wq), F32) + 2 * _nbytes((rows, wv), F32) + _nbytes((rows, wv), BF16))
    vmem += 16 * _nbytes((rows, wq), F32) + 8 * _nbytes((rows, rows), F32)
    return pl.pallas_call(
        functools.partial(_rec_kernel, chunk=chunk, heads=heads, dk=dk, dv=dv, is_gla=is_gla,
                          has_state=state is not None),
        grid=(nb, nblk),
        in_specs=in_specs,
        out_specs=[pl.BlockSpec((rows, wv), lambda b, i: (b * nblk + i, 0)),
                   pl.BlockSpec((None, heads, dk, dv), lambda b, i: (b, 0, 0, 0))],
        out_shape=[jax.ShapeDtypeStruct((nb * t, wv), BF16),
                   jax.ShapeDtypeStruct((nb, heads, dk, dv), F32)],
        scratch_shapes=[pltpu.VMEM((heads, dk, dv), F32)],
        compiler_params=_params(("arbitrary", "arbitrary"), vmem + 8 * 2**20),
        name="gla" if is_gla else "retention",
    )(*args)


def _rope_tables(pos, heads, dk):
    half = dk // 2
    freq = 1.0 / (ROPE_BASE ** jnp.linspace(0.0, 1.0, half, dtype=F32))
    ang = pos.astype(F32)[:, None] * freq[None, :]
    cos, sin = jnp.cos(ang), jnp.sin(ang)
    cos = jnp.tile(jnp.concatenate([cos, cos], axis=-1), (1, heads))
    sin = jnp.tile(jnp.concatenate([-sin, sin], axis=-1), (1, heads))
    return cos, sin


def kernel(x_prompt, x_sample, cache_k, cache_v, state_gla, state_ret, page_table, c_prompt, c_sample,
           w_ada, b_ada, w_ffn1_up, w_ffn1_down, w_in, lambda_q1, lambda_k1, lambda_q2, lambda_k2,
           attn_subln, w_gla_gate, b_gla_gate, gla_norm, w_branch_a, w_branch_b, w_branch_c, w_out,
           w_ffn2_up, w_ffn2_down, final_norm):
    nb_p, t_p, d = x_prompt.shape
    nb_s, t_s, _ = x_sample.shape
    n_layers = w_ada.shape[0]
    a_heads, a_dk2 = cache_k.shape[3:]
    a_dk, a_dv = a_dk2 // 2, cache_v.shape[4]
    b_heads, b_dk, b_dv = state_gla.shape[2:]
    c_heads, c_dk, c_dv = state_ret.shape[2:]
    rank = w_gla_gate.shape[1]
    n_pages = page_table.shape[1]
    past_len = n_pages * cache_k.shape[2]

    w_a = a_heads * a_dk2
    w_av = a_heads * a_dv
    w_b = 2 * b_heads * b_dk + 2 * b_heads * b_dv
    w_c = 2 * c_heads * c_dk + 2 * c_heads * c_dv
    off_b = 2 * w_a + w_av
    off_gl = off_b + w_b
    off_c = off_gl + rank
    off_m = off_c + w_c
    assert off_m + 3 * d == w_in.shape[2]
    w_gl = w_in[:, :, off_gl:off_c]
    w_tail = w_in[:, :, off_c:]

    n_c = nb_p + nb_s
    c_rows = jnp.concatenate([c_prompt, c_sample, jnp.zeros((-n_c % SUBLANES, d), F32)], axis=0)
    mod = _ada_all(c_rows, w_ada, b_ada)
    m_p, m_s = nb_p * t_p, nb_s * t_s
    grp_p = Group(mod[:, :nb_p].reshape(n_layers, nb_p, 1, N_MOD * d), False, m_p, t_p,
                  _largest_divisor(t_p, (1024, 512, 256, 128, 64, 32, 16)))
    grp_s = Group(jnp.repeat(mod[:, nb_p:n_c], t_s, axis=1), True, m_s, t_s, m_s)

    lam_vecs = jnp.stack([lambda_q1, lambda_k1, lambda_q2, lambda_k2], axis=1)
    subln = attn_subln.reshape(n_layers, 1, a_dv)
    b_gate = b_gla_gate.reshape(n_layers, 1, b_heads * b_dk)
    norm_b = gla_norm.reshape(n_layers, 1, b_dv)
    log_gamma = jnp.log(1.0 - jnp.power(2.0, -5.0 - jnp.arange(c_heads, dtype=F32)))
    loga_c = jnp.repeat(log_gamma, c_dk).reshape(1, c_heads * c_dk)
    rope_p = _rope_tables(jnp.arange(t_p, dtype=jnp.int32), c_heads, c_dk)
    rope_s = _rope_tables(past_len + jnp.arange(t_s, dtype=jnp.int32), c_heads, c_dk)

    def ffn(g, x, l, w_up, w_down, i0):
        h = _rmsmod(g, x, l, i0, i0 + 1)
        act = _mm_swiglu(h, w_up, l, g.tm)
        tm = min(g.tm, 512)
        return _mm_res(g, act, w_down, l, x, i0 + 2, True, tm, _largest_divisor(d, (512, 256, 128)))

    def layer(g, x, l, nb, t, rope, state_b, state_c, paged):
        lam_init = 0.8 - 0.6 * math.exp(-0.3 * l)
        x = ffn(g, x, l, w_ffn1_up, w_ffn1_down, 0)
        h = _rmsmod(g, x, l, 3, 4)
        proj = lambda w, off, n, dt=F32, act=None: _mm_plain(h, w, l, off, n, g.tm, dt, act)
        a_q, a_k, a_v = proj(w_in, 0, w_a), proj(w_in, w_a, w_a), proj(w_in, 2 * w_a, w_av)
        seg_b = proj(w_in, off_b, w_b)
        gl = proj(w_gl, 0, rank)
        seg_c = proj(w_tail, 0, w_c)
        gates = proj(w_tail, w_c, 3 * d, BF16, "sigmoid")
        if paged:
            rows = lambda z: z.reshape(nb, t * a_heads, z.shape[1] // a_heads)
            o_a = _attn_sample(rows(a_q), rows(a_k), rows(a_v), cache_k, cache_v, page_table,
                               lam_vecs, subln, l, lam_init).reshape(nb * t, w_av)
        else:
            o_a = _attn_prompt(a_q, a_k, a_v, lam_vecs, subln, l, nb, t, a_heads, a_dk, a_dv, lam_init)
        o_b, s_b = _recurrence(seg_b, nb, t, b_heads, b_dk, b_dv, l,
                               gla=(gl, w_gla_gate, b_gate, norm_b), state=state_b)
        o_c, s_c = _recurrence(seg_c, nb, t, c_heads, c_dk, c_dv, l, ret=rope + (loga_c,), state=state_c)
        merged = _mm_merge(o_a, o_b, o_c, w_branch_a, w_branch_b, w_branch_c, l, gates, g.tm)
        x = _mm_res(g, merged, w_out, l, x, 5, False, g.tm, _largest_divisor(d, (512, 256, 128)))
        x = ffn(g, x, l, w_ffn2_up, w_ffn2_down, 6)
        return x, a_k, a_v, s_b, s_c

    hp = x_prompt.reshape(m_p, d)
    hs = x_sample.reshape(m_s, d)
    outs_p, outs_s = [], []
    for l in range(n_layers):
        hp, *new_p = layer(grp_p, hp, l, nb_p, t_p, rope_p, None, None, False)
        hs, *new_s = layer(grp_s, hs, l, nb_s, t_s, rope_s, state_gla, state_ret, True)
        outs_p.append(new_p)
        outs_s.append(new_s)
    y_p = _rmsgain(hp, final_norm, min(grp_p.tm, 512)).reshape(nb_p, t_p, d)
    y_s = _rmsgain(hs, final_norm, m_s).reshape(nb_s, t_s, d)

    def stacked(outs, i, shape):
        return jnp.stack([o[i] for o in outs]).reshape((n_layers,) + shape)

    return (y_p, y_s,
            stacked(outs_p, 0, (nb_p, t_p, a_heads, a_dk2)), stacked(outs_p, 1, (nb_p, t_p, a_heads, a_dv)),
            stacked(outs_s, 0, (nb_s, t_s, a_heads, a_dk2)), stacked(outs_s, 1, (nb_s, t_s, a_heads, a_dv)),
            stacked(outs_p, 2, (nb_p, b_heads, b_dk, b_dv)), stacked(outs_s, 2, (nb_s, b_heads, b_dk, b_dv)),
            stacked(outs_p, 3, (nb_p, c_heads, c_dk, c_dv)), stacked(outs_s, 3, (nb_s, c_heads, c_dk, c_dv)))
```

```python
import functools
import math
from typing import NamedTuple

import jax
import jax.numpy as jnp
from jax import lax
from jax.experimental import pallas as pl
from jax.experimental.pallas import tpu as pltpu

EPS = 1e-6
CHUNK = 64
GLA_TAU = 16.0
ROPE_BASE = 10000.0
N_MOD = 9
BF16 = jnp.bfloat16
F32 = jnp.float32
NEG = float(jnp.finfo(jnp.float32).min)

V7X_VMEM_BYTES = 64 * 2**20
VMEM_CEILING = V7X_VMEM_BYTES - 6 * 2**20
SUBLANES = 8
PAGES_PER_STEP = 8


def _params(sem, vmem_bytes):
    return pltpu.CompilerParams(
        dimension_semantics=sem,
        vmem_limit_bytes=int(min(VMEM_CEILING, max(vmem_bytes, 16 * 2**20))))


def _largest_divisor(n, cands):
    for c in cands:
        if n % c == 0:
            return c
    raise ValueError(f"no tile in {cands} divides {n}")


def _nbytes(shape, dtype):
    return math.prod(shape) * jnp.dtype(dtype).itemsize


class Group(NamedTuple):
    mod: jax.Array
    per_row: bool
    rows: int
    rows_per_batch: int
    tm: int


def _mod_spec(g, l, i, d_model, tn, mn):
    nb = d_model // tn
    if g.per_row:
        return pl.BlockSpec((None, g.tm, tn), lambda *a: (l, mn(*a)[0], i * nb + mn(*a)[1]))
    blocks_per_batch = g.rows_per_batch // g.tm
    return pl.BlockSpec((None, None, 1, tn),
                        lambda *a: (l, mn(*a)[0] // blocks_per_batch, 0, i * nb + mn(*a)[1]))


def _ada_kernel(c_ref, w_ref, b_ref, o_ref):
    c = c_ref[...]
    s = (c * jax.nn.sigmoid(c)).astype(BF16)
    o_ref[...] = jnp.dot(s, w_ref[...].astype(BF16), preferred_element_type=F32) + b_ref[...]


def _ada_all(c_rows, w_ada, b_ada):
    n_layers, d_model, n = w_ada.shape
    rows = c_rows.shape[0]
    tn = _largest_divisor(n, (1024, 512, 256, 128))
    vmem = 2 * (_nbytes((d_model, tn), F32) + _nbytes((rows, d_model), F32)) + _nbytes((d_model, tn), F32)
    return pl.pallas_call(
        _ada_kernel,
        grid=(n_layers, n // tn),
        in_specs=[pl.BlockSpec((rows, d_model), lambda l, j: (0, 0)),
                  pl.BlockSpec((None, d_model, tn), lambda l, j: (l, 0, j)),
                  pl.BlockSpec((None, 1, tn), lambda l, j: (l, 0, j))],
        out_specs=pl.BlockSpec((None, rows, tn), lambda l, j: (l, 0, j)),
        out_shape=jax.ShapeDtypeStruct((n_layers, rows, n), F32),
        compiler_params=_params(("arbitrary", "arbitrary"), vmem + 8 * 2**20),
        name="adaln",
    )(c_rows, w_ada, b_ada.reshape(n_layers, 1, n))


def _rmsmod_kernel(x_ref, sc_ref, sh_ref, o_ref):
    x = x_ref[...]
    r = lax.rsqrt(jnp.mean(x * x, axis=-1, keepdims=True) + EPS)
    o_ref[...] = ((x * r) * (1.0 + sc_ref[...]) + sh_ref[...]).astype(o_ref.dtype)


def _rmsmod(g, x, l, i_shift, i_scale):
    m, d = x.shape
    tm = min(g.tm, 512)
    gg = g._replace(tm=tm)
    mn = lambda i: (i, 0)
    return pl.pallas_call(
        _rmsmod_kernel,
        grid=(m // tm,),
        in_specs=[pl.BlockSpec((tm, d), lambda i: (i, 0)),
                  _mod_spec(gg, l, i_scale, d, d, mn),
                  _mod_spec(gg, l, i_shift, d, d, mn)],
        out_specs=pl.BlockSpec((tm, d), lambda i: (i, 0)),
        out_shape=jax.ShapeDtypeStruct((m, d), BF16),
        compiler_params=_params(("arbitrary",), 6 * _nbytes((tm, d), F32) + 8 * 2**20),
        name="rmsmod",
    )(x, g.mod, g.mod)


def _rmsgain_kernel(x_ref, g_ref, o_ref):
    x = x_ref[...]
    r = lax.rsqrt(jnp.mean(x * x, axis=-1, keepdims=True) + EPS)
    o_ref[...] = (x * r) * g_ref[...]


def _rmsgain(x, gain, tm):
    m, d = x.shape
    return pl.pallas_call(
        _rmsgain_kernel,
        grid=(m // tm,),
        in_specs=[pl.BlockSpec((tm, d), lambda i: (i, 0)),
                  pl.BlockSpec((1, d), lambda i: (0, 0))],
        out_specs=pl.BlockSpec((tm, d), lambda i: (i, 0)),
        out_shape=jax.ShapeDtypeStruct((m, d), F32),
        compiler_params=_params(("arbitrary",), 6 * _nbytes((tm, d), F32) + 8 * 2**20),
        name="final_norm",
    )(x, gain.reshape(1, d))


def _cast_weight(w_ref, wb_ref, next_ref=None, shift=0):
    k, tn = wb_ref.shape
    rb = _largest_divisor(k, (512, 256, 128, 64, 32, 16))

    def body(i, c):
        r = pl.multiple_of(i * rb, rb)
        w = w_ref[pl.ds(r, rb), :]
        if shift:
            w = jnp.concatenate([w, next_ref[pl.ds(r, rb), :]], axis=1)[:, shift:shift + tn]
        wb_ref[pl.ds(r, rb), :] = w.astype(BF16)
        return c

    lax.fori_loop(0, k // rb, body, 0)


def _cast_on_first_row_tile(pairs):
    @pl.when(pl.program_id(1) == 0)
    def _():
        for w_ref, wb_ref in pairs:
            _cast_weight(w_ref, wb_ref)


def _mm_proj_kernel(*refs, shift, act, stacked):
    it = iter(refs)
    x_ref, w_ref = next(it), next(it)
    next_ref = next(it) if shift else None
    if stacked:
        next(it)
    outs = list(it)
    wb_ref = outs.pop()

    @pl.when(pl.program_id(1) == 0)
    def _():
        _cast_weight(w_ref, wb_ref, next_ref, shift)

    acc = jnp.dot(x_ref[...], wb_ref[...], preferred_element_type=F32)
    if act == "sigmoid":
        acc = jax.nn.sigmoid(acc)
    if stacked:
        o5_ref, ob_ref = outs
        o5_ref[...] = acc.reshape(o5_ref.shape)
        ob_ref[...] = acc.astype(ob_ref.dtype)
    else:
        outs[0][...] = acc.astype(outs[0].dtype)


def _mm_proj(x, w, l, col_off, ncols, tm, out_dtype, act=None, stacked=None):
    m, k = x.shape
    shift = col_off % 128
    base = col_off - shift
    tn = _largest_divisor(math.gcd(ncols, base) if base else ncols, (1024, 512, 256, 128))
    in_specs = [pl.BlockSpec((tm, k), lambda j, i: (i, 0)),
                pl.BlockSpec((None, k, tn), lambda j, i: (l, 0, base // tn + j))]
    args = [x, w]
    if shift:
        in_specs.append(pl.BlockSpec((None, k, 128), lambda j, i: (l, 0, (base + (j + 1) * tn) // 128)))
        args.append(w)
    out_spec = pl.BlockSpec((tm, tn), lambda j, i: (i, j))
    aliases = {}
    if stacked is None:
        out_specs, out_shape = out_spec, jax.ShapeDtypeStruct((m, ncols), out_dtype)
    else:
        buf, heads, rows_per_batch = stacked
        assert tn == ncols and rows_per_batch % tm == 0
        bpb = rows_per_batch // tm
        in_specs.append(pl.BlockSpec(memory_space=pl.ANY))
        args.append(buf)
        aliases = {len(args) - 1: 0}
        out_specs = [pl.BlockSpec((None, None, tm, heads, ncols // heads),
                                  lambda j, i: (l, i // bpb, i % bpb, 0, 0)), out_spec]
        out_shape = [jax.ShapeDtypeStruct(buf.shape, buf.dtype), jax.ShapeDtypeStruct((m, ncols), out_dtype)]
    vmem = (2 * (_nbytes((tm, k), BF16) + _nbytes((k, tn + 128), F32) + _nbytes((tm, tn), F32)
                 + _nbytes((tm, tn), out_dtype))
            + _nbytes((k, tn), BF16) + 2 * _nbytes((tm, tn), F32))
    return pl.pallas_call(
        functools.partial(_mm_proj_kernel, shift=shift, act=act, stacked=stacked is not None),
        grid=(ncols // tn, m // tm),
        in_specs=in_specs,
        out_specs=out_specs,
        out_shape=out_shape,
        scratch_shapes=[pltpu.VMEM((k, tn), BF16)],
        input_output_aliases=aliases,
        compiler_params=_params(("arbitrary", "arbitrary"), vmem + 4 * 2**20),
        name="mm_proj",
    )(*args)


def _mm_swiglu_kernel(x_ref, wa_ref, wg_ref, o_ref, wab_ref, wgb_ref):
    _cast_on_first_row_tile([(wa_ref, wab_ref), (wg_ref, wgb_ref)])
    x = x_ref[...]
    a = jnp.dot(x, wab_ref[...], preferred_element_type=F32)
    b = jnp.dot(x, wgb_ref[...], preferred_element_type=F32)
    o_ref[...] = ((a * jax.nn.sigmoid(a)) * b).astype(o_ref.dtype)


def _mm_swiglu(x, w_up, l, tm, col0, ncols, tn):
    m, k = x.shape
    f = w_up.shape[2] // 2
    assert ncols % tn == 0 and tn % 128 == 0 and col0 % 128 == 0 and f % 128 == 0
    off = lambda j, base: pl.multiple_of(base + j * tn, 128)
    w_spec = lambda base: pl.BlockSpec((pl.Squeezed(), pl.Element(k), pl.Element(tn)),
                                       lambda j, i: (l, 0, off(j, base)))
    vmem = (2 * (_nbytes((tm, k), BF16) + 2 * _nbytes((k, tn), F32) + _nbytes((tm, tn), BF16))
            + 2 * _nbytes((k, tn), BF16) + 4 * _nbytes((tm, tn), F32))
    return pl.pallas_call(
        _mm_swiglu_kernel,
        grid=(ncols // tn, m // tm),
        in_specs=[pl.BlockSpec((tm, k), lambda j, i: (i, 0)), w_spec(col0), w_spec(f + col0)],
        out_specs=pl.BlockSpec((tm, tn), lambda j, i: (i, j)),
        out_shape=jax.ShapeDtypeStruct((m, ncols), BF16),
        scratch_shapes=[pltpu.VMEM((k, tn), BF16), pltpu.VMEM((k, tn), BF16)],
        compiler_params=_params(("arbitrary", "arbitrary"), vmem + 4 * 2**20),
        name="mm_swiglu",
    )(x, w_up, w_up)


def _mm_res_kernel(*refs, n_parts, half):
    x_refs, w_refs = refs[:n_parts], refs[n_parts:2 * n_parts]
    res_ref, g_ref, o_ref = refs[2 * n_parts:2 * n_parts + 3]
    wb_refs = refs[2 * n_parts + 3:]
    _cast_on_first_row_tile(list(zip(w_refs, wb_refs)))
    acc = None
    for x_ref, wb_ref in zip(x_refs, wb_refs):
        part = jnp.dot(x_ref[...], wb_ref[...], preferred_element_type=F32)
        acc = part if acc is None else acc + part
    gate = 1.0 + g_ref[...]
    if half:
        gate = 0.5 * gate
    o_ref[...] = res_ref[...] + gate * acc


def _mm_res(g, xs, w, l, res, i_gate, half, tm, tn):
    m = xs[0].shape[0]
    d = w.shape[2]
    gg = g._replace(tm=tm)
    ks = [x.shape[1] for x in xs]
    row0 = [sum(ks[:i]) for i in range(len(ks))]
    assert sum(ks) == w.shape[1] and all(r % SUBLANES == 0 for r in row0)
    x_spec = lambda k: pl.BlockSpec((tm, k), lambda j, i: (i, 0))
    w_spec = lambda k, r: pl.BlockSpec((pl.Squeezed(), pl.Element(k), pl.Element(tn)),
                                       lambda j, i: (l, r, pl.multiple_of(j * tn, 128)))
    vmem = sum(2 * (_nbytes((tm, k), BF16) + _nbytes((k, tn), F32)) + _nbytes((k, tn), BF16) for k in ks)
    vmem += 2 * 2 * _nbytes((tm, tn), F32) + 3 * _nbytes((tm, tn), F32)
    return pl.pallas_call(
        functools.partial(_mm_res_kernel, n_parts=len(xs), half=half),
        grid=(d // tn, m // tm),
        in_specs=[x_spec(k) for k in ks] + [w_spec(k, r) for k, r in zip(ks, row0)]
                 + [pl.BlockSpec((tm, tn), lambda j, i: (i, j)),
                    _mod_spec(gg, l, i_gate, d, tn, lambda j, i: (i, j))],
        out_specs=pl.BlockSpec((tm, tn), lambda j, i: (i, j)),
        out_shape=jax.ShapeDtypeStruct((m, d), F32),
        scratch_shapes=[pltpu.VMEM((k, tn), BF16) for k in ks],
        compiler_params=_params(("arbitrary", "arbitrary"), vmem + 4 * 2**20),
        name="mm_res",
    )(*xs, *([w] * len(xs)), res, g.mod)


def _mm_merge_kernel(xa_ref, xb_ref, xc_ref, wa_ref, wb_ref, wc_ref, ga_ref, gb_ref, gc_ref,
                     o_ref, wab_ref, wbb_ref, wcb_ref):
    _cast_on_first_row_tile([(wa_ref, wab_ref), (wb_ref, wbb_ref), (wc_ref, wcb_ref)])
    acc = ga_ref[...].astype(F32) * jnp.dot(xa_ref[...], wab_ref[...], preferred_element_type=F32)
    acc = acc + gb_ref[...].astype(F32) * jnp.dot(xb_ref[...], wbb_ref[...], preferred_element_type=F32)
    acc = acc + gc_ref[...].astype(F32) * jnp.dot(xc_ref[...], wcb_ref[...], preferred_element_type=F32)
    o_ref[...] = acc.astype(o_ref.dtype)


def _mm_merge(xa, xb, xc, wa, wb, wc, l, gates, tm):
    m = xa.shape[0]
    d = wa.shape[2]
    tn = _largest_divisor(d, (512, 256, 128))
    nb = d // tn
    ks =(xa.shape[1], xb.shape[1], xc.shape[1])
    vmem = sum(2 * (_nbytes((tm, k), BF16) + _nbytes((k, tn), F32)) + _nbytes((k, tn), BF16) for k in ks)
    vmem += 2 * 4 * _nbytes((tm, tn), BF16) + 4 * _nbytes((tm, tn), F32)
    x_spec = lambda k: pl.BlockSpec((tm, k), lambda j, i: (i, 0))
    w_spec = lambda k: pl.BlockSpec((None, k, tn), lambda j, i: (l, 0, j))
    g_spec = lambda s: pl.BlockSpec((tm, tn), lambda j, i: (i, s * nb + j))
    return pl.pallas_call(
        _mm_merge_kernel,
        grid=(nb, m // tm),
        in_specs=[x_spec(ks[0]), x_spec(ks[1]), x_spec(ks[2]),
                  w_spec(ks[0]), w_spec(ks[1]), w_spec(ks[2]),
                  g_spec(0), g_spec(1), g_spec(2)],
        out_specs=pl.BlockSpec((tm, tn), lambda j, i: (i, j)),
        out_shape=jax.ShapeDtypeStruct((m, d), BF16),
        scratch_shapes=[pltpu.VMEM((k, tn), BF16) for k in ks],
        compiler_params=_params(("arbitrary", "arbitrary"), vmem + 4 * 2**20),
        name="mm_merge",
    )(xa, xb, xc, wa, wb, wc, gates, gates, gates)


def _lambda_value(lam_ref, lam_init):
    lv = lam_ref[...]
    e1 = jnp.exp(jnp.sum(lv[0:1] * lv[1:2], axis=-1, keepdims=True))
    e2 = jnp.exp(jnp.sum(lv[2:3] * lv[3:4], axis=-1, keepdims=True))
    return e1 - e2 + lam_init


def _subln(o, sub_ref, lam_init):
    r = lax.rsqrt(jnp.mean(o * o, axis=-1, keepdims=True) + EPS)
    return ((o * r) * sub_ref[...]) * (1.0 - lam_init)


def _attn_prompt_kernel(lam_ref, sub_ref, q_ref, k_ref, v_ref, o_ref, *, tq, dk, lam_init):
    t = q_ref.shape[0]
    c = (dk ** -0.5) * math.log2(math.e)
    lam = _lambda_value(lam_ref, lam_init)
    dims = (((1,), (1,)), ((), ()))
    below = (lax.broadcasted_iota(jnp.int32, (tq, tq), 1) <= lax.broadcasted_iota(jnp.int32, (tq, tq), 0))
    for qi in range(t // tq):
        lo, hi = qi * tq, (qi + 1) * tq
        q = q_ref[lo:hi, :]
        maps = []
        for n in range(2):
            cols = slice(n * dk, (n + 1) * dk)
            s_own = jnp.where(below, lax.dot_general(q[:, cols], k_ref[lo:hi, cols], dims,
                                                     preferred_element_type=F32), NEG)
            m = jnp.max(s_own, axis=-1, keepdims=True)
            if qi:
                s_past = lax.dot_general(q[:, cols], k_ref[0:lo, cols], dims, preferred_element_type=F32)
                m = jnp.maximum(m, jnp.max(s_past, axis=-1, keepdims=True))
            e = jnp.exp2((s_own - m) * c)
            l = jnp.sum(e, axis=-1, keepdims=True)
            o = jnp.dot(e.astype(BF16), v_ref[lo:hi, :], preferred_element_type=F32)
            if qi:
                e = jnp.exp2((s_past - m) * c)
                l = l + jnp.sum(e, axis=-1, keepdims=True)
                o = o + jnp.dot(e.astype(BF16), v_ref[0:lo, :], preferred_element_type=F32)
            maps.append(o * (1.0 / l))
        o = maps[0] - lam * maps[1]
        o_ref[lo:hi, :] = _subln(o, sub_ref, lam_init).astype(o_ref.dtype)


def _attn_prompt(q, k, v, lam_vecs, subln, l, nb, t, heads, dk, dv, lam_init):
    tq = _largest_divisor(t, (256, 128, 64, 32, 16, 8))
    blk = lambda w: pl.BlockSpec((t, w), lambda b, h: (b, h))
    vmem = 2 * (2 * _nbytes((t, 2 * dk), BF16) + 2 * _nbytes((t, dv), BF16)) + 8 * _nbytes((tq, t), F32)
    return pl.pallas_call(
        functools.partial(_attn_prompt_kernel, tq=tq, dk=dk, lam_init=lam_init),
        grid=(nb, heads),
        in_specs=[pl.BlockSpec((None, 4, dk), lambda b, h: (l, 0, 0)),
                  pl.BlockSpec((None, 1, dv), lambda b, h: (l, 0, 0)),
                  blk(2 * dk), blk(2 * dk), blk(dv)],
        out_specs=blk(dv),
        out_shape=jax.ShapeDtypeStruct((nb * t, heads * dv), BF16),
        compiler_params=_params(("arbitrary", "arbitrary"), vmem + 8 * 2**20),
        name="attn_prompt",
    )(lam_vecs, subln, q, k, v)


def _attn_sample_kernel(pt_ref, lam_ref, sub_ref, q_ref, kn_ref, vn_ref, *rest,
                        n_pages, heads, dk, lam_init):
    k_refs = rest[:n_pages]
    v_refs = rest[n_pages:2 * n_pages]
    o_ref, m_ref, l_ref, acc_ref, mask_ref = rest[2 * n_pages:]
    p = pl.program_id(1)
    scale = dk ** -0.5
    nq = q_ref.shape[0]
    dims = (((1,), (1,)), ((), ()))
    q = q_ref[...].astype(BF16)

    def scores(keys):
        s = [lax.dot_general(q[:, n * dk:(n + 1) * dk], keys[:, n * dk:(n + 1) * dk], dims,
                             preferred_element_type=F32) for n in range(2)]
        return jnp.concatenate(s, axis=0) * scale

    @pl.when(p == 0)
    def _():
        n_cols = mask_ref.shape[1]
        r = lax.broadcasted_iota(jnp.int32, (2 * nq, n_cols), 0)
        c = lax.broadcasted_iota(jnp.int32, (2 * nq, n_cols), 1)
        mask_ref[...] = jnp.where((r % heads) == (c % heads), 1.0, 0.0)
        r = lax.broadcasted_iota(jnp.int32, (2 * nq, nq), 0) % nq
        c = lax.broadcasted_iota(jnp.int32, (2 * nq, nq), 1)
        ok = ((r % heads) == (c % heads)) & ((c // heads) <= (r // heads))
        s = jnp.where(ok, scores(kn_ref[...].astype(BF16)), NEG)
        m = jnp.max(s, axis=-1, keepdims=True)
        e = jnp.exp(s - m)
        m_ref[...] = m
        l_ref[...] = jnp.sum(e, axis=-1, keepdims=True)
        acc_ref[...] = jnp.dot(e.astype(BF16), vn_ref[...].astype(BF16), preferred_element_type=F32)

    rows = k_refs[0].shape[0] * k_refs[0].shape[1]
    keys = jnp.concatenate([r[...].reshape(rows, 2 * dk).astype(BF16) for r in k_refs], axis=0)
    vals = jnp.concatenate([r[...].reshape(rows, v_refs[0].shape[2]).astype(BF16) for r in v_refs], axis=0)
    s = jnp.where(mask_ref[...] > 0.5, scores(keys), NEG)
    m_old = m_ref[...]
    m_new = jnp.maximum(m_old, jnp.max(s, axis=-1, keepdims=True))
    alpha = jnp.exp(m_old - m_new)
    e = jnp.exp(s - m_new)
    m_ref[...] = m_new
    l_ref[...] = alpha * l_ref[...] + jnp.sum(e, axis=-1, keepdims=True)
    acc_ref[...] = alpha * acc_ref[...] + jnp.dot(e.astype(BF16), vals, preferred_element_type=F32)

    @pl.when(p == pl.num_programs(1) - 1)
    def _():
        o = acc_ref[...] * (1.0 / l_ref[...])
        o = o[0:nq] - _lambda_value(lam_ref, lam_init) * o[nq:2 * nq]
        o_ref[...] = _subln(o, sub_ref, lam_init).astype(o_ref.dtype)


def _attn_sample(q, kn, vn, cache_k, cache_v, page_table, lam_vecs, subln, l, lam_init):
    nb, nq, _ = q.shape
    _, _, page, heads, dk2 = cache_k.shape
    dv = cache_v.shape[4]
    dk = dk2 // 2
    n_used = page_table.shape[1]
    g = _largest_divisor(n_used, (PAGES_PER_STEP, 2, 1))
    page_spec = lambda i, w: pl.BlockSpec((None, None, page, heads, w),
                                          lambda b, p, pt: (l, pt[b, p * g + i], 0, 0, 0))
    row_spec = lambda w: pl.BlockSpec((None, nq, w), lambda b, p, pt: (b, 0, 0))
    vmem = 2 * g * 2 * (_nbytes((page, SUBLANES, max(dk2, dv)), F32))
    vmem += 6 * _nbytes((2 * nq, g * page * heads), F32) + 2 * _nbytes((g * page * heads, dk2 + dv), BF16)
    return pl.pallas_call(
        functools.partial(_attn_sample_kernel, n_pages=g, heads=heads, dk=dk, lam_init=lam_init),
        grid_spec=pltpu.PrefetchScalarGridSpec(
            num_scalar_prefetch=1,
            grid=(nb, n_used // g),
            in_specs=[pl.BlockSpec((None, 4, dk), lambda b, p, pt: (l, 0, 0)),
                      pl.BlockSpec((None, 1, dv), lambda b, p, pt: (l, 0, 0)),
                      row_spec(dk2), row_spec(dk2), row_spec(dv)]
                     + [page_spec(i, dk2) for i in range(g)] + [page_spec(i, dv) for i in range(g)],
            out_specs=row_spec(dv),
            scratch_shapes=[pltpu.VMEM((2 * nq, 1), F32), pltpu.VMEM((2 * nq, 1), F32),
                            pltpu.VMEM((2 * nq, dv), F32), pltpu.VMEM((2 * nq, g * page * heads), F32)]),
        out_shape=jax.ShapeDtypeStruct((nb, nq, dv), BF16),
        compiler_params=_params(("arbitrary", "arbitrary"), vmem + 8 * 2**20),
        name="attn_sample",
    )(page_table, lam_vecs, subln, q, kn, vn, *([cache_k] * g), *([cache_v] * g))


def _split3(a):
    hi = a.astype(BF16)
    r1 = a - hi.astype(F32)
    mid = r1.astype(BF16)
    lo = (r1 - mid.astype(F32)).astype(BF16)
    return hi, mid, lo


def _exact_dot(sel, a, dims):
    out = None
    for part in _split3(a):
        d = lax.dot_general(sel, part, dims, preferred_element_type=F32)
        out = d if out is None else out + d
    return out


def _rec_kernel(*refs, chunk, heads, dk, dv, is_gla, has_state):
    it = iter(refs)
    q_ref, k_ref, v_ref, r_ref = next(it), next(it), next(it), next(it)
    if is_gla:
        gl_ref, wg_ref, bg_ref, norm_ref = next(it), next(it), next(it), next(it)
    else:
        cos_ref, sin_ref, loga_ref = next(it), next(it), next(it)
    s0_ref = next(it) if has_state else None
    o_ref, s_out_ref, s_ref = next(it), next(it), next(it)
    t = pl.program_id(1)
    rows = q_ref.shape[0]
    n_chunks = rows // chunk

    @pl.when(t == 0)
    def _():
        s_ref[...] = s0_ref[...] if has_state else jnp.zeros(s_ref.shape, F32)

    q, k, v = q_ref[...], k_ref[...], v_ref[...]
    if is_gla:
        gl = gl_ref[:, 0:wg_ref.shape[0]]
        z = jnp.dot(gl.astype(BF16), wg_ref[...].astype(BF16), preferred_element_type=F32) + bg_ref[...]
        a = jax.nn.log_sigmoid(z) / GLA_TAU
        q = q * (dk ** -0.5)
    else:
        a = jnp.broadcast_to(loga_ref[...], q.shape)
        half = dk // 2
        width = q.shape[1]
        lane = lax.broadcasted_iota(jnp.int32, q.shape, 1)
        first_half = (lane % dk) < half
        cos, sin = cos_ref[...], sin_ref[...]

        def rot(x):
            other = jnp.where(first_half, pltpu.roll(x, width - half, 1), pltpu.roll(x, half, 1))
            return x * cos + other * sin

        q = rot(q)
        k = rot(k) * (dk ** -0.5)

    ri = lax.broadcasted_iota(jnp.int32, (rows, rows), 0)
    ci = lax.broadcasted_iota(jnp.int32, (rows, rows), 1)
    same_chunk = (ri // chunk) == (ci // chunk)
    causal = same_chunk & (ci <= ri)
    mm = (((1,), (0,)), ((), ()))
    b = _exact_dot(jnp.where(causal, 1.0, 0.0).astype(BF16), a, mm)
    b_last = _exact_dot(jnp.where(same_chunk, 1.0, 0.0).astype(BF16), a, mm)
    q_t = q * jnp.exp(b)
    k_t = k * jnp.exp(-b)
    k_end = k * jnp.exp(b_last - b)
    ones = jnp.ones((chunk, dv), BF16)
    tt = (((0,), (0,)), ((), ()))
    nt = (((1,), (1,)), ((), ()))
    outs = []
    for h in range(heads):
        sk = slice(h * dk, (h + 1) * dk)
        sv = slice(h * dv, (h + 1) * dv)
        qh, kh, keh, vh = q_t[:, sk].astype(BF16), k_t[:, sk].astype(BF16), k_end[:, sk].astype(BF16), v[:, sv].astype(BF16)
        attn = jnp.where(causal, lax.dot_general(qh, kh, nt, preferred_element_type=F32), 0.0)
        o = jnp.dot(attn.astype(BF16), vh, preferred_element_type=F32)
        state = s_ref[h]
        inter = []
        for c in range(n_chunks):
            rs = slice(c * chunk, (c + 1) * chunk)
            inter.append(jnp.dot(qh[rs], state.astype(BF16), preferred_element_type=F32))
            decay = jnp.exp(_exact_dot_t(a[rs, sk], ones, tt))
            state = decay * state + lax.dot_general(keh[rs], vh[rs], tt, preferred_element_type=F32)
        s_ref[h] = state
        o = o + (inter[0] if n_chunks == 1 else jnp.concatenate(inter, axis=0))
        o = o * lax.rsqrt(jnp.mean(o * o, axis=-1, keepdims=True) + EPS)
        if is_gla:
            o = o * norm_ref[...]
        gate = r_ref[:, sv]
        outs.append(o * (gate * jax.nn.sigmoid(gate)))
    o_ref[...] = jnp.concatenate(outs, axis=1).astype(o_ref.dtype)

    @pl.when(t == pl.num_programs(1) - 1)
    def _():
        s_out_ref[...] = s_ref[...]


def _exact_dot_t(a, ones, dims):
    out = None
    for part in _split3(a):
        d = lax.dot_general(part, ones, dims, preferred_element_type=F32)
        out = d if out is None else out + d
    return out


def _recurrence(seg, nb, t, heads, dk, dv, l, *, gla=None, ret=None, state=None):
    is_gla = gla is not None
    chunk = min(CHUNK, t)
    rows = _largest_divisor(t, (4 * chunk, 2 * chunk, chunk))
    nblk = t // rows
    wq, wv = heads * dk, heads * dv
    assert wv == 2 * wq
    row_blk = lambda w, j: pl.BlockSpec((rows, w), lambda b, i: (b * nblk + i, j))
    in_specs = [row_blk(wq, 0), row_blk(wq, 1), row_blk(wv, 1), row_blk(wv, 2)]
    args = [seg, seg, seg, seg]
    if is_gla:
        gl, w_gate, b_gate, norm = gla
        rank = w_gate.shape[1]
        in_specs += [pl.BlockSpec((rows, gl.shape[1]), lambda b, i: (b * nblk + i, 0)),
                     pl.BlockSpec((None, rank, wq), lambda b, i: (l, 0, 0)),
                     pl.BlockSpec((None, 1, wq), lambda b, i: (l, 0, 0)),
                     pl.BlockSpec((None, 1, dv), lambda b, i: (l, 0, 0))]
        args += [gl, w_gate, b_gate, norm]
    else:
        cos, sin, loga = ret
        in_specs += [pl.BlockSpec((rows, wq), lambda b, i: (i, 0)),
                     pl.BlockSpec((rows, wq), lambda b, i: (i, 0)),
                     pl.BlockSpec((1, wq), lambda b, i: (0, 0))]
        args += [cos, sin, loga]
    if state is not None:
        in_specs.append(pl.BlockSpec((None, None, heads, dk, dv), lambda b, i: (l, b, 0, 0, 0)))
        args.append(state)
    vmem = 2 * (2 * _nbytes((rows, wq), F32) + 2 * _nbytes((rows, wv), F32) + _nbytes((rows, wv), BF16))
    vmem += 16 * _nbytes((rows, wq), F32) + 8 * _nbytes((rows, rows), F32)
    return pl.pallas_call(
        functools.partial(_rec_kernel, chunk=chunk, heads=heads, dk=dk, dv=dv, is_gla=is_gla,
                          has_state=state is not None),
        grid=(nb, nblk),
        in_specs=in_specs,
        out_specs=[pl.BlockSpec((rows, wv), lambda b, i: (b * nblk + i, 0)),
                   pl.BlockSpec((None, heads, dk, dv), lambda b, i: (b, 0, 0, 0))],
        out_shape=[jax.ShapeDtypeStruct((nb * t, wv), BF16),
                   jax.ShapeDtypeStruct((nb, heads, dk, dv), F32)],
        scratch_shapes=[pltpu.VMEM((heads, dk, dv), F32)],
        compiler_params=_params(("arbitrary", "arbitrary"), vmem + 8 * 2**20),
        name="gla" if is_gla else "retention",
    )(*args)


def _rope_tables(pos, heads, dk):
    half = dk // 2
    freq = 1.0 / (ROPE_BASE ** jnp.linspace(0.0, 1.0, half, dtype=F32))
    ang = pos.astype(F32)[:, None] * freq[None, :]
    cos, sin = jnp.cos(ang), jnp.sin(ang)
    cos = jnp.tile(jnp.concatenate([cos, cos], axis=-1), (1, heads))
    sin = jnp.tile(jnp.concatenate([-sin, sin], axis=-1), (1, heads))
    return cos, sin


def kernel(x_prompt, x_sample, cache_k, cache_v, state_gla, state_ret, page_table, c_prompt, c_sample,
           w_ada, b_ada, w_ffn1_up, w_ffn1_down, w_in, lambda_q1, lambda_k1, lambda_q2, lambda_k2,
           attn_subln, w_gla_gate, b_gla_gate, gla_norm, w_branch_a, w_branch_b, w_branch_c, w_out,
           w_ffn2_up, w_ffn2_down, final_norm):
    nb_p, t_p, d = x_prompt.shape
    nb_s, t_s, _ = x_sample.shape
    n_layers = w_ada.shape[0]
    a_heads, a_dk2 = cache_k.shape[3:]
    a_dk, a_dv = a_dk2 // 2, cache_v.shape[4]
    b_heads, b_dk, b_dv = state_gla.shape[2:]
    c_heads, c_dk, c_dv = state_ret.shape[2:]
    rank = w_gla_gate.shape[1]
    n_pages = page_table.shape[1]
    past_len = n_pages * cache_k.shape[2]

    w_a = a_heads * a_dk2
    w_av = a_heads * a_dv
    w_b = 2 * b_heads * b_dk + 2 * b_heads * b_dv
    w_c = 2 * c_heads * c_dk + 2 * c_heads * c_dv
    off_b = 2 * w_a + w_av
    off_gl = off_b + w_b
    off_c = off_gl + rank
    off_m = off_c + w_c
    assert off_m + 3 * d == w_in.shape[2]
    assert off_gl % 128 == 0 and rank <= 128 and off_gl + 128 <= w_in.shape[2]

    n_c = nb_p + nb_s
    c_rows = jnp.concatenate([c_prompt, c_sample, jnp.zeros((-n_c % SUBLANES, d), F32)], axis=0)
    mod = _ada_all(c_rows, w_ada, b_ada)
    m_p, m_s = nb_p * t_p, nb_s * t_s
    grp_p = Group(mod[:, :nb_p].reshape(n_layers, nb_p, 1, N_MOD * d), False, m_p, t_p,
                  _largest_divisor(t_p, (1024, 512, 256, 128, 64, 32, 16)))
    grp_s = Group(jnp.repeat(mod[:, nb_p:n_c], t_s, axis=1), True, m_s, t_s, m_s)

    lam_vecs = jnp.stack([lambda_q1, lambda_k1, lambda_q2, lambda_k2], axis=1)
    subln = attn_subln.reshape(n_layers, 1, a_dv)
    b_gate = b_gla_gate.reshape(n_layers, 1, b_heads * b_dk)
    norm_b = gla_norm.reshape(n_layers, 1, b_dv)
    log_gamma = jnp.log(1.0 - jnp.power(2.0, -5.0 - jnp.arange(c_heads, dtype=F32)))
    loga_c = jnp.repeat(log_gamma, c_dk).reshape(1, c_heads * c_dk)
    rope_p = _rope_tables(jnp.arange(t_p, dtype=jnp.int32), c_heads, c_dk)
    rope_s = _rope_tables(past_len + jnp.arange(t_s, dtype=jnp.int32), c_heads, c_dk)

    def ffn(g, x, l, w_up, w_down, i0):
        h = _rmsmod(g, x, l, i0, i0 + 1)
        f = w_down.shape[1]
        main = f // 512 * 512 if f > 512 else f
        acts = [_mm_swiglu(h, w_up, l, g.tm, 0, main, min(main, 512))]
        if f > main:
            acts.append(_mm_swiglu(h, w_up, l, g.tm, main, f - main, f - main))
        return _mm_res(g, acts, w_down, l, x, i0 + 2, True, min(g.tm, 512), _largest_divisor(d, (512, 256, 128)))

    def layer(g, x, l, nb, t, rope, state_b, state_c, kv_bufs):
        lam_init = 0.8 - 0.6 * math.exp(-0.3 * l)
        x = ffn(g, x, l, w_ffn1_up, w_ffn1_down, 0)
        h = _rmsmod(g, x, l, 3, 4)
        proj = lambda off, n, dt=F32, act=None, stacked=None: _mm_proj(h, w_in, l, off, n, g.tm, dt, act, stacked)
        seg_b = proj(off_b, w_b)
        gl = proj(off_gl, 128)
        seg_c = proj(off_c, w_c)
        gates = proj(off_m, 3 * d, BF16, "sigmoid")
        if kv_bufs is None:
            a_q, a_k, a_v = proj(0, w_a), proj(w_a, w_a), proj(2 * w_a, w_av)
            rows = lambda z: z.reshape(nb, t * a_heads, z.shape[1] // a_heads)
            o_a = _attn_sample(rows(a_q), rows(a_k), rows(a_v), cache_k, cache_v, page_table,
                               lam_vecs, subln, l, lam_init).reshape(nb * t, w_av)
            new_kv = (a_k, a_v)
        else:
            a_q = proj(0, w_a, BF16)
            k_buf, a_k = proj(w_a, w_a, BF16, stacked=(kv_bufs[0], a_heads, t))
            v_buf, a_v = proj(2 * w_a, w_av, BF16, stacked=(kv_bufs[1], a_heads, t))
            o_a = _attn_prompt(a_q, a_k, a_v, lam_vecs, subln, l, nb, t, a_heads, a_dk, a_dv, lam_init)
            new_kv = (k_buf, v_buf)
        o_b, s_b = _recurrence(seg_b, nb, t, b_heads, b_dk, b_dv, l,
                               gla=(gl, w_gla_gate, b_gate, norm_b), state=state_b)
        o_c, s_c = _recurrence(seg_c, nb, t, c_heads, c_dk, c_dv, l, ret=rope + (loga_c,), state=state_c)
        merged = _mm_merge(o_a, o_b, o_c, w_branch_a, w_branch_b, w_branch_c, l, gates, g.tm)
        x = _mm_res(g, [merged], w_out, l, x, 5, False, g.tm, _largest_divisor(d, (512, 256, 128)))
        x = ffn(g, x, l, w_ffn2_up, w_ffn2_down, 6)
        return x, new_kv, s_b, s_c

    hp = x_prompt.reshape(m_p, d)
    hs = x_sample.reshape(m_s, d)
    kv_p = (jnp.zeros((n_layers, nb_p, t_p, a_heads, a_dk2), F32),
            jnp.zeros((n_layers, nb_p, t_p, a_heads, a_dv), F32))
    outs_p, outs_s = [], []
    for l in range(n_layers):
        hp, kv_p, *new_p = layer(grp_p, hp, l, nb_p, t_p, rope_p, None, None, kv_p)
        hs, kv_s, *new_s = layer(grp_s, hs, l, nb_s, t_s, rope_s, state_gla, state_ret, None)
        outs_p.append(new_p)
        outs_s.append(list(kv_s) + new_s)
    y_p = _rmsgain(hp, final_norm, min(grp_p.tm, 512)).reshape(nb_p, t_p, d)
    y_s = _rmsgain(hs, final_norm, m_s).reshape(nb_s, t_s, d)

    def stacked(outs, i, shape):
        return jnp.stack([o[i] for o in outs]).reshape((n_layers,) + shape)

    return (y_p, y_s, kv_p[0], kv_p[1],
            stacked(outs_s, 0, (nb_s, t_s, a_heads, a_dk2)), stacked(outs_s, 1, (nb_s, t_s, a_heads, a_dv)),
            stacked(outs_p, 0, (nb_p, b_heads, b_dk, b_dv)), stacked(outs_s, 2, (nb_s, b_heads, b_dk, b_dv)),
            stacked(outs_p, 1, (nb_p, c_heads, c_dk, c_dv)), stacked(outs_s, 3, (nb_s, c_heads, c_dk, c_dv)))
```

```python
import functools
import math
from typing import Callable, NamedTuple

import jax
import jax.numpy as jnp
from jax import lax
from jax.experimental import pallas as pl
from jax.experimental.pallas import tpu as pltpu

EPS = 1e-6
CHUNK = 64
GLA_TAU = 16.0
ROPE_BASE = 10000.0
N_MOD = 9
BF16 = jnp.bfloat16
F32 = jnp.float32
NEG = float(jnp.finfo(jnp.float32).min)

V7X_VMEM_BYTES = 64 * 2**20
VMEM_CEILING = V7X_VMEM_BYTES - 6 * 2**20
SUBLANES = 8
PAGES_PER_STEP = 8


def _params(sem, vmem_bytes):
    return pltpu.CompilerParams(
        dimension_semantics=sem,
        vmem_limit_bytes=int(min(VMEM_CEILING, max(vmem_bytes, 16 * 2**20))))


def _largest_divisor(n, cands):
    for c in cands:
        if n % c == 0:
            return c
    raise ValueError(f"no tile in {cands} divides {n}")


def _nbytes(shape, dtype):
    return math.prod(shape) * jnp.dtype(dtype).itemsize


class Group(NamedTuple):
    mod: jax.Array
    per_row: bool
    rows: int
    rows_per_batch: int
    tm: int


def _mod_spec(g, l, i, d_model, tn, mn):
    nb = d_model // tn
    if g.per_row:
        return pl.BlockSpec((None, g.tm, tn), lambda *a: (l, mn(*a)[0], i * nb + mn(*a)[1]))
    blocks_per_batch = g.rows_per_batch // g.tm
    return pl.BlockSpec((None, None, 1, tn),
                        lambda *a: (l, mn(*a)[0] // blocks_per_batch, 0, i * nb + mn(*a)[1]))


def _ada_kernel(c_ref, w_ref, b_ref, o_ref):
    c = c_ref[...]
    s = (c * jax.nn.sigmoid(c)).astype(BF16)
    o_ref[...] = jnp.dot(s, w_ref[...].astype(BF16), preferred_element_type=F32) + b_ref[...]


def _ada_all(c_rows, w_ada, b_ada):
    n_layers, d_model, n = w_ada.shape
    rows = c_rows.shape[0]
    tn = _largest_divisor(n, (1024, 512, 256, 128))
    vmem = 2 * (_nbytes((d_model, tn), F32) + _nbytes((rows, d_model), F32)) + _nbytes((d_model, tn), F32)
    return pl.pallas_call(
        _ada_kernel,
        grid=(n_layers, n // tn),
        in_specs=[pl.BlockSpec((rows, d_model), lambda l, j: (0, 0)),
                  pl.BlockSpec((None, d_model, tn), lambda l, j: (l, 0, j)),
                  pl.BlockSpec((None, 1, tn), lambda l, j: (l, 0, j))],
        out_specs=pl.BlockSpec((None, rows, tn), lambda l, j: (l, 0, j)),
        out_shape=jax.ShapeDtypeStruct((n_layers, rows, n), F32),
        compiler_params=_params(("arbitrary", "arbitrary"), vmem + 8 * 2**20),
        name="adaln",
    )(c_rows, w_ada, b_ada.reshape(n_layers, 1, n))


def _rmsmod_kernel(x_ref, sc_ref, sh_ref, o_ref):
    x = x_ref[...]
    r = lax.rsqrt(jnp.mean(x * x, axis=-1, keepdims=True) + EPS)
    o_ref[...] = ((x * r) * (1.0 + sc_ref[...]) + sh_ref[...]).astype(o_ref.dtype)


def _rmsmod(g, x, l, i_shift, i_scale):
    m, d = x.shape
    tm = min(g.tm, 512)
    gg = g._replace(tm=tm)
    mn = lambda i: (i, 0)
    return pl.pallas_call(
        _rmsmod_kernel,
        grid=(m // tm,),
        in_specs=[pl.BlockSpec((tm, d), lambda i: (i, 0)),
                  _mod_spec(gg, l, i_scale, d, d, mn),
                  _mod_spec(gg, l, i_shift, d, d, mn)],
        out_specs=pl.BlockSpec((tm, d), lambda i: (i, 0)),
        out_shape=jax.ShapeDtypeStruct((m, d), BF16),
        compiler_params=_params(("arbitrary",), 6 * _nbytes((tm, d), F32) + 8 * 2**20),
        name="rmsmod",
    )(x, g.mod, g.mod)


def _rmsgain_kernel(x_ref, g_ref, o_ref):
    x = x_ref[...]
    r = lax.rsqrt(jnp.mean(x * x, axis=-1, keepdims=True) + EPS)
    o_ref[...] = (x * r) * g_ref[...]


def _rmsgain(x, gain, tm):
    m, d = x.shape
    return pl.pallas_call(
        _rmsgain_kernel,
        grid=(m // tm,),
        in_specs=[pl.BlockSpec((tm, d), lambda i: (i, 0)),
                  pl.BlockSpec((1, d), lambda i: (0, 0))],
        out_specs=pl.BlockSpec((tm, d), lambda i: (i, 0)),
        out_shape=jax.ShapeDtypeStruct((m, d), F32),
        compiler_params=_params(("arbitrary",), 6 * _nbytes((tm, d), F32) + 8 * 2**20),
        name="final_norm",
    )(x, gain.reshape(1, d))


class Rows(NamedTuple):
    xs: tuple
    extras: tuple
    outs: tuple
    epilogue: Callable
    tm: int


class Weight(NamedTuple):
    array: jax.Array
    spec: pl.BlockSpec
    tile: tuple
    x_index: int


def _tile_spec(tm, tn, rider):
    return pl.BlockSpec((tm, tn), (lambda j, i: (0, j)) if rider else (lambda j, i: (i, j)))


def _grid_mn(rider):
    return (lambda j, i: (0, j)) if rider else (lambda j, i: (i, j))


def _block_bytes(spec, dtype):
    if spec.block_shape is None:
        return 0
    dims = [1 if d is None or isinstance(d, pl.Squeezed) else getattr(d, "block_size", d) for d in spec.block_shape]
    return _nbytes(dims, dtype)


def _cast_weight(w_ref, wb_ref, transposed):
    if transposed:
        for c in range(w_ref.shape[0] // 128):
            wb_ref[:, c * 128:(c + 1) * 128] = w_ref[c * 128:(c + 1) * 128, :].T.astype(BF16)
        return
    rows = w_ref.shape[0]
    rb = _largest_divisor(rows, (512, 256, 128, 64, 32, 16))

    def body(i, c):
        r = pl.multiple_of(i * rb, rb)
        wb_ref[pl.ds(r, rb), :] = w_ref[pl.ds(r, rb), :].astype(BF16)
        return c

    lax.fori_loop(0, rows // rb, body, 0)


def _mm_kernel(*refs, counts, x_index, transposed, epilogues):
    it = iter(refs)
    take = lambda n: [next(it) for _ in range(n)]
    ins = [(take(nx), take(ne)) for nx, ne, _ in counts]
    w_refs = take(len(x_index))
    outs = [take(no) for _, _, no in counts]
    wb_refs = take(len(x_index))
    first_row_tile = pl.program_id(1) == 0

    @pl.when(first_row_tile)
    def _():
        for w_ref, wb_ref in zip(w_refs, wb_refs):
            _cast_weight(w_ref, wb_ref, transposed)

    def run(g):
        x_refs, extra_refs = ins[g]
        accs = [jnp.dot(x_refs[xi][...], wb_ref[...], preferred_element_type=F32)
                for xi, wb_ref in zip(x_index, wb_refs)]
        epilogues[g](accs, extra_refs, outs[g])

    run(0)
    if len(counts) > 1:
        pl.when(first_row_tile)(lambda: run(1))


def _mm(name, weights, groups, n_tiles, transposed=False, alias=None):
    primary = groups[0]
    in_specs, args, out_specs, out_shapes, counts = [], [], [], [], []
    vmem = 0
    for gi, g in enumerate(groups):
        for x in g.xs:
            spec = pl.BlockSpec((g.tm, x.shape[1]), (lambda j, i: (0, 0)) if gi else (lambda j, i: (i, 0)))
            in_specs.append(spec)
            args.append(x)
            vmem += 2 * _block_bytes(spec, x.dtype)
        if gi == 0 and alias is not None:
            aliases = {len(args) + alias[0]: alias[1]}
        for a, spec in g.extras:
            in_specs.append(spec)
            args.append(a)
            vmem += 2 * _block_bytes(spec, a.dtype)
        counts.append((len(g.xs), len(g.extras), len(g.outs)))
    for w in weights:
        in_specs.append(w.spec)
        args.append(w.array)
        vmem += 2 * _nbytes(w.tile, F32) + _nbytes(w.tile, BF16)
    for g in groups:
        for shape, spec in g.outs:
            out_specs.append(spec)
            out_shapes.append(shape)
            vmem += 2 * _block_bytes(spec, shape.dtype)
    tn = weights[0].tile[1]
    vmem += (len(weights) + 2) * _nbytes((primary.tm, tn), F32)
    m = primary.xs[0].shape[0]
    return pl.pallas_call(
        functools.partial(_mm_kernel, counts=tuple(counts), x_index=tuple(w.x_index for w in weights),
                          transposed=transposed, epilogues=tuple(g.epilogue for g in groups)),
        grid=(n_tiles, m // primary.tm),
        in_specs=in_specs,
        out_specs=out_specs,
        out_shape=out_shapes,
        scratch_shapes=[pltpu.VMEM(w.tile, BF16) for w in weights],
        input_output_aliases=aliases if alias is not None else {},
        compiler_params=_params(("arbitrary", "arbitrary"), vmem + 4 * 2**20),
        name=name,
    )(*args)


def _swiglu_epilogue(accs, extra_refs, out_refs):
    a, b = accs
    out_refs[0][...] = ((a * jax.nn.sigmoid(a)) * b).astype(out_refs[0].dtype)


def _mm_swiglu(hs, w_up, l, tms, col0, ncols, tn):
    k = hs[0].shape[1]
    f = w_up.shape[2] // 2
    assert ncols % tn == 0 and tn % 128 == 0 and col0 % 128 == 0 and f % 128 == 0
    w_spec = lambda base: pl.BlockSpec((pl.Squeezed(), pl.Element(k), pl.Element(tn)),
                                       lambda j, i: (l, 0, pl.multiple_of(base + j * tn, 128)))
    weights = [Weight(w_up, w_spec(col0), (k, tn), 0), Weight(w_up, w_spec(f + col0), (k, tn), 0)]
    groups = [Rows((h,), (), ((jax.ShapeDtypeStruct((h.shape[0], ncols), BF16), _tile_spec(tm, tn, gi > 0)),),
                   _swiglu_epilogue, tm) for gi, (h, tm) in enumerate(zip(hs, tms))]
    return _mm("mm_swiglu", weights, groups, ncols // tn)


def _res_epilogue(half, accs, extra_refs, out_refs):
    acc = accs[0]
    for part in accs[1:]:
        acc = acc + part
    res_ref, g_ref = extra_refs
    gate = 1.0 + g_ref[...]
    if half:
        gate = 0.5 * gate
    out_refs[0][...] = res_ref[...] + gate * acc


def _mm_res(gs, xss, w, l, ress, i_gate, half, tms, tn):
    d = w.shape[2]
    ks = [x.shape[1] for x in xss[0]]
    row0 = [sum(ks[:i]) for i in range(len(ks))]
    assert sum(ks) == w.shape[1] and all(r % SUBLANES == 0 for r in row0)
    w_spec = lambda k, r: pl.BlockSpec((pl.Squeezed(), pl.Element(k), pl.Element(tn)),
                                       lambda j, i: (l, r, pl.multiple_of(j * tn, 128)))
    weights = [Weight(w, w_spec(k, r), (k, tn), i) for i, (k, r) in enumerate(zip(ks, row0))]
    groups = []
    for gi, (g, xs, res, tm) in enumerate(zip(gs, xss, ress, tms)):
        spec = _tile_spec(tm, tn, gi > 0)
        gate_spec = _mod_spec(g._replace(tm=tm), l, i_gate, d, tn, _grid_mn(gi > 0))
        groups.append(Rows(tuple(xs), ((res, spec), (g.mod, gate_spec)),
                           ((jax.ShapeDtypeStruct(res.shape, F32), spec),),
                           functools.partial(_res_epilogue, half), tm))
    return _mm("mm_res", weights, groups, d // tn)


def _proj_epilogue(act, accs, extra_refs, out_refs):
    acc = accs[0]
    if act == "sigmoid":
        acc = jax.nn.sigmoid(acc)
    if len(out_refs) == 2:
        out_refs[0][...] = acc.reshape(out_refs[0].shape)
        out_refs[1][...] = acc.astype(out_refs[1].dtype)
    else:
        out_refs[0][...] = acc.astype(out_refs[0].dtype)


def _mm_proj(hs, w_t, l, col_off, ncols, tms, dtypes, act=None, stacked=None):
    k = hs[0].shape[1]
    assert col_off % SUBLANES == 0
    tn = _largest_divisor(ncols, (1024, 512, 256, 128))
    w_spec = pl.BlockSpec((pl.Squeezed(), pl.Element(tn), pl.Element(k)),
                          lambda j, i: (l, pl.multiple_of(col_off + j * tn, SUBLANES), 0))
    groups = []
    for gi, (h, tm, dt) in enumerate(zip(hs, tms, dtypes)):
        extras = ()
        outs = ((jax.ShapeDtypeStruct((h.shape[0], ncols), dt), _tile_spec(tm, tn, gi > 0)),)
        if gi == 0 and stacked is not None:
            buf, heads, rows_per_batch = stacked
            assert tn == ncols and rows_per_batch % tm == 0
            bpb = rows_per_batch // tm
            extras = ((buf, pl.BlockSpec(memory_space=pl.ANY)),)
            outs = ((jax.ShapeDtypeStruct(buf.shape, buf.dtype),
                     pl.BlockSpec((None, None, tm, heads, ncols // heads),
                                  lambda j, i: (l, i // bpb, i % bpb, 0, 0))),) + outs
        groups.append(Rows((h,), extras, outs, functools.partial(_proj_epilogue, act), tm))
    return _mm("mm_proj", [Weight(w_t, w_spec, (k, tn), 0)], groups, ncols // tn, transposed=True,
               alias=(0, 0) if stacked is not None else None)


def _merge_epilogue(accs, extra_refs, out_refs):
    acc = None
    for part, g_ref in zip(accs, extra_refs):
        part = g_ref[...].astype(F32) * part
        acc = part if acc is None else acc + part
    out_refs[0][...] = acc.astype(out_refs[0].dtype)


def _mm_merge(xss, ws, l, gatess, tms):
    d = ws[0].shape[2]
    tn = _largest_divisor(d, (512, 256, 128))
    nb = d // tn
    weights = [Weight(w, pl.BlockSpec((None, w.shape[1], tn), lambda j, i: (l, 0, j)), (w.shape[1], tn), i)
               for i, w in enumerate(ws)]
    groups = []
    for gi, (xs, gates, tm) in enumerate(zip(xss, gatess, tms)):
        g_spec = lambda s, gi=gi, tm=tm: pl.BlockSpec(
            (tm, tn), (lambda j, i: (0, s * nb + j)) if gi else (lambda j, i: (i, s * nb + j)))
        groups.append(Rows(tuple(xs), tuple((gates, g_spec(s)) for s in range(len(ws))),
                           ((jax.ShapeDtypeStruct((xs[0].shape[0], d), BF16), _tile_spec(tm, tn, gi > 0)),),
                           _merge_epilogue, tm))
    return _mm("mm_merge", weights, groups, nb)


def _lambda_value(lam_ref, lam_init):
    lv = lam_ref[...]
    e1 = jnp.exp(jnp.sum(lv[0:1] * lv[1:2], axis=-1, keepdims=True))
    e2 = jnp.exp(jnp.sum(lv[2:3] * lv[3:4], axis=-1, keepdims=True))
    return e1 - e2 + lam_init


def _subln(o, sub_ref, lam_init):
    r = lax.rsqrt(jnp.mean(o * o, axis=-1, keepdims=True) + EPS)
    return ((o * r) * sub_ref[...]) * (1.0 - lam_init)


def _attn_prompt_kernel(lam_ref, sub_ref, q_ref, k_ref, v_ref, o_ref, *, tq, dk, lam_init):
    t = q_ref.shape[0]
    c = (dk ** -0.5) * math.log2(math.e)
    lam = _lambda_value(lam_ref, lam_init)
    dims = (((1,), (1,)), ((), ()))
    below = (lax.broadcasted_iota(jnp.int32, (tq, tq), 1) <= lax.broadcasted_iota(jnp.int32, (tq, tq), 0))
    for qi in range(t // tq):
        lo, hi = qi * tq, (qi + 1) * tq
        q = q_ref[lo:hi, :]
        maps = []
        for n in range(2):
            cols = slice(n * dk, (n + 1) * dk)
            s_own = jnp.where(below, lax.dot_general(q[:, cols], k_ref[lo:hi, cols], dims,
                                                     preferred_element_type=F32), NEG)
            m = jnp.max(s_own, axis=-1, keepdims=True)
            if qi:
                s_past = lax.dot_general(q[:, cols], k_ref[0:lo, cols], dims, preferred_element_type=F32)
                m = jnp.maximum(m, jnp.max(s_past, axis=-1, keepdims=True))
            e = jnp.exp2((s_own - m) * c)
            l = jnp.sum(e, axis=-1, keepdims=True)
            o = jnp.dot(e.astype(BF16), v_ref[lo:hi, :], preferred_element_type=F32)
            if qi:
                e = jnp.exp2((s_past - m) * c)
                l = l + jnp.sum(e, axis=-1, keepdims=True)
                o = o + jnp.dot(e.astype(BF16), v_ref[0:lo, :], preferred_element_type=F32)
            maps.append(o * (1.0 / l))
        o = maps[0] - lam * maps[1]
        o_ref[lo:hi, :] = _subln(o, sub_ref, lam_init).astype(o_ref.dtype)


def _attn_prompt(q, k, v, lam_vecs, subln, l, nb, t, heads, dk, dv, lam_init):
    tq = _largest_divisor(t, (256, 128, 64, 32, 16, 8))
    blk = lambda w: pl.BlockSpec((t, w), lambda b, h: (b, h))
    vmem = 2 * (2 * _nbytes((t, 2 * dk), BF16) + 2 * _nbytes((t, dv), BF16)) + 8 * _nbytes((tq, t), F32)
    return pl.pallas_call(
        functools.partial(_attn_prompt_kernel, tq=tq, dk=dk, lam_init=lam_init),
        grid=(nb, heads),
        in_specs=[pl.BlockSpec((None, 4, dk), lambda b, h: (l, 0, 0)),
                  pl.BlockSpec((None, 1, dv), lambda b, h: (l, 0, 0)),
                  blk(2 * dk), blk(2 * dk), blk(dv)],
        out_specs=blk(dv),
        out_shape=jax.ShapeDtypeStruct((nb * t, heads * dv), BF16),
        compiler_params=_params(("arbitrary", "arbitrary"), vmem + 8 * 2**20),
        name="attn_prompt",
    )(lam_vecs, subln, q, k, v)


def _attn_sample_kernel(pt_ref, lam_ref, sub_ref, q_ref, kn_ref, vn_ref, *rest,
                        n_pages, heads, dk, lam_init):
    k_refs = rest[:n_pages]
    v_refs = rest[n_pages:2 * n_pages]
    o_ref, m_ref, l_ref, acc_ref, mask_ref = rest[2 * n_pages:]
    p = pl.program_id(1)
    scale = dk ** -0.5
    nq = q_ref.shape[0]
    dims = (((1,), (1,)), ((), ()))
    q = q_ref[...].astype(BF16)

    def scores(keys):
        s = [lax.dot_general(q[:, n * dk:(n + 1) * dk], keys[:, n * dk:(n + 1) * dk], dims,
                             preferred_element_type=F32) for n in range(2)]
        return jnp.concatenate(s, axis=0) * scale

    @pl.when(p == 0)
    def _():
        n_cols = mask_ref.shape[1]
        r = lax.broadcasted_iota(jnp.int32, (2 * nq, n_cols), 0)
        c = lax.broadcasted_iota(jnp.int32, (2 * nq, n_cols), 1)
        mask_ref[...] = jnp.where((r % heads) == (c % heads), 1.0, 0.0)
        r = lax.broadcasted_iota(jnp.int32, (2 * nq, nq), 0) % nq
        c = lax.broadcasted_iota(jnp.int32, (2 * nq, nq), 1)
        ok = ((r % heads) == (c % heads)) & ((c // heads) <= (r // heads))
        s = jnp.where(ok, scores(kn_ref[...].astype(BF16)), NEG)
        m = jnp.max(s, axis=-1, keepdims=True)
        e = jnp.exp(s - m)
        m_ref[...] = m
        l_ref[...] = jnp.sum(e, axis=-1, keepdims=True)
        acc_ref[...] = jnp.dot(e.astype(BF16), vn_ref[...].astype(BF16), preferred_element_type=F32)

    rows = k_refs[0].shape[0] * k_refs[0].shape[1]
    keys = jnp.concatenate([r[...].reshape(rows, 2 * dk).astype(BF16) for r in k_refs], axis=0)
    vals = jnp.concatenate([r[...].reshape(rows, v_refs[0].shape[2]).astype(BF16) for r in v_refs], axis=0)
    s = jnp.where(mask_ref[...] > 0.5, scores(keys), NEG)
    m_old = m_ref[...]
    m_new = jnp.maximum(m_old, jnp.max(s, axis=-1, keepdims=True))
    alpha = jnp.exp(m_old - m_new)
    e = jnp.exp(s - m_new)
    m_ref[...] = m_new
    l_ref[...] = alpha * l_ref[...] + jnp.sum(e, axis=-1, keepdims=True)
    acc_ref[...] = alpha * acc_ref[...] + jnp.dot(e.astype(BF16), vals, preferred_element_type=F32)

    @pl.when(p == pl.num_programs(1) - 1)
    def _():
        o = acc_ref[...] * (1.0 / l_ref[...])
        o = o[0:nq] - _lambda_value(lam_ref, lam_init) * o[nq:2 * nq]
        o_ref[...] = _subln(o, sub_ref, lam_init).astype(o_ref.dtype)


def _attn_sample(q, kn, vn, cache_k, cache_v, page_table, lam_vecs, subln, l, lam_init):
    nb, nq, _ = q.shape
    _, _, page, heads, dk2 = cache_k.shape
    dv = cache_v.shape[4]
    dk = dk2 // 2
    n_used = page_table.shape[1]
    g = _largest_divisor(n_used, (PAGES_PER_STEP, 2, 1))
    page_spec = lambda i, w: pl.BlockSpec((None, None, page, heads, w),
                                          lambda b, p, pt: (l, pt[b, p * g + i], 0, 0, 0))
    row_spec = lambda w: pl.BlockSpec((None, nq, w), lambda b, p, pt: (b, 0, 0))
    vmem = 2 * g * 2 * (_nbytes((page, SUBLANES, max(dk2, dv)), F32))
    vmem += 6 * _nbytes((2 * nq, g * page * heads), F32) + 2 * _nbytes((g * page * heads, dk2 + dv), BF16)
    return pl.pallas_call(
        functools.partial(_attn_sample_kernel, n_pages=g, heads=heads, dk=dk, lam_init=lam_init),
        grid_spec=pltpu.PrefetchScalarGridSpec(
            num_scalar_prefetch=1,
            grid=(nb, n_used // g),
            in_specs=[pl.BlockSpec((None, 4, dk), lambda b, p, pt: (l, 0, 0)),
                      pl.BlockSpec((None, 1, dv), lambda b, p, pt: (l, 0, 0)),
                      row_spec(dk2), row_spec(dk2), row_spec(dv)]
                     + [page_spec(i, dk2) for i in range(g)] + [page_spec(i, dv) for i in range(g)],
            out_specs=row_spec(dv),
            scratch_shapes=[pltpu.VMEM((2 * nq, 1), F32), pltpu.VMEM((2 * nq, 1), F32),
                            pltpu.VMEM((2 * nq, dv), F32), pltpu.VMEM((2 * nq, g * page * heads), F32)]),
        out_shape=jax.ShapeDtypeStruct((nb, nq, dv), BF16),
        compiler_params=_params(("arbitrary", "arbitrary"), vmem + 8 * 2**20),
        name="attn_sample",
    )(page_table, lam_vecs, subln, q, kn, vn, *([cache_k] * g), *([cache_v] * g))


def _split3(a):
    hi = a.astype(BF16)
    r1 = a - hi.astype(F32)
    mid = r1.astype(BF16)
    lo = (r1 - mid.astype(F32)).astype(BF16)
    return hi, mid, lo


def _exact_dot(sel, a, dims):
    out = None
    for part in _split3(a):
        d = lax.dot_general(sel, part, dims, preferred_element_type=F32)
        out = d if out is None else out + d
    return out


def _exact_dot_t(a, ones, dims):
    out = None
    for part in _split3(a):
        d = lax.dot_general(part, ones, dims, preferred_element_type=F32)
        out = d if out is None else out + d
    return out


def _rec_kernel(*refs, chunk, heads, dk, dv, is_gla, has_state):
    it = iter(refs)
    q_ref, k_ref, v_ref, r_ref = next(it), next(it), next(it), next(it)
    if is_gla:
        gl_ref, wg_ref, bg_ref, norm_ref = next(it), next(it), next(it), next(it)
    else:
        cos_ref, sin_ref, loga_ref = next(it), next(it), next(it)
    s0_ref = next(it) if has_state else None
    o_ref, s_out_ref, s_ref = next(it), next(it), next(it)
    t = pl.program_id(1)
    rows = q_ref.shape[0]
    n_chunks = rows // chunk

    @pl.when(t == 0)
    def _():
        s_ref[...] = s0_ref[...] if has_state else jnp.zeros(s_ref.shape, F32)

    q, k, v = q_ref[...], k_ref[...], v_ref[...]
    if is_gla:
        gl = gl_ref[:, 0:wg_ref.shape[0]]
        z = jnp.dot(gl.astype(BF16), wg_ref[...].astype(BF16), preferred_element_type=F32) + bg_ref[...]
        a = jax.nn.log_sigmoid(z) / GLA_TAU
        q = q * (dk ** -0.5)
    else:
        a = jnp.broadcast_to(loga_ref[...], q.shape)
        half = dk // 2
        width = q.shape[1]
        lane = lax.broadcasted_iota(jnp.int32, q.shape, 1)
        first_half = (lane % dk) < half
        cos, sin = cos_ref[...], sin_ref[...]

        def rot(x):
            other = jnp.where(first_half, pltpu.roll(x, width - half, 1), pltpu.roll(x, half, 1))
            return x * cos + other * sin

        q = rot(q)
        k = rot(k) * (dk ** -0.5)

    ri = lax.broadcasted_iota(jnp.int32, (rows, rows), 0)
    ci = lax.broadcasted_iota(jnp.int32, (rows, rows), 1)
    same_chunk = (ri // chunk) == (ci // chunk)
    causal = same_chunk & (ci <= ri)
    mm = (((1,), (0,)), ((), ()))
    b = _exact_dot(jnp.where(causal, 1.0, 0.0).astype(BF16), a, mm)
    b_last = _exact_dot(jnp.where(same_chunk, 1.0, 0.0).astype(BF16), a, mm)
    q_t = q * jnp.exp(b)
    k_t = k * jnp.exp(-b)
    k_end = k * jnp.exp(b_last - b)
    ones = jnp.ones((chunk, dv), BF16)
    tt = (((0,), (0,)), ((), ()))
    nt = (((1,), (1,)), ((), ()))
    outs = []
    for h in range(heads):
        sk = slice(h * dk, (h + 1) * dk)
        sv = slice(h * dv, (h + 1) * dv)
        qh, kh, keh, vh = q_t[:, sk].astype(BF16), k_t[:, sk].astype(BF16), k_end[:, sk].astype(BF16), v[:, sv].astype(BF16)
        attn = jnp.where(causal, lax.dot_general(qh, kh, nt, preferred_element_type=F32), 0.0)
        o = jnp.dot(attn.astype(BF16), vh, preferred_element_type=F32)
        state = s_ref[h]
        inter = []
        for c in range(n_chunks):
            rs = slice(c * chunk, (c + 1) * chunk)
            inter.append(jnp.dot(qh[rs], state.astype(BF16), preferred_element_type=F32))
            decay = jnp.exp(_exact_dot_t(a[rs, sk], ones, tt))
            state = decay * state + lax.dot_general(keh[rs], vh[rs], tt, preferred_element_type=F32)
        s_ref[h] = state
        o = o + (inter[0] if n_chunks == 1 else jnp.concatenate(inter, axis=0))
        o = o * lax.rsqrt(jnp.mean(o * o, axis=-1, keepdims=True) + EPS)
        if is_gla:
            o = o * norm_ref[...]
        gate = r_ref[:, sv]
        outs.append(o * (gate * jax.nn.sigmoid(gate)))
    o_ref[...] = jnp.concatenate(outs, axis=1).astype(o_ref.dtype)

    @pl.when(t == pl.num_programs(1) - 1)
    def _():
        s_out_ref[...] = s_ref[...]


def _recurrence(seg, nb, t, heads, dk, dv, l, *, gla=None, ret=None, state=None):
    is_gla = gla is not None
    chunk = min(CHUNK, t)
    rows = _largest_divisor(t, (4 * chunk, 2 * chunk, chunk))
    nblk = t // rows
    wq, wv = heads * dk, heads * dv
    assert wv == 2 * wq
    row_blk = lambda w, j: pl.BlockSpec((rows, w), lambda b, i: (b * nblk + i, j))
    in_specs = [row_blk(wq, 0), row_blk(wq, 1), row_blk(wv, 1), row_blk(wv, 2)]
    args = [seg, seg, seg, seg]
    if is_gla:
        gl, w_gate, b_gate, norm = gla
        rank = w_gate.shape[1]
        in_specs += [pl.BlockSpec((rows, gl.shape[1]), lambda b, i: (b * nblk + i, 0)),
                     pl.BlockSpec((None, rank, wq), lambda b, i: (l, 0, 0)),
                     pl.BlockSpec((None, 1, wq), lambda b, i: (l, 0, 0)),
                     pl.BlockSpec((None, 1, dv), lambda b, i: (l, 0, 0))]
        args += [gl, w_gate, b_gate, norm]
    else:
        cos, sin, loga = ret
        in_specs += [pl.BlockSpec((rows, wq), lambda b, i: (i, 0)),
                     pl.BlockSpec((rows, wq), lambda b, i: (i, 0)),
                     pl.BlockSpec((1, wq), lambda b, i: (0, 0))]
        args += [cos, sin, loga]
    if state is not None:
        in_specs.append(pl.BlockSpec((None, None, heads, dk, dv), lambda b, i: (l, b, 0, 0, 0)))
        args.append(state)
    vmem = 2 * (2 * _nbytes((rows, wq), F32) + 2 * _nbytes((rows, wv), F32) + _nbytes((rows, wv), BF16))
    vmem += 16 * _nbytes((rows, wq), F32) + 8 * _nbytes((rows, rows), F32)
    return pl.pallas_call(
        functools.partial(_rec_kernel, chunk=chunk, heads=heads, dk=dk, dv=dv, is_gla=is_gla,
                          has_state=state is not None),
        grid=(nb, nblk),
        in_specs=in_specs,
        out_specs=[pl.BlockSpec((rows, wv), lambda b, i: (b * nblk + i, 0)),
                   pl.BlockSpec((None, heads, dk, dv), lambda b, i: (b, 0, 0, 0))],
        out_shape=[jax.ShapeDtypeStruct((nb * t, wv), BF16),
                   jax.ShapeDtypeStruct((nb, heads, dk, dv), F32)],
        scratch_shapes=[pltpu.VMEM((heads, dk, dv), F32)],
        compiler_params=_params(("arbitrary", "arbitrary"), vmem + 8 * 2**20),
        name="gla" if is_gla else "retention",
    )(*args)


def _rope_tables(pos, heads, dk):
    half = dk // 2
    freq = 1.0 / (ROPE_BASE ** jnp.linspace(0.0, 1.0, half, dtype=F32))
    ang = pos.astype(F32)[:, None] * freq[None, :]
    cos, sin = jnp.cos(ang), jnp.sin(ang)
    cos = jnp.tile(jnp.concatenate([cos, cos], axis=-1), (1, heads))
    sin = jnp.tile(jnp.concatenate([-sin, sin], axis=-1), (1, heads))
    return cos, sin


def kernel(x_prompt, x_sample, cache_k, cache_v, state_gla, state_ret, page_table, c_prompt, c_sample,
           w_ada, b_ada, w_ffn1_up, w_ffn1_down, w_in, lambda_q1, lambda_k1, lambda_q2, lambda_k2,
           attn_subln, w_gla_gate, b_gla_gate, gla_norm, w_branch_a, w_branch_b, w_branch_c, w_out,
           w_ffn2_up, w_ffn2_down, final_norm):
    nb_p, t_p, d = x_prompt.shape
    nb_s, t_s, _ = x_sample.shape
    n_layers = w_ada.shape[0]
    a_heads, a_dk2 = cache_k.shape[3:]
    a_dk, a_dv = a_dk2 // 2, cache_v.shape[4]
    b_heads, b_dk, b_dv = state_gla.shape[2:]
    c_heads, c_dk, c_dv = state_ret.shape[2:]
    rank = w_gla_gate.shape[1]
    n_pages = page_table.shape[1]
    past_len = n_pages * cache_k.shape[2]

    w_a = a_heads * a_dk2
    w_av = a_heads * a_dv
    w_b = 2 * b_heads * b_dk + 2 * b_heads * b_dv
    w_c = 2 * c_heads * c_dk + 2 * c_heads * c_dv
    off_b = 2 * w_a + w_av
    off_gl = off_b + w_b
    off_c = off_gl + rank
    off_m = off_c + w_c
    assert off_m + 3 * d == w_in.shape[2]
    assert rank <= 128 and off_gl + 128 <= w_in.shape[2]
    w_in_t = jnp.swapaxes(w_in, 1, 2)

    n_c = nb_p + nb_s
    c_rows = jnp.concatenate([c_prompt, c_sample, jnp.zeros((-n_c % SUBLANES, d), F32)], axis=0)
    mod = _ada_all(c_rows, w_ada, b_ada)
    m_p, m_s = nb_p * t_p, nb_s * t_s
    grp_p = Group(mod[:, :nb_p].reshape(n_layers, nb_p, 1, N_MOD * d), False, m_p, t_p,
                  _largest_divisor(t_p, (1024, 512, 256, 128, 64, 32, 16)))
    grp_s = Group(jnp.repeat(mod[:, nb_p:n_c], t_s, axis=1), True, m_s, t_s, m_s)
    grps = (grp_p, grp_s)
    tms = (grp_p.tm, m_s)
    tms_narrow = (min(grp_p.tm, 512), m_s)

    lam_vecs = jnp.stack([lambda_q1, lambda_k1, lambda_q2, lambda_k2], axis=1)
    subln = attn_subln.reshape(n_layers, 1, a_dv)
    b_gate = b_gla_gate.reshape(n_layers, 1, b_heads * b_dk)
    norm_b = gla_norm.reshape(n_layers, 1, b_dv)
    log_gamma = jnp.log(1.0 - jnp.power(2.0, -5.0 - jnp.arange(c_heads, dtype=F32)))
    loga_c = jnp.repeat(log_gamma, c_dk).reshape(1, c_heads * c_dk)
    rope_p = _rope_tables(jnp.arange(t_p, dtype=jnp.int32), c_heads, c_dk)
    rope_s = _rope_tables(past_len + jnp.arange(t_s, dtype=jnp.int32), c_heads, c_dk)

    def ffn(xs, l, w_up, w_down, i0):
        hs = [_rmsmod(g, x, l, i0, i0 + 1) for g, x in zip(grps, xs)]
        f = w_down.shape[1]
        main = f // 512 * 512 if f > 512 else f
        pieces = [_mm_swiglu(hs, w_up, l, tms, 0, main, min(main, 512))]
        if f > main:
            pieces.append(_mm_swiglu(hs, w_up, l, tms, main, f - main, f - main))
        acts = list(zip(*pieces))
        return _mm_res(grps, acts, w_down, l, xs, i0 + 2, True, tms_narrow, _largest_divisor(d, (512, 256, 128)))

    def layer(xs, l, kv_p):
        lam_init = 0.8 - 0.6 * math.exp(-0.3 * l)
        xs = ffn(xs, l, w_ffn1_up, w_ffn1_down, 0)
        hs = [_rmsmod(g, x, l, 3, 4) for g, x in zip(grps, xs)]
        proj = lambda off, n, dts=(F32, F32), act=None, stacked=None: _mm_proj(
            hs, w_in_t, l, off, n, tms, dts, act, stacked)
        q_p, q_s = proj(0, w_a, (BF16, F32))
        k_buf, k_p, k_s = proj(w_a, w_a, (BF16, F32), stacked=(kv_p[0], a_heads, t_p))
        v_buf, v_p, v_s = proj(2 * w_a, w_av, (BF16, F32), stacked=(kv_p[1], a_heads, t_p))
        seg_b = proj(off_b, w_b)
        gl = proj(off_gl, 128)
        seg_c = proj(off_c, w_c)
        gates = proj(off_m, 3 * d, (BF16, BF16), "sigmoid")

        o_a_p = _attn_prompt(q_p, k_p, v_p, lam_vecs, subln, l, nb_p, t_p, a_heads, a_dk, a_dv, lam_init)
        rows = lambda z: z.reshape(nb_s, t_s * a_heads, z.shape[1] // a_heads)
        o_a_s = _attn_sample(rows(q_s), rows(k_s), rows(v_s), cache_k, cache_v, page_table,
                             lam_vecs, subln, l, lam_init).reshape(m_s, w_av)
        o_b_p, gla_p = _recurrence(seg_b[0], nb_p, t_p, b_heads, b_dk, b_dv, l,
                                   gla=(gl[0], w_gla_gate, b_gate, norm_b))
        o_b_s, gla_s = _recurrence(seg_b[1], nb_s, t_s, b_heads, b_dk, b_dv, l,
                                   gla=(gl[1], w_gla_gate, b_gate, norm_b), state=state_gla)
        o_c_p, ret_p = _recurrence(seg_c[0], nb_p, t_p, c_heads, c_dk, c_dv, l, ret=rope_p + (loga_c,))
        o_c_s, ret_s = _recurrence(seg_c[1], nb_s, t_s, c_heads, c_dk, c_dv, l, ret=rope_s + (loga_c,),
                                   state=state_ret)

        merged = _mm_merge([(o_a_p, o_b_p, o_c_p), (o_a_s, o_b_s, o_c_s)],
                           (w_branch_a, w_branch_b, w_branch_c), l, gates, tms)
        xs = _mm_res(grps, [(merged[0],), (merged[1],)], w_out, l, xs, 5, False, tms,
                     _largest_divisor(d, (512, 256, 128)))
        xs = ffn(xs, l, w_ffn2_up, w_ffn2_down, 6)
        return xs, (k_buf, v_buf), (k_s, v_s, gla_p, gla_s, ret_p, ret_s)

    xs = [x_prompt.reshape(m_p, d), x_sample.reshape(m_s, d)]
    kv_p = (jnp.zeros((n_layers, nb_p, t_p, a_heads, a_dk2), F32),
            jnp.zeros((n_layers, nb_p, t_p, a_heads, a_dv), F32))
    per_layer = []
    for l in range(n_layers):
        xs, kv_p, new = layer(xs, l, kv_p)
        per_layer.append(new)
    y_p = _rmsgain(xs[0], final_norm, min(grp_p.tm, 512)).reshape(nb_p, t_p, d)
    y_s = _rmsgain(xs[1], final_norm, m_s).reshape(nb_s, t_s, d)

    def stacked(i, shape):
        return jnp.stack([new[i] for new in per_layer]).reshape((n_layers,) + shape)

    return (y_p, y_s, kv_p[0], kv_p[1],
            stacked(0, (nb_s, t_s, a_heads, a_dk2)), stacked(1, (nb_s, t_s, a_heads, a_dv)),
            stacked(2, (nb_p, b_heads, b_dk, b_dv)), stacked(3, (nb_s, b_heads, b_dk, b_dv)),
            stacked(4, (nb_p, c_heads, c_dk, c_dv)), stacked(5, (nb_s, c_heads, c_dk, c_dv)))
```

```python
import functools
import math
from typing import Callable, NamedTuple

import jax
import jax.numpy as jnp
from jax import lax
from jax.experimental import pallas as pl
from jax.experimental.pallas import tpu as pltpu

EPS = 1e-6
CHUNK = 64
GLA_TAU = 16.0
ROPE_BASE = 10000.0
N_MOD = 9
BF16 = jnp.bfloat16
F32 = jnp.float32
NEG = float(jnp.finfo(jnp.float32).min)

V7X_VMEM_BYTES = 64 * 2**20
VMEM_CEILING = V7X_VMEM_BYTES - 6 * 2**20
SUBLANES = 8
PAGES_PER_STEP = 8


def _params(sem, vmem_bytes):
    return pltpu.CompilerParams(
        dimension_semantics=sem,
        vmem_limit_bytes=int(min(VMEM_CEILING, max(vmem_bytes, 16 * 2**20))))


def _largest_divisor(n, cands):
    for c in cands:
        if n % c == 0:
            return c
    raise ValueError(f"no tile in {cands} divides {n}")


def _nbytes(shape, dtype):
    return math.prod(shape) * jnp.dtype(dtype).itemsize


class Group(NamedTuple):
    mod: jax.Array
    per_row: bool
    rows: int
    rows_per_batch: int
    tm: int


def _mod_spec(g, l, i, d_model, tn, mn):
    nb = d_model // tn
    if g.per_row:
        return pl.BlockSpec((None, g.tm, tn), lambda *a: (l, mn(*a)[0], i * nb + mn(*a)[1]))
    blocks_per_batch = g.rows_per_batch // g.tm
    return pl.BlockSpec((None, None, 1, tn),
                        lambda *a: (l, mn(*a)[0] // blocks_per_batch, 0, i * nb + mn(*a)[1]))


def _ada_kernel(c_ref, w_ref, b_ref, o_ref):
    c = c_ref[...]
    s = (c * jax.nn.sigmoid(c)).astype(BF16)
    o_ref[...] = jnp.dot(s, w_ref[...].astype(BF16), preferred_element_type=F32) + b_ref[...]


def _ada_all(c_rows, w_ada, b_ada):
    n_layers, d_model, n = w_ada.shape
    rows = c_rows.shape[0]
    tn = _largest_divisor(n, (1024, 512, 256, 128))
    vmem = 2 * (_nbytes((d_model, tn), F32) + _nbytes((rows, d_model), F32)) + _nbytes((d_model, tn), F32)
    return pl.pallas_call(
        _ada_kernel,
        grid=(n_layers, n // tn),
        in_specs=[pl.BlockSpec((rows, d_model), lambda l, j: (0, 0)),
                  pl.BlockSpec((None, d_model, tn), lambda l, j: (l, 0, j)),
                  pl.BlockSpec((None, 1, tn), lambda l, j: (l, 0, j))],
        out_specs=pl.BlockSpec((None, rows, tn), lambda l, j: (l, 0, j)),
        out_shape=jax.ShapeDtypeStruct((n_layers, rows, n), F32),
        compiler_params=_params(("arbitrary", "arbitrary"), vmem + 8 * 2**20),
        name="adaln",
    )(c_rows, w_ada, b_ada.reshape(n_layers, 1, n))


def _rmsmod_kernel(x_ref, sc_ref, sh_ref, o_ref):
    x = x_ref[...]
    r = lax.rsqrt(jnp.mean(x * x, axis=-1, keepdims=True) + EPS)
    o_ref[...] = ((x * r) * (1.0 + sc_ref[...]) + sh_ref[...]).astype(o_ref.dtype)


def _rmsmod(g, x, l, i_shift, i_scale):
    m, d = x.shape
    tm = min(g.tm, 512)
    gg = g._replace(tm=tm)
    mn = lambda i: (i, 0)
    return pl.pallas_call(
        _rmsmod_kernel,
        grid=(m // tm,),
        in_specs=[pl.BlockSpec((tm, d), lambda i: (i, 0)),
                  _mod_spec(gg, l, i_scale, d, d, mn),
                  _mod_spec(gg, l, i_shift, d, d, mn)],
        out_specs=pl.BlockSpec((tm, d), lambda i: (i, 0)),
        out_shape=jax.ShapeDtypeStruct((m, d), BF16),
        compiler_params=_params(("arbitrary",), 6 * _nbytes((tm, d), F32) + 8 * 2**20),
        name="rmsmod",
    )(x, g.mod, g.mod)


class Rows(NamedTuple):
    xs: tuple
    extras: tuple
    outs: tuple
    epilogue: Callable
    tm: int


class Weight(NamedTuple):
    array: jax.Array
    spec: pl.BlockSpec
    tile: tuple
    x_index: int
    is_bf16: bool = False


def _tile_spec(tm, tn, rider):
    return pl.BlockSpec((tm, tn), (lambda j, i: (0, j)) if rider else (lambda j, i: (i, j)))


def _grid_mn(rider):
    return (lambda j, i: (0, j)) if rider else (lambda j, i: (i, j))


def _block_bytes(spec, dtype):
    if spec.block_shape is None:
        return 0
    dims = [1 if d is None or isinstance(d, pl.Squeezed) else getattr(d, "block_size", d) for d in spec.block_shape]
    return _nbytes(dims, dtype)


def _cast_weight(w_ref, wb_ref, transposed):
    if transposed:
        for c in range(w_ref.shape[0] // 128):
            wb_ref[:, c * 128:(c + 1) * 128] = w_ref[c * 128:(c + 1) * 128, :].T.astype(BF16)
        return
    rows = w_ref.shape[0]
    rb = _largest_divisor(rows, (512, 256, 128, 64, 32, 16))

    def body(i, c):
        r = pl.multiple_of(i * rb, rb)
        wb_ref[pl.ds(r, rb), :] = w_ref[pl.ds(r, rb), :].astype(BF16)
        return c

    lax.fori_loop(0, rows // rb, body, 0)


def _mm_kernel(*refs, counts, x_index, is_bf16, transposed, epilogues):
    it = iter(refs)
    take = lambda n: [next(it) for _ in range(n)]
    ins = [(take(nx), take(ne)) for nx, ne, _ in counts]
    w_refs = take(len(x_index))
    outs = [take(no) for _, _, no in counts]
    wb_refs = [w_ref if ready else next(it) for w_ref, ready in zip(w_refs, is_bf16)]
    first_row_tile = pl.program_id(1) == 0

    @pl.when(first_row_tile)
    def _():
        for w_ref, wb_ref, ready in zip(w_refs, wb_refs, is_bf16):
            if not ready:
                _cast_weight(w_ref, wb_ref, transposed)

    def run(g):
        x_refs, extra_refs = ins[g]
        accs = [jnp.dot(x_refs[xi][...], wb_ref[...], preferred_element_type=F32)
                for xi, wb_ref in zip(x_index, wb_refs)]
        epilogues[g](accs, extra_refs, outs[g])

    run(0)
    if len(counts) > 1:
        pl.when(first_row_tile)(lambda: run(1))


def _mm(name, weights, groups, n_tiles, transposed=False, alias=None):
    primary = groups[0]
    in_specs, args, out_specs, out_shapes, counts = [], [], [], [], []
    vmem = 0
    for gi, g in enumerate(groups):
        for x in g.xs:
            spec = pl.BlockSpec((g.tm, x.shape[1]), (lambda j, i: (0, 0)) if gi else (lambda j, i: (i, 0)))
            in_specs.append(spec)
            args.append(x)
            vmem += 2 * _block_bytes(spec, x.dtype)
        if gi == 0 and alias is not None:
            aliases = {len(args) + alias[0]: alias[1]}
        for a, spec in g.extras:
            in_specs.append(spec)
            args.append(a)
            vmem += 2 * _block_bytes(spec, a.dtype)
        counts.append((len(g.xs), len(g.extras), len(g.outs)))
    for w in weights:
        in_specs.append(w.spec)
        args.append(w.array)
        vmem += _nbytes(w.tile, BF16) + (0 if w.is_bf16 else 2 * _nbytes(w.tile, F32))
    for g in groups:
        for shape, spec in g.outs:
            out_specs.append(spec)
            out_shapes.append(shape)
            vmem += 2 * _block_bytes(spec, shape.dtype)
    tn = weights[0].tile[1]
    vmem += (len(weights) + 2) * _nbytes((primary.tm, tn), F32)
    m = primary.xs[0].shape[0]
    return pl.pallas_call(
        functools.partial(_mm_kernel, counts=tuple(counts), x_index=tuple(w.x_index for w in weights),
                          is_bf16=tuple(w.is_bf16 for w in weights), transposed=transposed,
                          epilogues=tuple(g.epilogue for g in groups)),
        grid=(n_tiles, m // primary.tm),
        in_specs=in_specs,
        out_specs=out_specs,
        out_shape=out_shapes,
        scratch_shapes=[pltpu.VMEM(w.tile, BF16) for w in weights if not w.is_bf16],
        input_output_aliases=aliases if alias is not None else {},
        compiler_params=_params(("arbitrary", "arbitrary"), vmem + 4 * 2**20),
        name=name,
    )(*args)


def _swiglu_epilogue(accs, extra_refs, out_refs):
    a, b = accs
    out_refs[0][...] = ((a * jax.nn.sigmoid(a)) * b).astype(out_refs[0].dtype)


def _mm_swiglu(hs, w_up, l, tms, col0, ncols, tn):
    k = hs[0].shape[1]
    f = w_up.shape[2] // 2
    assert ncols % tn == 0 and tn % 128 == 0 and col0 % 128 == 0 and f % 128 == 0
    w_spec = lambda base: pl.BlockSpec((pl.Squeezed(), pl.Element(k), pl.Element(tn)),
                                       lambda j, i: (l, 0, pl.multiple_of(base + j * tn, 128)))
    weights = [Weight(w_up, w_spec(col0), (k, tn), 0), Weight(w_up, w_spec(f + col0), (k, tn), 0)]
    groups = [Rows((h,), (), ((jax.ShapeDtypeStruct((h.shape[0], ncols), BF16), _tile_spec(tm, tn, gi > 0)),),
                   _swiglu_epilogue, tm) for gi, (h, tm) in enumerate(zip(hs, tms))]
    return _mm("mm_swiglu", weights, groups, ncols // tn)


def _res_norm_epilogue(half, final, accs, extra_refs, out_refs):
    acc = accs[0]
    for part in accs[1:]:
        acc = acc + part
    res_ref, g_ref = extra_refs[:2]
    gate = 1.0 + g_ref[...]
    if half:
        gate = 0.5 * gate
    x = res_ref[...] + gate * acc
    r = lax.rsqrt(jnp.mean(x * x, axis=-1, keepdims=True) + EPS)
    if final:
        out_refs[0][...] = (x * r) * extra_refs[2][...]
    else:
        sc_ref, sh_ref = extra_refs[2:]
        out_refs[0][...] = x
        out_refs[1][...] = ((x * r) * (1.0 + sc_ref[...]) + sh_ref[...]).astype(out_refs[1].dtype)


def _mm_res_norm(gs, xss, w, l, ress, i_gate, half, nxt, tms):
    d = w.shape[2]
    ks = [x.shape[1] for x in xss[0]]
    row0 = [sum(ks[:i]) for i in range(len(ks))]
    assert sum(ks) == w.shape[1] and all(r % 16 == 0 for r in row0)
    final = not isinstance(nxt, tuple)
    w_spec = lambda k, r: pl.BlockSpec((pl.Squeezed(), pl.Element(k), pl.Element(d)), lambda j, i: (l, r, 0),
                                       pipeline_mode=pl.Buffered(1))
    weights = [Weight(w, w_spec(k, r), (k, d), i, True) for i, (k, r) in enumerate(zip(ks, row0))]
    groups = []
    for gi, (g, xs, res, tm) in enumerate(zip(gs, xss, ress, tms)):
        spec = _tile_spec(tm, d, gi > 0)
        mod = lambda ll, i: (g.mod, _mod_spec(g._replace(tm=tm), ll, i, d, d, _grid_mn(gi > 0)))
        extras = ((res, spec), mod(l, i_gate))
        if final:
            extras += ((nxt.reshape(1, d), pl.BlockSpec((1, d), lambda j, i: (0, 0))),)
            outs = ((jax.ShapeDtypeStruct(res.shape, F32), spec),)
        else:
            extras += (mod(nxt[0], nxt[1]), mod(nxt[0], nxt[2]))
            outs = ((jax.ShapeDtypeStruct(res.shape, F32), spec), (jax.ShapeDtypeStruct(res.shape, BF16), spec))
        groups.append(Rows(tuple(xs), extras, outs, functools.partial(_res_norm_epilogue, half, final), tm))
    return _mm("mm_res_norm", weights, groups, 1)


def _round_kernel(w_ref, o_ref):
    o_ref[...] = w_ref[...].astype(o_ref.dtype)


def _round_bf16(w):
    n_layers, k, n = w.shape
    rb = _largest_divisor(k, (512, 256, 128, 64, 32, 16))
    spec = pl.BlockSpec((None, rb, n), lambda l, i: (l, i, 0))
    return pl.pallas_call(
        _round_kernel,
        grid=(n_layers, k // rb),
        in_specs=[spec],
        out_specs=spec,
        out_shape=jax.ShapeDtypeStruct(w.shape, BF16),
        compiler_params=_params(("arbitrary", "arbitrary"), 2 * (_nbytes((rb, n), F32) + _nbytes((rb, n), BF16))),
        name="round_bf16",
    )(w)


def _proj_epilogue(act, accs, extra_refs, out_refs):
    acc = accs[0]
    if act == "sigmoid":
        acc = jax.nn.sigmoid(acc)
    if len(out_refs) == 2:
        out_refs[0][...] = acc.reshape(out_refs[0].shape)
        out_refs[1][...] = acc.astype(out_refs[1].dtype)
    else:
        out_refs[0][...] = acc.astype(out_refs[0].dtype)


def _mm_proj(hs, w_t, l, col_off, ncols, tms, dtypes, act=None, stacked=None):
    k = hs[0].shape[1]
    assert col_off % SUBLANES == 0
    tn = _largest_divisor(ncols, (1024, 512, 256, 128))
    w_spec = pl.BlockSpec((pl.Squeezed(), pl.Element(tn), pl.Element(k)),
                          lambda j, i: (l, pl.multiple_of(col_off + j * tn, SUBLANES), 0))
    groups = []
    for gi, (h, tm, dt) in enumerate(zip(hs, tms, dtypes)):
        extras = ()
        outs = ((jax.ShapeDtypeStruct((h.shape[0], ncols), dt), _tile_spec(tm, tn, gi > 0)),)
        if gi == 0 and stacked is not None:
            buf, heads, rows_per_batch = stacked
            assert tn == ncols and rows_per_batch % tm == 0
            bpb = rows_per_batch // tm
            extras = ((buf, pl.BlockSpec(memory_space=pl.ANY)),)
            outs = ((jax.ShapeDtypeStruct(buf.shape, buf.dtype),
                     pl.BlockSpec((None, None, tm, heads, ncols // heads),
                                  lambda j, i: (l, i // bpb, i % bpb, 0, 0))),) + outs
        groups.append(Rows((h,), extras, outs, functools.partial(_proj_epilogue, act), tm))
    return _mm("mm_proj", [Weight(w_t, w_spec, (k, tn), 0)], groups, ncols // tn, transposed=True,
               alias=(0, 0) if stacked is not None else None)


def _merge_epilogue(accs, extra_refs, out_refs):
    acc = None
    for part, g_ref in zip(accs, extra_refs):
        part = g_ref[...].astype(F32) * part
        acc = part if acc is None else acc + part
    out_refs[0][...] = acc.astype(out_refs[0].dtype)


def _mm_merge(xss, ws, l, gatess, tms):
    d = ws[0].shape[2]
    tn = _largest_divisor(d, (512, 256, 128))
    nb = d // tn
    weights = [Weight(w, pl.BlockSpec((None, w.shape[1], tn), lambda j, i: (l, 0, j)), (w.shape[1], tn), i)
               for i, w in enumerate(ws)]
    groups = []
    for gi, (xs, gates, tm) in enumerate(zip(xss, gatess, tms)):
        g_spec = lambda s, gi=gi, tm=tm: pl.BlockSpec(
            (tm, tn), (lambda j, i: (0, s * nb + j)) if gi else (lambda j, i: (i, s * nb + j)))
        groups.append(Rows(tuple(xs), tuple((gates, g_spec(s)) for s in range(len(ws))),
                           ((jax.ShapeDtypeStruct((xs[0].shape[0], d), BF16), _tile_spec(tm, tn, gi > 0)),),
                           _merge_epilogue, tm))
    return _mm("mm_merge", weights, groups, nb)


def _lambda_value(lam_ref, lam_init):
    lv = lam_ref[...]
    e1 = jnp.exp(jnp.sum(lv[0:1] * lv[1:2], axis=-1, keepdims=True))
    e2 = jnp.exp(jnp.sum(lv[2:3] * lv[3:4], axis=-1, keepdims=True))
    return e1 - e2 + lam_init


def _subln(o, sub_ref, lam_init):
    r = lax.rsqrt(jnp.mean(o * o, axis=-1, keepdims=True) + EPS)
    return ((o * r) * sub_ref[...]) * (1.0 - lam_init)


def _attn_prompt_kernel(lam_ref, sub_ref, q_ref, k_ref, v_ref, o_ref, *, tq, dk, lam_init):
    t = q_ref.shape[0]
    c = (dk ** -0.5) * math.log2(math.e)
    lam = _lambda_value(lam_ref, lam_init)
    dims = (((1,), (1,)), ((), ()))
    below = (lax.broadcasted_iota(jnp.int32, (tq, tq), 1) <= lax.broadcasted_iota(jnp.int32, (tq, tq), 0))
    for qi in range(t // tq):
        lo, hi = qi * tq, (qi + 1) * tq
        q = q_ref[lo:hi, :]
        maps = []
        for n in range(2):
            cols = slice(n * dk, (n + 1) * dk)
            s_own = jnp.where(below, lax.dot_general(q[:, cols], k_ref[lo:hi, cols], dims,
                                                     preferred_element_type=F32), NEG)
            m = jnp.max(s_own, axis=-1, keepdims=True)
            if qi:
                s_past = lax.dot_general(q[:, cols], k_ref[0:lo, cols], dims, preferred_element_type=F32)
                m = jnp.maximum(m, jnp.max(s_past, axis=-1, keepdims=True))
            e = jnp.exp2((s_own - m) * c)
            l = jnp.sum(e, axis=-1, keepdims=True)
            o = jnp.dot(e.astype(BF16), v_ref[lo:hi, :], preferred_element_type=F32)
            if qi:
                e = jnp.exp2((s_past - m) * c)
                l = l + jnp.sum(e, axis=-1, keepdims=True)
                o = o + jnp.dot(e.astype(BF16), v_ref[0:lo, :], preferred_element_type=F32)
            maps.append(o * (1.0 / l))
        o = maps[0] - lam * maps[1]
        o_ref[lo:hi, :] = _subln(o, sub_ref, lam_init).astype(o_ref.dtype)


def _attn_prompt(q, k, v, lam_vecs, subln, l, nb, t, heads, dk, dv, lam_init):
    tq = _largest_divisor(t, (256, 128, 64, 32, 16, 8))
    blk = lambda w: pl.BlockSpec((t, w), lambda b, h: (b, h))
    vmem = 2 * (2 * _nbytes((t, 2 * dk), BF16) + 2 * _nbytes((t, dv), BF16)) + 8 * _nbytes((tq, t), F32)
    return pl.pallas_call(
        functools.partial(_attn_prompt_kernel, tq=tq, dk=dk, lam_init=lam_init),
        grid=(nb, heads),
        in_specs=[pl.BlockSpec((None, 4, dk), lambda b, h: (l, 0, 0)),
                  pl.BlockSpec((None, 1, dv), lambda b, h: (l, 0, 0)),
                  blk(2 * dk), blk(2 * dk), blk(dv)],
        out_specs=blk(dv),
        out_shape=jax.ShapeDtypeStruct((nb * t, heads * dv), BF16),
        compiler_params=_params(("arbitrary", "arbitrary"), vmem + 8 * 2**20),
        name="attn_prompt",
    )(lam_vecs, subln, q, k, v)


def _attn_sample_kernel(pt_ref, lam_ref, sub_ref, q_ref, kn_ref, vn_ref, *rest,
                        n_pages, heads, dk, lam_init):
    k_refs = rest[:n_pages]
    v_refs = rest[n_pages:2 * n_pages]
    o_ref, m_ref, l_ref, acc_ref, mask_ref = rest[2 * n_pages:]
    p = pl.program_id(1)
    scale = dk ** -0.5
    nq = q_ref.shape[0]
    dims = (((1,), (1,)), ((), ()))
    q = q_ref[...].astype(BF16)

    def scores(keys):
        s = [lax.dot_general(q[:, n * dk:(n + 1) * dk], keys[:, n * dk:(n + 1) * dk], dims,
                             preferred_element_type=F32) for n in range(2)]
        return jnp.concatenate(s, axis=0) * scale

    @pl.when(p == 0)
    def _():
        n_cols = mask_ref.shape[1]
        r = lax.broadcasted_iota(jnp.int32, (2 * nq, n_cols), 0)
        c = lax.broadcasted_iota(jnp.int32, (2 * nq, n_cols), 1)
        mask_ref[...] = jnp.where((r % heads) == (c % heads), 1.0, 0.0)
        r = lax.broadcasted_iota(jnp.int32, (2 * nq, nq), 0) % nq
        c = lax.broadcasted_iota(jnp.int32, (2 * nq, nq), 1)
        ok = ((r % heads) == (c % heads)) & ((c // heads) <= (r // heads))
        s = jnp.where(ok, scores(kn_ref[...].astype(BF16)), NEG)
        m = jnp.max(s, axis=-1, keepdims=True)
        e = jnp.exp(s - m)
        m_ref[...] = m
        l_ref[...] = jnp.sum(e, axis=-1, keepdims=True)
        acc_ref[...] = jnp.dot(e.astype(BF16), vn_ref[...].astype(BF16), preferred_element_type=F32)

    rows = k_refs[0].shape[0] * k_refs[0].shape[1]
    keys = jnp.concatenate([r[...].reshape(rows, 2 * dk).astype(BF16) for r in k_refs], axis=0)
    vals = jnp.concatenate([r[...].reshape(rows, v_refs[0].shape[2]).astype(BF16) for r in v_refs], axis=0)
    s = jnp.where(mask_ref[...] > 0.5, scores(keys), NEG)
    m_old = m_ref[...]
    m_new = jnp.maximum(m_old, jnp.max(s, axis=-1, keepdims=True))
    alpha = jnp.exp(m_old - m_new)
    e = jnp.exp(s - m_new)
    m_ref[...] = m_new
    l_ref[...] = alpha * l_ref[...] + jnp.sum(e, axis=-1, keepdims=True)
    acc_ref[...] = alpha * acc_ref[...] + jnp.dot(e.astype(BF16), vals, preferred_element_type=F32)

    @pl.when(p == pl.num_programs(1) - 1)
    def _():
        o = acc_ref[...] * (1.0 / l_ref[...])
        o = o[0:nq] - _lambda_value(lam_ref, lam_init) * o[nq:2 * nq]
        o_ref[...] = _subln(o, sub_ref, lam_init).astype(o_ref.dtype)


def _attn_sample(q, kn, vn, cache_k, cache_v, page_table, lam_vecs, subln, l, lam_init):
    nb, nq, _ = q.shape
    _, _, page, heads, dk2 = cache_k.shape
    dv = cache_v.shape[4]
    dk = dk2 // 2
    n_used = page_table.shape[1]
    g = _largest_divisor(n_used, (PAGES_PER_STEP, 2, 1))
    page_spec = lambda i, w: pl.BlockSpec((None, None, page, heads, w),
                                          lambda b, p, pt: (l, pt[b, p * g + i], 0, 0, 0))
    row_spec = lambda w: pl.BlockSpec((None, nq, w), lambda b, p, pt: (b, 0, 0))
    vmem = 2 * g * 2 * (_nbytes((page, SUBLANES, max(dk2, dv)), F32))
    vmem += 6 * _nbytes((2 * nq, g * page * heads), F32) + 2 * _nbytes((g * page * heads, dk2 + dv), BF16)
    return pl.pallas_call(
        functools.partial(_attn_sample_kernel, n_pages=g, heads=heads, dk=dk, lam_init=lam_init),
        grid_spec=pltpu.PrefetchScalarGridSpec(
            num_scalar_prefetch=1,
            grid=(nb, n_used // g),
            in_specs=[pl.BlockSpec((None, 4, dk), lambda b, p, pt: (l, 0, 0)),
                      pl.BlockSpec((None, 1, dv), lambda b, p, pt: (l, 0, 0)),
                      row_spec(dk2), row_spec(dk2), row_spec(dv)]
                     + [page_spec(i, dk2) for i in range(g)] + [page_spec(i, dv) for i in range(g)],
            out_specs=row_spec(dv),
            scratch_shapes=[pltpu.VMEM((2 * nq, 1), F32), pltpu.VMEM((2 * nq, 1), F32),
                            pltpu.VMEM((2 * nq, dv), F32), pltpu.VMEM((2 * nq, g * page * heads), F32)]),
        out_shape=jax.ShapeDtypeStruct((nb, nq, dv), BF16),
        compiler_params=_params(("arbitrary", "arbitrary"), vmem + 8 * 2**20),
        name="attn_sample",
    )(page_table, lam_vecs, subln, q, kn, vn, *([cache_k] * g), *([cache_v] * g))


def _split3(a):
    hi = a.astype(BF16)
    r1 = a - hi.astype(F32)
    mid = r1.astype(BF16)
    lo = (r1 - mid.astype(F32)).astype(BF16)
    return hi, mid, lo


def _exact_dot(sel, a, dims):
    out = None
    for part in _split3(a):
        d = lax.dot_general(sel, part, dims, preferred_element_type=F32)
        out = d if out is None else out + d
    return out


def _exact_dot_t(a, ones, dims):
    out = None
    for part in _split3(a):
        d = lax.dot_general(part, ones, dims, preferred_element_type=F32)
        out = d if out is None else out + d
    return out


def _rec_kernel(*refs, chunk, heads, dk, dv, is_gla, has_state):
    it = iter(refs)
    q_ref, k_ref, v_ref, r_ref = next(it), next(it), next(it), next(it)
    if is_gla:
        gl_ref, wg_ref, bg_ref, norm_ref = next(it), next(it), next(it), next(it)
    else:
        cos_ref, sin_ref, loga_ref = next(it), next(it), next(it)
    s0_ref = next(it) if has_state else None
    o_ref, s_out_ref, s_ref = next(it), next(it), next(it)
    t = pl.program_id(1)
    rows = q_ref.shape[0]
    n_chunks = rows // chunk

    @pl.when(t == 0)
    def _():
        s_ref[...] = s0_ref[...] if has_state else jnp.zeros(s_ref.shape, F32)

    q, k, v = q_ref[...], k_ref[...], v_ref[...]
    if is_gla:
        gl = gl_ref[:, 0:wg_ref.shape[0]]
        z = jnp.dot(gl.astype(BF16), wg_ref[...].astype(BF16), preferred_element_type=F32) + bg_ref[...]
        a = jax.nn.log_sigmoid(z) / GLA_TAU
        q = q * (dk ** -0.5)
    else:
        a = jnp.broadcast_to(loga_ref[...], q.shape)
        half = dk // 2
        width = q.shape[1]
        lane = lax.broadcasted_iota(jnp.int32, q.shape, 1)
        first_half = (lane % dk) < half
        cos, sin = cos_ref[...], sin_ref[...]

        def rot(x):
            other = jnp.where(first_half, pltpu.roll(x, width - half, 1), pltpu.roll(x, half, 1))
            return x * cos + other * sin

        q = rot(q)
        k = rot(k) * (dk ** -0.5)

    ri = lax.broadcasted_iota(jnp.int32, (rows, rows), 0)
    ci = lax.broadcasted_iota(jnp.int32, (rows, rows), 1)
    same_chunk = (ri // chunk) == (ci // chunk)
    causal = same_chunk & (ci <= ri)
    mm = (((1,), (0,)), ((), ()))
    b = _exact_dot(jnp.where(causal, 1.0, 0.0).astype(BF16), a, mm)
    b_last = _exact_dot(jnp.where(same_chunk, 1.0, 0.0).astype(BF16), a, mm)
    q_t = q * jnp.exp(b)
    k_t = k * jnp.exp(-b)
    k_end = k * jnp.exp(b_last - b)
    ones = jnp.ones((chunk, dv), BF16)
    tt = (((0,), (0,)), ((), ()))
    nt = (((1,), (1,)), ((), ()))
    outs = []
    for h in range(heads):
        sk = slice(h * dk, (h + 1) * dk)
        sv = slice(h * dv, (h + 1) * dv)
        qh, kh, keh, vh = q_t[:, sk].astype(BF16), k_t[:, sk].astype(BF16), k_end[:, sk].astype(BF16), v[:, sv].astype(BF16)
        attn = jnp.where(causal, lax.dot_general(qh, kh, nt, preferred_element_type=F32), 0.0)
        o = jnp.dot(attn.astype(BF16), vh, preferred_element_type=F32)
        state = s_ref[h]
        inter = []
        for c in range(n_chunks):
            rs = slice(c * chunk, (c + 1) * chunk)
            inter.append(jnp.dot(qh[rs], state.astype(BF16), preferred_element_type=F32))
            decay = jnp.exp(_exact_dot_t(a[rs, sk], ones, tt))
            state = decay * state + lax.dot_general(keh[rs], vh[rs], tt, preferred_element_type=F32)
        s_ref[h] = state
        o = o + (inter[0] if n_chunks == 1 else jnp.concatenate(inter, axis=0))
        o = o * lax.rsqrt(jnp.mean(o * o, axis=-1, keepdims=True) + EPS)
        if is_gla:
            o = o * norm_ref[...]
        gate = r_ref[:, sv]
        outs.append(o * (gate * jax.nn.sigmoid(gate)))
    o_ref[...] = jnp.concatenate(outs, axis=1).astype(o_ref.dtype)

    @pl.when(t == pl.num_programs(1) - 1)
    def _():
        s_out_ref[...] = s_ref[...]


def _recurrence(seg, nb, t, heads, dk, dv, l, *, gla=None, ret=None, state=None):
    is_gla = gla is not None
    chunk = min(CHUNK, t)
    rows = _largest_divisor(t, (4 * chunk, 2 * chunk, chunk))
    nblk = t // rows
    wq, wv = heads * dk, heads * dv
    assert wv == 2 * wq
    row_blk = lambda w, j: pl.BlockSpec((rows, w), lambda b, i: (b * nblk + i, j))
    in_specs = [row_blk(wq, 0), row_blk(wq, 1), row_blk(wv, 1), row_blk(wv, 2)]
    args = [seg, seg, seg, seg]
    if is_gla:
        gl, w_gate, b_gate, norm = gla
        rank = w_gate.shape[1]
        in_specs += [pl.BlockSpec((rows, gl.shape[1]), lambda b, i: (b * nblk + i, 0)),
                     pl.BlockSpec((None, rank, wq), lambda b, i: (l, 0, 0)),
                     pl.BlockSpec((None, 1, wq), lambda b, i: (l, 0, 0)),
                     pl.BlockSpec((None, 1, dv), lambda b, i: (l, 0, 0))]
        args += [gl, w_gate, b_gate, norm]
    else:
        cos, sin, loga = ret
        in_specs += [pl.BlockSpec((rows, wq), lambda b, i: (i, 0)),
                     pl.BlockSpec((rows, wq), lambda b, i: (i, 0)),
                     pl.BlockSpec((1, wq), lambda b, i: (0, 0))]
        args += [cos, sin, loga]
    if state is not None:
        in_specs.append(pl.BlockSpec((None, None, heads, dk, dv), lambda b, i: (l, b, 0, 0, 0)))
        args.append(state)
    vmem = 2 * (2 * _nbytes((rows, wq), F32) + 2 * _nbytes((rows, wv), F32) + _nbytes((rows, wv), BF16))
    vmem += 16 * _nbytes((rows, wq), F32) + 8 * _nbytes((rows, rows), F32)
    return pl.pallas_call(
        functools.partial(_rec_kernel, chunk=chunk, heads=heads, dk=dk, dv=dv, is_gla=is_gla,
                          has_state=state is not None),
        grid=(nb, nblk),
        in_specs=in_specs,
        out_specs=[pl.BlockSpec((rows, wv), lambda b, i: (b * nblk + i, 0)),
                   pl.BlockSpec((None, heads, dk, dv), lambda b, i: (b, 0, 0, 0))],
        out_shape=[jax.ShapeDtypeStruct((nb * t, wv), BF16),
                   jax.ShapeDtypeStruct((nb, heads, dk, dv), F32)],
        scratch_shapes=[pltpu.VMEM((heads, dk, dv), F32)],
        compiler_params=_params(("arbitrary", "arbitrary"), vmem + 8 * 2**20),
        name="gla" if is_gla else "retention",
    )(*args)


def _rope_tables(pos, heads, dk):
    half = dk // 2
    freq = 1.0 / (ROPE_BASE ** jnp.linspace(0.0, 1.0, half, dtype=F32))
    ang = pos.astype(F32)[:, None] * freq[None, :]
    cos, sin = jnp.cos(ang), jnp.sin(ang)
    cos = jnp.tile(jnp.concatenate([cos, cos], axis=-1), (1, heads))
    sin = jnp.tile(jnp.concatenate([-sin, sin], axis=-1), (1, heads))
    return cos, sin


def kernel(x_prompt, x_sample, cache_k, cache_v, state_gla, state_ret, page_table, c_prompt, c_sample,
           w_ada, b_ada, w_ffn1_up, w_ffn1_down, w_in, lambda_q1, lambda_k1, lambda_q2, lambda_k2,
           attn_subln, w_gla_gate, b_gla_gate, gla_norm, w_branch_a, w_branch_b, w_branch_c, w_out,
           w_ffn2_up, w_ffn2_down, final_norm):
    nb_p, t_p, d = x_prompt.shape
    nb_s, t_s, _ = x_sample.shape
    n_layers = w_ada.shape[0]
    a_heads, a_dk2 = cache_k.shape[3:]
    a_dk, a_dv = a_dk2 // 2, cache_v.shape[4]
    b_heads, b_dk, b_dv = state_gla.shape[2:]
    c_heads, c_dk, c_dv = state_ret.shape[2:]
    rank = w_gla_gate.shape[1]
    n_pages = page_table.shape[1]
    past_len = n_pages * cache_k.shape[2]

    w_a = a_heads * a_dk2
    w_av = a_heads * a_dv
    w_b = 2 * b_heads * b_dk + 2 * b_heads * b_dv
    w_c = 2 * c_heads * c_dk + 2 * c_heads * c_dv
    off_b = 2 * w_a + w_av
    off_gl = off_b + w_b
    off_c = off_gl + rank
    off_m = off_c + w_c
    assert off_m + 3 * d == w_in.shape[2]
    assert rank <= 128 and off_gl + 128 <= w_in.shape[2]
    w_in_t = jnp.swapaxes(w_in, 1, 2)

    n_c = nb_p + nb_s
    c_rows = jnp.concatenate([c_prompt, c_sample, jnp.zeros((-n_c % SUBLANES, d), F32)], axis=0)
    mod = _ada_all(c_rows, w_ada, b_ada)
    m_p, m_s = nb_p * t_p, nb_s * t_s
    grp_p = Group(mod[:, :nb_p].reshape(n_layers, nb_p, 1, N_MOD * d), False, m_p, t_p,
                  _largest_divisor(t_p, (1024, 512, 256, 128, 64, 32, 16)))
    grp_s = Group(jnp.repeat(mod[:, nb_p:n_c], t_s, axis=1), True, m_s, t_s, m_s)
    grps = (grp_p, grp_s)
    tms = (grp_p.tm, m_s)
    tms_narrow = (min(grp_p.tm, 512), m_s)

    lam_vecs = jnp.stack([lambda_q1, lambda_k1, lambda_q2, lambda_k2], axis=1)
    subln = attn_subln.reshape(n_layers, 1, a_dv)
    b_gate = b_gla_gate.reshape(n_layers, 1, b_heads * b_dk)
    norm_b = gla_norm.reshape(n_layers, 1, b_dv)
    log_gamma = jnp.log(1.0 - jnp.power(2.0, -5.0 - jnp.arange(c_heads, dtype=F32)))
    loga_c = jnp.repeat(log_gamma, c_dk).reshape(1, c_heads * c_dk)
    rope_p = _rope_tables(jnp.arange(t_p, dtype=jnp.int32), c_heads, c_dk)
    rope_s = _rope_tables(past_len + jnp.arange(t_s, dtype=jnp.int32), c_heads, c_dk)

    w_down1, w_down2, w_out_b = _round_bf16(w_ffn1_down), _round_bf16(w_ffn2_down), _round_bf16(w_out)
    tms_down = (min(grp_p.tm, 256), m_s)

    def ffn(xs, hs, l, w_up, w_down, i_gate, nxt):
        f = w_down.shape[1]
        main = f // 512 * 512 if f > 512 else f
        pieces = [_mm_swiglu(hs, w_up, l, tms, 0, main, min(main, 512))]
        if f > main:
            pieces.append(_mm_swiglu(hs, w_up, l, tms, main, f - main, f - main))
        acts = list(zip(*pieces))
        return _mm_res_norm(grps, acts, w_down, l, xs, i_gate, True, nxt, tms_down)

    def layer(xs, hs, l, kv_p):
        lam_init = 0.8 - 0.6 * math.exp(-0.3 * l)
        x_p, h_p, x_s, h_s = ffn(xs, hs, l, w_ffn1_up, w_down1, 2, (l, 4, 3))
        xs, hs = [x_p, x_s], [h_p, h_s]
        proj = lambda off, n, dts=(F32, F32), act=None, stacked=None: _mm_proj(
            hs, w_in_t, l, off, n, tms, dts, act, stacked)
        q_p, q_s = proj(0, w_a, (BF16, F32))
        k_buf, k_p, k_s = proj(w_a, w_a, (BF16, F32), stacked=(kv_p[0], a_heads, t_p))
        v_buf, v_p, v_s = proj(2 * w_a, w_av, (BF16, F32), stacked=(kv_p[1], a_heads, t_p))
        seg_b = proj(off_b, w_b)
        gl = proj(off_gl, 128)
        seg_c = proj(off_c, w_c)
        gates = proj(off_m, 3 * d, (BF16, BF16), "sigmoid")

        o_a_p = _attn_prompt(q_p, k_p, v_p, lam_vecs, subln, l, nb_p, t_p, a_heads, a_dk, a_dv, lam_init)
        rows = lambda z: z.reshape(nb_s, t_s * a_heads, z.shape[1] // a_heads)
        o_a_s = _attn_sample(rows(q_s), rows(k_s), rows(v_s), cache_k, cache_v, page_table,
                             lam_vecs, subln, l, lam_init).reshape(m_s, w_av)
        o_b_p, gla_p = _recurrence(seg_b[0], nb_p, t_p, b_heads, b_dk, b_dv, l,
                                   gla=(gl[0], w_gla_gate, b_gate, norm_b))
        o_b_s, gla_s = _recurrence(seg_b[1], nb_s, t_s, b_heads, b_dk, b_dv, l,
                                   gla=(gl[1], w_gla_gate, b_gate, norm_b), state=state_gla)
        o_c_p, ret_p = _recurrence(seg_c[0], nb_p, t_p, c_heads, c_dk, c_dv, l, ret=rope_p + (loga_c,))
        o_c_s, ret_s = _recurrence(seg_c[1], nb_s, t_s, c_heads, c_dk, c_dv, l, ret=rope_s + (loga_c,),
                                   state=state_ret)

        merged = _mm_merge([(o_a_p, o_b_p, o_c_p), (o_a_s, o_b_s, o_c_s)],
                           (w_branch_a, w_branch_b, w_branch_c), l, gates, tms)
        x_p, h_p, x_s, h_s = _mm_res_norm(grps, [(merged[0],), (merged[1],)], w_out_b, l, xs, 5, False,
                                          (l, 7, 6), tms_narrow)
        last = l == n_layers - 1
        out = ffn([x_p, x_s], [h_p, h_s], l, w_ffn2_up, w_down2, 8, final_norm if last else (l + 1, 1, 0))
        return out, (k_buf, v_buf), (k_s, v_s, gla_p, gla_s, ret_p, ret_s)

    xs = [x_prompt.reshape(m_p, d), x_sample.reshape(m_s, d)]
    hs = [_rmsmod(g, x, 0, 0, 1) for g, x in zip(grps, xs)]
    kv_p = (jnp.zeros((n_layers, nb_p, t_p, a_heads, a_dk2), F32),
            jnp.zeros((n_layers, nb_p, t_p, a_heads, a_dv), F32))
    per_layer = []
    for l in range(n_layers):
        out, kv_p, new = layer(xs, hs, l, kv_p)
        per_layer.append(new)
        if l < n_layers - 1:
            xs, hs = [out[0], out[2]], [out[1], out[3]]
    y_p = out[0].reshape(nb_p, t_p, d)
    y_s = out[1].reshape(nb_s, t_s, d)

    def stacked(i, shape):
        return jnp.stack([new[i] for new in per_layer]).reshape((n_layers,) + shape)

    return (y_p, y_s, kv_p[0], kv_p[1],
            stacked(0, (nb_s, t_s, a_heads, a_dk2)), stacked(1, (nb_s, t_s, a_heads, a_dv)),
            stacked(2, (nb_p, b_heads, b_dk, b_dv)), stacked(3, (nb_s, b_heads, b_dk, b_dv)),
            stacked(4, (nb_p, c_heads, c_dk, c_dv)), stacked(5, (nb_s, c_heads, c_dk, c_dv)))
```

```python
import functools
import math
from typing import Callable, NamedTuple

import jax
import jax.numpy as jnp
from jax import lax
from jax.experimental import pallas as pl
from jax.experimental.pallas import tpu as pltpu

EPS = 1e-6
CHUNK = 64
GLA_TAU = 16.0
ROPE_BASE = 10000.0
N_MOD = 9
BF16 = jnp.bfloat16
F32 = jnp.float32
NEG = float(jnp.finfo(jnp.float32).min)

V7X_VMEM_BYTES = 64 * 2**20
VMEM_CEILING = V7X_VMEM_BYTES - 6 * 2**20
SUBLANES = 8
PAGES_PER_STEP = 16


def _params(sem, vmem_bytes):
    return pltpu.CompilerParams(
        dimension_semantics=sem,
        vmem_limit_bytes=int(min(VMEM_CEILING, max(vmem_bytes, 16 * 2**20))))


def _largest_divisor(n, cands):
    for c in cands:
        if n % c == 0:
            return c
    raise ValueError(f"no tile in {cands} divides {n}")


def _nbytes(shape, dtype):
    return math.prod(shape) * jnp.dtype(dtype).itemsize


class Group(NamedTuple):
    mod: jax.Array
    per_row: bool
    rows: int
    rows_per_batch: int
    tm: int


def _mod_spec(g, l, i, d_model, tn, mn):
    nb = d_model // tn
    if g.per_row:
        return pl.BlockSpec((None, g.tm, tn), lambda *a: (l, mn(*a)[0], i * nb + mn(*a)[1]))
    blocks_per_batch = g.rows_per_batch // g.tm
    return pl.BlockSpec((None, None, 1, tn),
                        lambda *a: (l, mn(*a)[0] // blocks_per_batch, 0, i * nb + mn(*a)[1]))


def _ada_kernel(c_ref, w_ref, b_ref, o_ref):
    c = c_ref[...]
    s = (c * jax.nn.sigmoid(c)).astype(BF16)
    o_ref[...] = jnp.dot(s, w_ref[...].astype(BF16), preferred_element_type=F32) + b_ref[...]


def _ada_all(c_rows, w_ada, b_ada):
    n_layers, d_model, n = w_ada.shape
    rows = c_rows.shape[0]
    tn = _largest_divisor(n, (1024, 512, 256, 128))
    vmem = 2 * (_nbytes((d_model, tn), F32) + _nbytes((rows, d_model), F32)) + _nbytes((d_model, tn), F32)
    return pl.pallas_call(
        _ada_kernel,
        grid=(n_layers, n // tn),
        in_specs=[pl.BlockSpec((rows, d_model), lambda l, j: (0, 0)),
                  pl.BlockSpec((None, d_model, tn), lambda l, j: (l, 0, j)),
                  pl.BlockSpec((None, 1, tn), lambda l, j: (l, 0, j))],
        out_specs=pl.BlockSpec((None, rows, tn), lambda l, j: (l, 0, j)),
        out_shape=jax.ShapeDtypeStruct((n_layers, rows, n), F32),
        compiler_params=_params(("arbitrary", "arbitrary"), vmem + 8 * 2**20),
        name="adaln",
    )(c_rows, w_ada, b_ada.reshape(n_layers, 1, n))


def _rmsmod_kernel(x_ref, sc_ref, sh_ref, o_ref):
    x = x_ref[...]
    r = lax.rsqrt(jnp.mean(x * x, axis=-1, keepdims=True) + EPS)
    o_ref[...] = ((x * r) * (1.0 + sc_ref[...]) + sh_ref[...]).astype(o_ref.dtype)


def _rmsmod(g, x, l, i_shift, i_scale):
    m, d = x.shape
    tm = min(g.tm, 512)
    gg = g._replace(tm=tm)
    mn = lambda i: (i, 0)
    return pl.pallas_call(
        _rmsmod_kernel,
        grid=(m // tm,),
        in_specs=[pl.BlockSpec((tm, d), lambda i: (i, 0)),
                  _mod_spec(gg, l, i_scale, d, d, mn),
                  _mod_spec(gg, l, i_shift, d, d, mn)],
        out_specs=pl.BlockSpec((tm, d), lambda i: (i, 0)),
        out_shape=jax.ShapeDtypeStruct((m, d), BF16),
        compiler_params=_params(("arbitrary",), 6 * _nbytes((tm, d), F32) + 8 * 2**20),
        name="rmsmod",
    )(x, g.mod, g.mod)


class Rows(NamedTuple):
    xs: tuple
    extras: tuple
    outs: tuple
    epilogue: Callable
    tm: int


class Weight(NamedTuple):
    array: jax.Array
    spec: pl.BlockSpec
    tile: tuple
    x_index: int


def _tile_spec(tm, tn, rider):
    return pl.BlockSpec((tm, tn), (lambda j, i: (0, j)) if rider else (lambda j, i: (i, j)))


def _grid_mn(rider):
    return (lambda j, i: (0, j)) if rider else (lambda j, i: (i, j))


def _block_bytes(spec, dtype):
    if spec.block_shape is None:
        return 0
    dims = [1 if d is None or isinstance(d, pl.Squeezed) else getattr(d, "block_size", d) for d in spec.block_shape]
    return _nbytes(dims, dtype)


def _cast_weight(w_ref, wb_ref, transposed):
    if transposed:
        for c in range(w_ref.shape[0] // 128):
            wb_ref[:, c * 128:(c + 1) * 128] = w_ref[c * 128:(c + 1) * 128, :].T.astype(BF16)
        return
    rows = w_ref.shape[0]
    rb = _largest_divisor(rows, (512, 256, 128, 64, 32, 16))

    def body(i, c):
        r = pl.multiple_of(i * rb, rb)
        wb_ref[pl.ds(r, rb), :] = w_ref[pl.ds(r, rb), :].astype(BF16)
        return c

    lax.fori_loop(0, rows // rb, body, 0)


def _mm_kernel(*refs, counts, x_index, transposed, epilogues):
    it = iter(refs)
    take = lambda n: [next(it) for _ in range(n)]
    ins = [(take(nx), take(ne)) for nx, ne, _ in counts]
    w_refs = take(len(x_index))
    outs = [take(no) for _, _, no in counts]
    wb_refs = take(len(x_index))
    first_row_tile = pl.program_id(1) == 0

    @pl.when(first_row_tile)
    def _():
        for w_ref, wb_ref in zip(w_refs, wb_refs):
            _cast_weight(w_ref, wb_ref, transposed)

    def run(g):
        x_refs, extra_refs = ins[g]
        accs = [jnp.dot(x_refs[xi][...], wb_ref[...], preferred_element_type=F32)
                for xi, wb_ref in zip(x_index, wb_refs)]
        epilogues[g](accs, extra_refs, outs[g])

    run(0)
    if len(counts) > 1:
        pl.when(first_row_tile)(lambda: run(1))


def _mm(name, weights, groups, n_tiles, transposed=False, alias=None):
    primary = groups[0]
    in_specs, args, out_specs, out_shapes, counts = [], [], [], [], []
    vmem = 0
    for gi, g in enumerate(groups):
        for x in g.xs:
            spec = pl.BlockSpec((g.tm, x.shape[1]), (lambda j, i: (0, 0)) if gi else (lambda j, i: (i, 0)))
            in_specs.append(spec)
            args.append(x)
            vmem += 2 * _block_bytes(spec, x.dtype)
        if gi == 0 and alias is not None:
            aliases = {len(args) + alias[0]: alias[1]}
        for a, spec in g.extras:
            in_specs.append(spec)
            args.append(a)
            vmem += 2 * _block_bytes(spec, a.dtype)
        counts.append((len(g.xs), len(g.extras), len(g.outs)))
    for w in weights:
        in_specs.append(w.spec)
        args.append(w.array)
        vmem += 2 * _nbytes(w.tile, F32) + _nbytes(w.tile, BF16)
    for g in groups:
        for shape, spec in g.outs:
            out_specs.append(spec)
            out_shapes.append(shape)
            vmem += 2 * _block_bytes(spec, shape.dtype)
    tn = weights[0].tile[1]
    vmem += (len(weights) + 2) * _nbytes((primary.tm, tn), F32)
    m = primary.xs[0].shape[0]
    return pl.pallas_call(
        functools.partial(_mm_kernel, counts=tuple(counts), x_index=tuple(w.x_index for w in weights),
                          transposed=transposed, epilogues=tuple(g.epilogue for g in groups)),
        grid=(n_tiles, m // primary.tm),
        in_specs=in_specs,
        out_specs=out_specs,
        out_shape=out_shapes,
        scratch_shapes=[pltpu.VMEM(w.tile, BF16) for w in weights],
        input_output_aliases=aliases if alias is not None else {},
        compiler_params=_params(("arbitrary", "arbitrary"), vmem + 4 * 2**20),
        name=name,
    )(*args)


def _swiglu_epilogue(accs, extra_refs, out_refs):
    a, b = accs
    out_refs[0][...] = ((a * jax.nn.sigmoid(a)) * b).astype(out_refs[0].dtype)


def _mm_swiglu(hs, w_up, l, tms, col0, ncols, tn):
    k = hs[0].shape[1]
    f = w_up.shape[2] // 2
    assert ncols % tn == 0 and tn % 128 == 0 and col0 % 128 == 0 and f % 128 == 0
    w_spec = lambda base: pl.BlockSpec((pl.Squeezed(), pl.Element(k), pl.Element(tn)),
                                       lambda j, i: (l, 0, pl.multiple_of(base + j * tn, 128)))
    weights = [Weight(w_up, w_spec(col0), (k, tn), 0), Weight(w_up, w_spec(f + col0), (k, tn), 0)]
    groups = [Rows((h,), (), ((jax.ShapeDtypeStruct((h.shape[0], ncols), BF16), _tile_spec(tm, tn, gi > 0)),),
                   _swiglu_epilogue, tm) for gi, (h, tm) in enumerate(zip(hs, tms))]
    return _mm("mm_swiglu", weights, groups, ncols // tn)


def _res_norm_kernel(*refs, counts, row0, n_stage, rc, half, final):
    it = iter(refs)
    take = lambda n: [next(it) for _ in range(n)]
    ins = [(take(nx), take(ne)) for nx, ne, _ in counts]
    w_ref = next(it)
    outs = [take(no) for _, _, no in counts]
    wb_ref = next(it)
    step = pl.program_id(0)

    @pl.when(step < n_stage)
    def _():
        wb_ref[pl.ds(pl.multiple_of(step * rc, rc), rc), :] = w_ref[...].astype(BF16)

    def run(g):
        x_refs, extra_refs = ins[g]
        acc = None
        for x_ref, r0 in zip(x_refs, row0):
            part = jnp.dot(x_ref[...], wb_ref[r0:r0 + x_ref.shape[1], :], preferred_element_type=F32)
            acc = part if acc is None else acc + part
        res_ref, g_ref = extra_refs[:2]
        gate = 1.0 + g_ref[...]
        if half:
            gate = 0.5 * gate
        x = res_ref[...] + gate * acc
        r = lax.rsqrt(jnp.mean(x * x, axis=-1, keepdims=True) + EPS)
        if final:
            outs[g][0][...] = (x * r) * extra_refs[2][...]
        else:
            sc_ref, sh_ref = extra_refs[2:]
            outs[g][0][...] = x
            outs[g][1][...] = ((x * r) * (1.0 + sc_ref[...]) + sh_ref[...]).astype(outs[g][1].dtype)

    pl.when(step >= n_stage)(lambda: run(0))
    if len(counts) > 1:
        pl.when(step == n_stage)(lambda: run(1))


def _mm_res_norm(gs, xss, w, l, ress, i_gate, half, nxt, tms):
    k, d = w.shape[1:]
    ks = [x.shape[1] for x in xss[0]]
    row0 = tuple(sum(ks[:i]) for i in range(len(ks)))
    assert sum(ks) == k and all(r % 16 == 0 for r in row0)
    final = not isinstance(nxt, tuple)
    rc = _largest_divisor(k, (256, 128, 64, 32, 16))
    n_stage = k // rc
    in_specs, args, out_specs, out_shapes, counts = [], [], [], [], []
    vmem = _nbytes((k, d), BF16) + 2 * _nbytes((rc, d), F32)
    for gi, (g, xs, res, tm) in enumerate(zip(gs, xss, ress, tms)):
        row = (lambda s: 0) if gi else (lambda s: jnp.maximum(s - n_stage, 0))
        blk = lambda w_: pl.BlockSpec((tm, w_), lambda s, row=row: (row(s), 0))
        mod = lambda ll, i, g=g, tm=tm, row=row: _mod_spec(g._replace(tm=tm), ll, i, d, d, lambda s: (row(s), 0))
        specs = [blk(x.shape[1]) for x in xs] + [blk(d), mod(l, i_gate)]
        ops = list(xs) + [res, g.mod]
        if final:
            specs.append(pl.BlockSpec((1, d), lambda s: (0, 0)))
            ops.append(nxt.reshape(1, d))
            outs = [jax.ShapeDtypeStruct(res.shape, F32)]
        else:
            specs += [mod(nxt[0], nxt[1]), mod(nxt[0], nxt[2])]
            ops += [g.mod, g.mod]
            outs = [jax.ShapeDtypeStruct(res.shape, F32), jax.ShapeDtypeStruct(res.shape, BF16)]
        in_specs += specs
        args += ops
        out_specs += [blk(d)] * len(outs)
        out_shapes += outs
        counts.append((len(xs), len(specs) - len(xs), len(outs)))
        vmem += 2 * sum(_nbytes((tm, x.shape[1]), BF16) for x in xs) + (2 * 2 + 2 + 4) * _nbytes((tm, d), F32)
    in_specs.append(pl.BlockSpec((None, rc, d), lambda s: (l, jnp.minimum(s, n_stage - 1), 0)))
    args.append(w)
    return pl.pallas_call(
        functools.partial(_res_norm_kernel, counts=tuple(counts), row0=row0, n_stage=n_stage, rc=rc,
                          half=half, final=final),
        grid=(n_stage + xss[0][0].shape[0] // tms[0],),
        in_specs=in_specs,
        out_specs=out_specs,
        out_shape=out_shapes,
        scratch_shapes=[pltpu.VMEM((k, d), BF16)],
        compiler_params=_params(("arbitrary",), vmem + 4 * 2**20),
        name="mm_res_norm",
    )(*args)


def _proj_epilogue(act, accs, extra_refs, out_refs):
    acc = accs[0]
    if act == "sigmoid":
        acc = jax.nn.sigmoid(acc)
    if len(out_refs) == 2:
        out_refs[0][...] = acc.reshape(out_refs[0].shape)
        out_refs[1][...] = acc.astype(out_refs[1].dtype)
    else:
        out_refs[0][...] = acc.astype(out_refs[0].dtype)


def _mm_proj(hs, w_t, l, col_off, ncols, tms, dtypes, act=None, stacked=None):
    k = hs[0].shape[1]
    assert col_off % SUBLANES == 0
    tn = _largest_divisor(ncols, (1024, 512, 256, 128))
    w_spec = pl.BlockSpec((pl.Squeezed(), pl.Element(tn), pl.Element(k)),
                          lambda j, i: (l, pl.multiple_of(col_off + j * tn, SUBLANES), 0))
    groups = []
    for gi, (h, tm, dt) in enumerate(zip(hs, tms, dtypes)):
        extras = ()
        outs = ((jax.ShapeDtypeStruct((h.shape[0], ncols), dt), _tile_spec(tm, tn, gi > 0)),)
        if gi == 0 and stacked is not None:
            buf, heads, rows_per_batch = stacked
            assert tn == ncols and rows_per_batch % tm == 0
            bpb = rows_per_batch // tm
            extras = ((buf, pl.BlockSpec(memory_space=pl.ANY)),)
            outs = ((jax.ShapeDtypeStruct(buf.shape, buf.dtype),
                     pl.BlockSpec((None, None, tm, heads, ncols // heads),
                                  lambda j, i: (l, i // bpb, i % bpb, 0, 0))),) + outs
        groups.append(Rows((h,), extras, outs, functools.partial(_proj_epilogue, act), tm))
    return _mm("mm_proj", [Weight(w_t, w_spec, (k, tn), 0)], groups, ncols // tn, transposed=True,
               alias=(0, 0) if stacked is not None else None)


def _merge_epilogue(accs, extra_refs, out_refs):
    acc = None
    for part, g_ref in zip(accs, extra_refs):
        part = g_ref[...].astype(F32) * part
        acc = part if acc is None else acc + part
    out_refs[0][...] = acc.astype(out_refs[0].dtype)


def _mm_merge(xss, ws, l, gatess, tms):
    d = ws[0].shape[2]
    tn = _largest_divisor(d, (512, 256, 128))
    nb = d // tn
    weights = [Weight(w, pl.BlockSpec((None, w.shape[1], tn), lambda j, i: (l, 0, j)), (w.shape[1], tn), i)
               for i, w in enumerate(ws)]
    groups = []
    for gi, (xs, gates, tm) in enumerate(zip(xss, gatess, tms)):
        g_spec = lambda s, gi=gi, tm=tm: pl.BlockSpec(
            (tm, tn), (lambda j, i: (0, s * nb + j)) if gi else (lambda j, i: (i, s * nb + j)))
        groups.append(Rows(tuple(xs), tuple((gates, g_spec(s)) for s in range(len(ws))),
                           ((jax.ShapeDtypeStruct((xs[0].shape[0], d), BF16), _tile_spec(tm, tn, gi > 0)),),
                           _merge_epilogue, tm))
    return _mm("mm_merge", weights, groups, nb)


def _lambda_value(lam_ref, lam_init):
    lv = lam_ref[...]
    e1 = jnp.exp(jnp.sum(lv[0:1] * lv[1:2], axis=-1, keepdims=True))
    e2 = jnp.exp(jnp.sum(lv[2:3] * lv[3:4], axis=-1, keepdims=True))
    return e1 - e2 + lam_init


def _subln(o, sub_ref, lam_init):
    r = lax.rsqrt(jnp.mean(o * o, axis=-1, keepdims=True) + EPS)
    return ((o * r) * sub_ref[...]) * (1.0 - lam_init)


def _attn_prompt_kernel(lam_ref, sub_ref, q_ref, k_ref, v_ref, o_ref, *, tq, dk, lam_init):
    t = q_ref.shape[0]
    c = (dk ** -0.5) * math.log2(math.e)
    lam = _lambda_value(lam_ref, lam_init)
    dims = (((1,), (1,)), ((), ()))
    below = (lax.broadcasted_iota(jnp.int32, (tq, tq), 1) <= lax.broadcasted_iota(jnp.int32, (tq, tq), 0))
    for qi in range(t // tq):
        lo, hi = qi * tq, (qi + 1) * tq
        q = q_ref[lo:hi, :]
        maps = []
        for n in range(2):
            cols = slice(n * dk, (n + 1) * dk)
            s_own = jnp.where(below, lax.dot_general(q[:, cols], k_ref[lo:hi, cols], dims,
                                                     preferred_element_type=F32), NEG)
            m = jnp.max(s_own, axis=-1, keepdims=True)
            if qi:
                s_past = lax.dot_general(q[:, cols], k_ref[0:lo, cols], dims, preferred_element_type=F32)
                m = jnp.maximum(m, jnp.max(s_past, axis=-1, keepdims=True))
            e = jnp.exp2((s_own - m) * c)
            l = jnp.sum(e, axis=-1, keepdims=True)
            o = jnp.dot(e.astype(BF16), v_ref[lo:hi, :], preferred_element_type=F32)
            if qi:
                e = jnp.exp2((s_past - m) * c)
                l = l + jnp.sum(e, axis=-1, keepdims=True)
                o = o + jnp.dot(e.astype(BF16), v_ref[0:lo, :], preferred_element_type=F32)
            maps.append(o * (1.0 / l))
        o = maps[0] - lam * maps[1]
        o_ref[lo:hi, :] = _subln(o, sub_ref, lam_init).astype(o_ref.dtype)


def _attn_prompt(q, k, v, lam_vecs, subln, l, nb, t, heads, dk, dv, lam_init):
    tq = _largest_divisor(t, (256, 128, 64, 32, 16, 8))
    blk = lambda w: pl.BlockSpec((t, w), lambda b, h: (b, h))
    vmem = 2 * (2 * _nbytes((t, 2 * dk), BF16) + 2 * _nbytes((t, dv), BF16)) + 8 * _nbytes((tq, t), F32)
    return pl.pallas_call(
        functools.partial(_attn_prompt_kernel, tq=tq, dk=dk, lam_init=lam_init),
        grid=(nb, heads),
        in_specs=[pl.BlockSpec((None, 4, dk), lambda b, h: (l, 0, 0)),
                  pl.BlockSpec((None, 1, dv), lambda b, h: (l, 0, 0)),
                  blk(2 * dk), blk(2 * dk), blk(dv)],
        out_specs=blk(dv),
        out_shape=jax.ShapeDtypeStruct((nb * t, heads * dv), BF16),
        compiler_params=_params(("arbitrary", "arbitrary"), vmem + 8 * 2**20),
        name="attn_prompt",
    )(lam_vecs, subln, q, k, v)


def _attn_sample_kernel(pt_ref, lam_ref, sub_ref, q_ref, kn_ref, vn_ref, *rest,
                        n_pages, heads, dk, lam_init):
    k_refs = rest[:n_pages]
    v_refs = rest[n_pages:2 * n_pages]
    o_ref, m_ref, l_ref, acc_ref, mask_ref = rest[2 * n_pages:]
    p = pl.program_id(1)
    scale = dk ** -0.5
    nq = q_ref.shape[0]
    dims = (((1,), (1,)), ((), ()))
    q = q_ref[...].astype(BF16)

    def scores(keys):
        s = [lax.dot_general(q[:, n * dk:(n + 1) * dk], keys[:, n * dk:(n + 1) * dk], dims,
                             preferred_element_type=F32) for n in range(2)]
        return jnp.concatenate(s, axis=0) * scale

    @pl.when(p == 0)
    def _():
        n_cols = mask_ref.shape[1]
        r = lax.broadcasted_iota(jnp.int32, (2 * nq, n_cols), 0)
        c = lax.broadcasted_iota(jnp.int32, (2 * nq, n_cols), 1)
        mask_ref[...] = jnp.where((r % heads) == (c % heads), 1.0, 0.0)
        r = lax.broadcasted_iota(jnp.int32, (2 * nq, nq), 0) % nq
        c = lax.broadcasted_iota(jnp.int32, (2 * nq, nq), 1)
        ok = ((r % heads) == (c % heads)) & ((c // heads) <= (r // heads))
        s = jnp.where(ok, scores(kn_ref[...].astype(BF16)), NEG)
        m = jnp.max(s, axis=-1, keepdims=True)
        e = jnp.exp(s - m)
        m_ref[...] = m
        l_ref[...] = jnp.sum(e, axis=-1, keepdims=True)
        acc_ref[...] = jnp.dot(e.astype(BF16), vn_ref[...].astype(BF16), preferred_element_type=F32)

    rows = k_refs[0].shape[0] * k_refs[0].shape[1]
    keys = jnp.concatenate([r[...].reshape(rows, 2 * dk).astype(BF16) for r in k_refs], axis=0)
    vals = jnp.concatenate([r[...].reshape(rows, v_refs[0].shape[2]).astype(BF16) for r in v_refs], axis=0)
    s = jnp.where(mask_ref[...] > 0.5, scores(keys), NEG)
    m_old = m_ref[...]
    m_new = jnp.maximum(m_old, jnp.max(s, axis=-1, keepdims=True))
    alpha = jnp.exp(m_old - m_new)
    e = jnp.exp(s - m_new)
    m_ref[...] = m_new
    l_ref[...] = alpha * l_ref[...] + jnp.sum(e, axis=-1, keepdims=True)
    acc_ref[...] = alpha * acc_ref[...] + jnp.dot(e.astype(BF16), vals, preferred_element_type=F32)

    @pl.when(p == pl.num_programs(1) - 1)
    def _():
        o = acc_ref[...] * (1.0 / l_ref[...])
        o = o[0:nq] - _lambda_value(lam_ref, lam_init) * o[nq:2 * nq]
        o_ref[...] = _subln(o, sub_ref, lam_init).astype(o_ref.dtype)


def _attn_sample(q, kn, vn, cache_k, cache_v, page_table, lam_vecs, subln, l, lam_init):
    nb, nq, _ = q.shape
    _, _, page, heads, dk2 = cache_k.shape
    dv = cache_v.shape[4]
    dk = dk2 // 2
    n_used = page_table.shape[1]
    g = _largest_divisor(n_used, (PAGES_PER_STEP, 2, 1))
    page_spec = lambda i, w: pl.BlockSpec((None, None, page, heads, w),
                                          lambda b, p, pt: (l, pt[b, p * g + i], 0, 0, 0))
    row_spec = lambda w: pl.BlockSpec((None, nq, w), lambda b, p, pt: (b, 0, 0))
    vmem = 2 * g * 2 * (_nbytes((page, SUBLANES, max(dk2, dv)), F32))
    vmem += 6 * _nbytes((2 * nq, g * page * heads), F32) + 2 * _nbytes((g * page * heads, dk2 + dv), BF16)
    return pl.pallas_call(
        functools.partial(_attn_sample_kernel, n_pages=g, heads=heads, dk=dk, lam_init=lam_init),
        grid_spec=pltpu.PrefetchScalarGridSpec(
            num_scalar_prefetch=1,
            grid=(nb, n_used // g),
            in_specs=[pl.BlockSpec((None, 4, dk), lambda b, p, pt: (l, 0, 0)),
                      pl.BlockSpec((None, 1, dv), lambda b, p, pt: (l, 0, 0)),
                      row_spec(dk2), row_spec(dk2), row_spec(dv)]
                     + [page_spec(i, dk2) for i in range(g)] + [page_spec(i, dv) for i in range(g)],
            out_specs=row_spec(dv),
            scratch_shapes=[pltpu.VMEM((2 * nq, 1), F32), pltpu.VMEM((2 * nq, 1), F32),
                            pltpu.VMEM((2 * nq, dv), F32), pltpu.VMEM((2 * nq, g * page * heads), F32)]),
        out_shape=jax.ShapeDtypeStruct((nb, nq, dv), BF16),
        compiler_params=_params(("arbitrary", "arbitrary"), vmem + 8 * 2**20),
        name="attn_sample",
    )(page_table, lam_vecs, subln, q, kn, vn, *([cache_k] * g), *([cache_v] * g))


def _split3(a):
    hi = a.astype(BF16)
    r1 = a - hi.astype(F32)
    mid = r1.astype(BF16)
    lo = (r1 - mid.astype(F32)).astype(BF16)
    return hi, mid, lo


def _exact_dot(sel, a, dims):
    out = None
    for part in _split3(a):
        d = lax.dot_general(sel, part, dims, preferred_element_type=F32)
        out = d if out is None else out + d
    return out


def _exact_dot_t(a, ones, dims):
    out = None
    for part in _split3(a):
        d = lax.dot_general(part, ones, dims, preferred_element_type=F32)
        out = d if out is None else out + d
    return out


def _rec_kernel(*refs, chunk, heads, dk, dv, is_gla, has_state):
    it = iter(refs)
    q_ref, k_ref, v_ref, r_ref = next(it), next(it), next(it), next(it)
    if is_gla:
        gl_ref, wg_ref, bg_ref, norm_ref = next(it), next(it), next(it), next(it)
    else:
        cos_ref, sin_ref, loga_ref = next(it), next(it), next(it)
    s0_ref = next(it) if has_state else None
    o_ref, s_out_ref, s_ref = next(it), next(it), next(it)
    t = pl.program_id(1)
    rows = q_ref.shape[0]
    n_chunks = rows // chunk

    @pl.when(t == 0)
    def _():
        s_ref[...] = s0_ref[...] if has_state else jnp.zeros(s_ref.shape, F32)

    q, k, v = q_ref[...], k_ref[...], v_ref[...]
    if is_gla:
        gl = gl_ref[:, 0:wg_ref.shape[0]]
        z = jnp.dot(gl.astype(BF16), wg_ref[...].astype(BF16), preferred_element_type=F32) + bg_ref[...]
        a = jax.nn.log_sigmoid(z) / GLA_TAU
        q = q * (dk ** -0.5)
    else:
        a = jnp.broadcast_to(loga_ref[...], q.shape)
        half = dk // 2
        width = q.shape[1]
        lane = lax.broadcasted_iota(jnp.int32, q.shape, 1)
        first_half = (lane % dk) < half
        cos, sin = cos_ref[...], sin_ref[...]

        def rot(x):
            other = jnp.where(first_half, pltpu.roll(x, width - half, 1), pltpu.roll(x, half, 1))
            return x * cos + other * sin

        q = rot(q)
        k = rot(k) * (dk ** -0.5)

    ri = lax.broadcasted_iota(jnp.int32, (rows, rows), 0)
    ci = lax.broadcasted_iota(jnp.int32, (rows, rows), 1)
    same_chunk = (ri // chunk) == (ci // chunk)
    causal = same_chunk & (ci <= ri)
    mm = (((1,), (0,)), ((), ()))
    b = _exact_dot(jnp.where(causal, 1.0, 0.0).astype(BF16), a, mm)
    b_last = _exact_dot(jnp.where(same_chunk, 1.0, 0.0).astype(BF16), a, mm)
    q_t = q * jnp.exp(b)
    k_t = k * jnp.exp(-b)
    k_end = k * jnp.exp(b_last - b)
    ones = jnp.ones((chunk, dv), BF16)
    tt = (((0,), (0,)), ((), ()))
    nt = (((1,), (1,)), ((), ()))
    outs = []
    for h in range(heads):
        sk = slice(h * dk, (h + 1) * dk)
        sv = slice(h * dv, (h + 1) * dv)
        qh, kh, keh, vh = q_t[:, sk].astype(BF16), k_t[:, sk].astype(BF16), k_end[:, sk].astype(BF16), v[:, sv].astype(BF16)
        attn = jnp.where(causal, lax.dot_general(qh, kh, nt, preferred_element_type=F32), 0.0)
        o = jnp.dot(attn.astype(BF16), vh, preferred_element_type=F32)
        state = s_ref[h]
        inter = []
        for c in range(n_chunks):
            rs = slice(c * chunk, (c + 1) * chunk)
            inter.append(jnp.dot(qh[rs], state.astype(BF16), preferred_element_type=F32))
            decay = jnp.exp(_exact_dot_t(a[rs, sk], ones, tt))
            state = decay * state + lax.dot_general(keh[rs], vh[rs], tt, preferred_element_type=F32)
        s_ref[h] = state
        o = o + (inter[0] if n_chunks == 1 else jnp.concatenate(inter, axis=0))
        o = o * lax.rsqrt(jnp.mean(o * o, axis=-1, keepdims=True) + EPS)
        if is_gla:
            o = o * norm_ref[...]
        gate = r_ref[:, sv]
        outs.append(o * (gate * jax.nn.sigmoid(gate)))
    o_ref[...] = jnp.concatenate(outs, axis=1).astype(o_ref.dtype)

    @pl.when(t == pl.num_programs(1) - 1)
    def _():
        s_out_ref[...] = s_ref[...]


def _recurrence(seg, nb, t, heads, dk, dv, l, *, gla=None, ret=None, state=None):
    is_gla = gla is not None
    chunk = min(CHUNK, t)
    rows = _largest_divisor(t, (4 * chunk, 2 * chunk, chunk))
    nblk = t // rows
    wq, wv = heads * dk, heads * dv
    assert wv == 2 * wq
    row_blk = lambda w, j: pl.BlockSpec((rows, w), lambda b, i: (b * nblk + i, j))
    in_specs = [row_blk(wq, 0), row_blk(wq, 1), row_blk(wv, 1), row_blk(wv, 2)]
    args = [seg, seg, seg, seg]
    if is_gla:
        gl, w_gate, b_gate, norm = gla
        rank = w_gate.shape[1]
        in_specs += [pl.BlockSpec((rows, gl.shape[1]), lambda b, i: (b * nblk + i, 0)),
                     pl.BlockSpec((None, rank, wq), lambda b, i: (l, 0, 0)),
                     pl.BlockSpec((None, 1, wq), lambda b, i: (l, 0, 0)),
                     pl.BlockSpec((None, 1, dv), lambda b, i: (l, 0, 0))]
        args += [gl, w_gate, b_gate, norm]
    else:
        cos, sin, loga = ret
        in_specs += [pl.BlockSpec((rows, wq), lambda b, i: (i, 0)),
                     pl.BlockSpec((rows, wq), lambda b, i: (i, 0)),
                     pl.BlockSpec((1, wq), lambda b, i: (0, 0))]
        args += [cos, sin, loga]
    if state is not None:
        in_specs.append(pl.BlockSpec((None, None, heads, dk, dv), lambda b, i: (l, b, 0, 0, 0)))
        args.append(state)
    vmem = 2 * (2 * _nbytes((rows, wq), F32) + 2 * _nbytes((rows, wv), F32) + _nbytes((rows, wv), BF16))
    vmem += 16 * _nbytes((rows, wq), F32) + 8 * _nbytes((rows, rows), F32)
    return pl.pallas_call(
        functools.partial(_rec_kernel, chunk=chunk, heads=heads, dk=dk, dv=dv, is_gla=is_gla,
                          has_state=state is not None),
        grid=(nb, nblk),
        in_specs=in_specs,
        out_specs=[pl.BlockSpec((rows, wv), lambda b, i: (b * nblk + i, 0)),
                   pl.BlockSpec((None, heads, dk, dv), lambda b, i: (b, 0, 0, 0))],
        out_shape=[jax.ShapeDtypeStruct((nb * t, wv), BF16),
                   jax.ShapeDtypeStruct((nb, heads, dk, dv), F32)],
        scratch_shapes=[pltpu.VMEM((heads, dk, dv), F32)],
        compiler_params=_params(("arbitrary", "arbitrary"), vmem + 8 * 2**20),
        name="gla" if is_gla else "retention",
    )(*args)


def _rope_tables(pos, heads, dk):
    half = dk // 2
    freq = 1.0 / (ROPE_BASE ** jnp.linspace(0.0, 1.0, half, dtype=F32))
    ang = pos.astype(F32)[:, None] * freq[None, :]
    cos, sin = jnp.cos(ang), jnp.sin(ang)
    cos = jnp.tile(jnp.concatenate([cos, cos], axis=-1), (1, heads))
    sin = jnp.tile(jnp.concatenate([-sin, sin], axis=-1), (1, heads))
    return cos, sin


def kernel(x_prompt, x_sample, cache_k, cache_v, state_gla, state_ret, page_table, c_prompt, c_sample,
           w_ada, b_ada, w_ffn1_up, w_ffn1_down, w_in, lambda_q1, lambda_k1, lambda_q2, lambda_k2,
           attn_subln, w_gla_gate, b_gla_gate, gla_norm, w_branch_a, w_branch_b, w_branch_c, w_out,
           w_ffn2_up, w_ffn2_down, final_norm):
    nb_p, t_p, d = x_prompt.shape
    nb_s, t_s, _ = x_sample.shape
    n_layers = w_ada.shape[0]
    a_heads, a_dk2 = cache_k.shape[3:]
    a_dk, a_dv = a_dk2 // 2, cache_v.shape[4]
    b_heads, b_dk, b_dv = state_gla.shape[2:]
    c_heads, c_dk, c_dv = state_ret.shape[2:]
    rank = w_gla_gate.shape[1]
    n_pages = page_table.shape[1]
    past_len = n_pages * cache_k.shape[2]

    w_a = a_heads * a_dk2
    w_av = a_heads * a_dv
    w_b = 2 * b_heads * b_dk + 2 * b_heads * b_dv
    w_c = 2 * c_heads * c_dk + 2 * c_heads * c_dv
    off_b = 2 * w_a + w_av
    off_gl = off_b + w_b
    off_c = off_gl + rank
    off_m = off_c + w_c
    assert off_m + 3 * d == w_in.shape[2]
    assert rank <= 128 and off_gl + 128 <= w_in.shape[2]
    w_in_t = jnp.swapaxes(w_in, 1, 2)

    n_c = nb_p + nb_s
    c_rows = jnp.concatenate([c_prompt, c_sample, jnp.zeros((-n_c % SUBLANES, d), F32)], axis=0)
    mod = _ada_all(c_rows, w_ada, b_ada)
    m_p, m_s = nb_p * t_p, nb_s * t_s
    grp_p = Group(mod[:, :nb_p].reshape(n_layers, nb_p, 1, N_MOD * d), False, m_p, t_p,
                  _largest_divisor(t_p, (1024, 512, 256, 128, 64, 32, 16)))
    grp_s = Group(jnp.repeat(mod[:, nb_p:n_c], t_s, axis=1), True, m_s, t_s, m_s)
    grps = (grp_p, grp_s)
    tms = (grp_p.tm, m_s)
    tms_narrow = (min(grp_p.tm, 512), m_s)

    lam_vecs = jnp.stack([lambda_q1, lambda_k1, lambda_q2, lambda_k2], axis=1)
    subln = attn_subln.reshape(n_layers, 1, a_dv)
    b_gate = b_gla_gate.reshape(n_layers, 1, b_heads * b_dk)
    norm_b = gla_norm.reshape(n_layers, 1, b_dv)
    log_gamma = jnp.log(1.0 - jnp.power(2.0, -5.0 - jnp.arange(c_heads, dtype=F32)))
    loga_c = jnp.repeat(log_gamma, c_dk).reshape(1, c_heads * c_dk)
    rope_p = _rope_tables(jnp.arange(t_p, dtype=jnp.int32), c_heads, c_dk)
    rope_s = _rope_tables(past_len + jnp.arange(t_s, dtype=jnp.int32), c_heads, c_dk)

    tms_down = (min(grp_p.tm, 256), m_s)

    def ffn(xs, hs, l, w_up, w_down, i_gate, nxt):
        f = w_down.shape[1]
        main = f // 512 * 512 if f > 512 else f
        pieces = [_mm_swiglu(hs, w_up, l, tms, 0, main, min(main, 512))]
        if f > main:
            pieces.append(_mm_swiglu(hs, w_up, l, tms, main, f - main, f - main))
        acts = list(zip(*pieces))
        return _mm_res_norm(grps, acts, w_down, l, xs, i_gate, True, nxt, tms_down)

    def layer(xs, hs, l, kv_p):
        lam_init = 0.8 - 0.6 * math.exp(-0.3 * l)
        x_p, h_p, x_s, h_s = ffn(xs, hs, l, w_ffn1_up, w_ffn1_down, 2, (l, 4, 3))
        xs, hs = [x_p, x_s], [h_p, h_s]
        proj = lambda off, n, dts=(F32, F32), act=None, stacked=None: _mm_proj(
            hs, w_in_t, l, off, n, tms, dts, act, stacked)
        q_p, q_s = proj(0, w_a, (BF16, F32))
        k_buf, k_p, k_s = proj(w_a, w_a, (BF16, F32), stacked=(kv_p[0], a_heads, t_p))
        v_buf, v_p, v_s = proj(2 * w_a, w_av, (BF16, F32), stacked=(kv_p[1], a_heads, t_p))
        seg_b = proj(off_b, w_b)
        gl = proj(off_gl, 128)
        seg_c = proj(off_c, w_c)
        gates = proj(off_m, 3 * d, (BF16, BF16), "sigmoid")

        o_a_p = _attn_prompt(q_p, k_p, v_p, lam_vecs, subln, l, nb_p, t_p, a_heads, a_dk, a_dv, lam_init)
        rows = lambda z: z.reshape(nb_s, t_s * a_heads, z.shape[1] // a_heads)
        o_a_s = _attn_sample(rows(q_s), rows(k_s), rows(v_s), cache_k, cache_v, page_table,
                             lam_vecs, subln, l, lam_init).reshape(m_s, w_av)
        o_b_p, gla_p = _recurrence(seg_b[0], nb_p, t_p, b_heads, b_dk, b_dv, l,
                                   gla=(gl[0], w_gla_gate, b_gate, norm_b))
        o_b_s, gla_s = _recurrence(seg_b[1], nb_s, t_s, b_heads, b_dk, b_dv, l,
                                   gla=(gl[1], w_gla_gate, b_gate, norm_b), state=state_gla)
        o_c_p, ret_p = _recurrence(seg_c[0], nb_p, t_p, c_heads, c_dk, c_dv, l, ret=rope_p + (loga_c,))
        o_c_s, ret_s = _recurrence(seg_c[1], nb_s, t_s, c_heads, c_dk, c_dv, l, ret=rope_s + (loga_c,),
                                   state=state_ret)

        merged = _mm_merge([(o_a_p, o_b_p, o_c_p), (o_a_s, o_b_s, o_c_s)],
                           (w_branch_a, w_branch_b, w_branch_c), l, gates, tms)
        x_p, h_p, x_s, h_s = _mm_res_norm(grps, [(merged[0],), (merged[1],)], w_out, l, xs, 5, False,
                                          (l, 7, 6), tms_narrow)
        last = l == n_layers - 1
        out = ffn([x_p, x_s], [h_p, h_s], l, w_ffn2_up, w_ffn2_down, 8, final_norm if last else (l + 1, 1, 0))
        return out, (k_buf, v_buf), (k_s, v_s, gla_p, gla_s, ret_p, ret_s)

    xs = [x_prompt.reshape(m_p, d), x_sample.reshape(m_s, d)]
    hs = [_rmsmod(g, x, 0, 0, 1) for g, x in zip(grps, xs)]
    kv_p = (jnp.zeros((n_layers, nb_p, t_p, a_heads, a_dk2), F32),
            jnp.zeros((n_layers, nb_p, t_p, a_heads, a_dv), F32))
    per_layer = []
    for l in range(n_layers):
        out, kv_p, new = layer(xs, hs, l, kv_p)
        per_layer.append(new)
        if l < n_layers - 1:
            xs, hs = [out[0], out[2]], [out[1], out[3]]
    y_p = out[0].reshape(nb_p, t_p, d)
    y_s = out[1].reshape(nb_s, t_s, d)

    def stacked(i, shape):
        return jnp.stack([new[i] for new in per_layer]).reshape((n_layers,) + shape)

    return (y_p, y_s, kv_p[0], kv_p[1],
            stacked(0, (nb_s, t_s, a_heads, a_dk2)), stacked(1, (nb_s, t_s, a_heads, a_dv)),
            stacked(2, (nb_p, b_heads, b_dk, b_dv)), stacked(3, (nb_s, b_heads, b_dk, b_dv)),
            stacked(4, (nb_p, c_heads, c_dk, c_dv)), stacked(5, (nb_s, c_heads, c_dk, c_dv)))
```

```python
import functools
import math
from typing import Callable, NamedTuple

import jax
import jax.numpy as jnp
from jax import lax
from jax.experimental import pallas as pl
from jax.experimental.pallas import tpu as pltpu

EPS = 1e-6
CHUNK = 64
GLA_TAU = 16.0
ROPE_BASE = 10000.0
N_MOD = 9
BF16 = jnp.bfloat16
F32 = jnp.float32
NEG = float(jnp.finfo(jnp.float32).min)

V7X_VMEM_BYTES = 64 * 2**20
VMEM_CEILING = V7X_VMEM_BYTES - 6 * 2**20
SUBLANES = 8
PAGES_PER_STEP = 16


def _params(sem, vmem_bytes):
    return pltpu.CompilerParams(
        dimension_semantics=sem,
        vmem_limit_bytes=int(min(VMEM_CEILING, max(vmem_bytes, 16 * 2**20))))


def _largest_divisor(n, cands):
    for c in cands:
        if n % c == 0:
            return c
    raise ValueError(f"no tile in {cands} divides {n}")


def _nbytes(shape, dtype):
    return math.prod(shape) * jnp.dtype(dtype).itemsize


class Group(NamedTuple):
    mod: jax.Array
    per_row: bool
    rows: int
    rows_per_batch: int
    tm: int


def _mod_spec(g, l, i, d_model, tn, mn):
    nb = d_model // tn
    if g.per_row:
        return pl.BlockSpec((None, g.tm, tn), lambda *a: (l, mn(*a)[0], i * nb + mn(*a)[1]))
    blocks_per_batch = g.rows_per_batch // g.tm
    return pl.BlockSpec((None, None, 1, tn),
                        lambda *a: (l, mn(*a)[0] // blocks_per_batch, 0, i * nb + mn(*a)[1]))


def _ada_kernel(c_ref, w_ref, b_ref, o_ref):
    c = c_ref[...]
    s = (c * jax.nn.sigmoid(c)).astype(BF16)
    o_ref[...] = jnp.dot(s, w_ref[...].astype(BF16), preferred_element_type=F32) + b_ref[...]


def _ada_all(c_rows, w_ada, b_ada):
    n_layers, d_model, n = w_ada.shape
    rows = c_rows.shape[0]
    tn = _largest_divisor(n, (1024, 512, 256, 128))
    vmem = 2 * (_nbytes((d_model, tn), F32) + _nbytes((rows, d_model), F32)) + _nbytes((d_model, tn), F32)
    return pl.pallas_call(
        _ada_kernel,
        grid=(n_layers, n // tn),
        in_specs=[pl.BlockSpec((rows, d_model), lambda l, j: (0, 0)),
                  pl.BlockSpec((None, d_model, tn), lambda l, j: (l, 0, j)),
                  pl.BlockSpec((None, 1, tn), lambda l, j: (l, 0, j))],
        out_specs=pl.BlockSpec((None, rows, tn), lambda l, j: (l, 0, j)),
        out_shape=jax.ShapeDtypeStruct((n_layers, rows, n), F32),
        compiler_params=_params(("arbitrary", "arbitrary"), vmem + 8 * 2**20),
        name="adaln",
    )(c_rows, w_ada, b_ada.reshape(n_layers, 1, n))


def _rmsmod_kernel(x_ref, sc_ref, sh_ref, o_ref):
    x = x_ref[...]
    r = lax.rsqrt(jnp.mean(x * x, axis=-1, keepdims=True) + EPS)
    o_ref[...] = ((x * r) * (1.0 + sc_ref[...]) + sh_ref[...]).astype(o_ref.dtype)


def _rmsmod(g, x, l, i_shift, i_scale):
    m, d = x.shape
    tm = min(g.tm, 512)
    gg = g._replace(tm=tm)
    mn = lambda i: (i, 0)
    return pl.pallas_call(
        _rmsmod_kernel,
        grid=(m // tm,),
        in_specs=[pl.BlockSpec((tm, d), lambda i: (i, 0)),
                  _mod_spec(gg, l, i_scale, d, d, mn),
                  _mod_spec(gg, l, i_shift, d, d, mn)],
        out_specs=pl.BlockSpec((tm, d), lambda i: (i, 0)),
        out_shape=jax.ShapeDtypeStruct((m, d), BF16),
        compiler_params=_params(("arbitrary",), 6 * _nbytes((tm, d), F32) + 8 * 2**20),
        name="rmsmod",
    )(x, g.mod, g.mod)


class Rows(NamedTuple):
    xs: tuple
    extras: tuple
    outs: tuple
    epilogue: Callable
    tm: int


class Weight(NamedTuple):
    array: jax.Array
    spec: pl.BlockSpec
    tile: tuple
    x_index: int


def _tile_spec(tm, tn, rider):
    return pl.BlockSpec((tm, tn), (lambda j, i: (0, j)) if rider else (lambda j, i: (i, j)))


def _grid_mn(rider):
    return (lambda j, i: (0, j)) if rider else (lambda j, i: (i, j))


def _block_bytes(spec, dtype):
    if spec.block_shape is None:
        return 0
    dims = [1 if d is None or isinstance(d, pl.Squeezed) else getattr(d, "block_size", d) for d in spec.block_shape]
    return _nbytes(dims, dtype)


def _cast_weight(w_ref, wb_ref, transposed):
    if transposed:
        for c in range(w_ref.shape[0] // 128):
            wb_ref[:, c * 128:(c + 1) * 128] = w_ref[c * 128:(c + 1) * 128, :].T.astype(BF16)
        return
    rows = w_ref.shape[0]
    rb = _largest_divisor(rows, (512, 256, 128, 64, 32, 16))

    def body(i, c):
        r = pl.multiple_of(i * rb, rb)
        wb_ref[pl.ds(r, rb), :] = w_ref[pl.ds(r, rb), :].astype(BF16)
        return c

    lax.fori_loop(0, rows // rb, body, 0)


def _mm_kernel(*refs, counts, x_index, transposed, epilogues):
    it = iter(refs)
    take = lambda n: [next(it) for _ in range(n)]
    ins = [(take(nx), take(ne)) for nx, ne, _ in counts]
    w_refs = take(len(x_index))
    outs = [take(no) for _, _, no in counts]
    wb_refs = take(len(x_index))
    first_row_tile = pl.program_id(1) == 0

    @pl.when(first_row_tile)
    def _():
        for w_ref, wb_ref in zip(w_refs, wb_refs):
            _cast_weight(w_ref, wb_ref, transposed)

    def run(g):
        x_refs, extra_refs = ins[g]
        accs = [jnp.dot(x_refs[xi][...], wb_ref[...], preferred_element_type=F32)
                for xi, wb_ref in zip(x_index, wb_refs)]
        epilogues[g](accs, extra_refs, outs[g])

    run(0)
    if len(counts) > 1:
        pl.when(first_row_tile)(lambda: run(1))


def _mm(name, weights, groups, n_tiles, transposed=False, alias=None):
    primary = groups[0]
    in_specs, args, out_specs, out_shapes, counts = [], [], [], [], []
    vmem = 0
    for gi, g in enumerate(groups):
        for x in g.xs:
            spec = pl.BlockSpec((g.tm, x.shape[1]), (lambda j, i: (0, 0)) if gi else (lambda j, i: (i, 0)))
            in_specs.append(spec)
            args.append(x)
            vmem += 2 * _block_bytes(spec, x.dtype)
        if gi == 0 and alias is not None:
            aliases = {len(args) + alias[0]: alias[1]}
        for a, spec in g.extras:
            in_specs.append(spec)
            args.append(a)
            vmem += 2 * _block_bytes(spec, a.dtype)
        counts.append((len(g.xs), len(g.extras), len(g.outs)))
    for w in weights:
        in_specs.append(w.spec)
        args.append(w.array)
        vmem += 2 * _nbytes(w.tile, F32) + _nbytes(w.tile, BF16)
    for g in groups:
        for shape, spec in g.outs:
            out_specs.append(spec)
            out_shapes.append(shape)
            vmem += 2 * _block_bytes(spec, shape.dtype)
    tn = weights[0].tile[1]
    vmem += (len(weights) + 2) * _nbytes((primary.tm, tn), F32)
    m = primary.xs[0].shape[0]
    return pl.pallas_call(
        functools.partial(_mm_kernel, counts=tuple(counts), x_index=tuple(w.x_index for w in weights),
                          transposed=transposed, epilogues=tuple(g.epilogue for g in groups)),
        grid=(n_tiles, m // primary.tm),
        in_specs=in_specs,
        out_specs=out_specs,
        out_shape=out_shapes,
        scratch_shapes=[pltpu.VMEM(w.tile, BF16) for w in weights],
        input_output_aliases=aliases if alias is not None else {},
        compiler_params=_params(("arbitrary", "arbitrary"), vmem + 4 * 2**20),
        name=name,
    )(*args)


def _swiglu_epilogue(tail, accs, extra_refs, out_refs):
    a, b = accs
    if tail:
        tn = b.shape[1]
        shifted = jnp.concatenate([b[:, tn - tail:], b[:, :tn - tail]], axis=1)
        b = jnp.where(pl.program_id(0) == pl.num_programs(0) - 1, shifted, b)
    out_refs[0][...] = ((a * jax.nn.sigmoid(a)) * b).astype(out_refs[0].dtype)


def _mm_swiglu(hs, w_up, l, tms):
    k = hs[0].shape[1]
    f = w_up.shape[2] // 2
    tn = min(512, f)
    nt = pl.cdiv(f, tn)
    tail = f % tn
    assert tn % 128 == 0 and tail % 128 == 0 and nt * tn <= 2 * f
    back = lambda j: jnp.where(j == nt - 1, tn - tail, 0) if tail else 0
    w_spec = lambda base, moved: pl.BlockSpec(
        (pl.Squeezed(), pl.Element(k), pl.Element(tn)),
        lambda j, i: (l, 0, pl.multiple_of(base + j * tn - (back(j) if moved else 0), 128)))
    weights = [Weight(w_up, w_spec(0, False), (k, tn), 0), Weight(w_up, w_spec(f, True), (k, tn), 0)]
    groups = [Rows((h,), (), ((jax.ShapeDtypeStruct((h.shape[0], f), BF16), _tile_spec(tm, tn, gi > 0)),),
                   functools.partial(_swiglu_epilogue, tail), tm) for gi, (h, tm) in enumerate(zip(hs, tms))]
    return _mm("mm_swiglu", weights, groups, nt)


def _res_norm_kernel(*refs, counts, row0, n_stage, n_rows, half, final):
    it = iter(refs)
    take = lambda n: [next(it) for _ in range(n)]
    ins = [(take(nx), take(ne)) for nx, ne, _ in counts]
    w_ref = next(it)
    outs = [take(no) for _, _, no in counts]
    wb_ref = next(it)
    step = pl.program_id(0)

    rc, dc = w_ref.shape
    for c in range(n_stage // n_rows):
        @pl.when((step >= c * n_rows) & (step < (c + 1) * n_rows))
        def _():
            r = pl.multiple_of((step - c * n_rows) * rc, rc)
            wb_ref[pl.ds(r, rc), c * dc:(c + 1) * dc] = w_ref[...].astype(BF16)

    def run(g):
        x_refs, extra_refs = ins[g]
        acc = None
        for x_ref, r0 in zip(x_refs, row0):
            part = jnp.dot(x_ref[...], wb_ref[r0:r0 + x_ref.shape[1], :], preferred_element_type=F32)
            acc = part if acc is None else acc + part
        res_ref, g_ref = extra_refs[:2]
        gate = 1.0 + g_ref[...]
        if half:
            gate = 0.5 * gate
        x = res_ref[...] + gate * acc
        r = lax.rsqrt(jnp.mean(x * x, axis=-1, keepdims=True) + EPS)
        if final:
            outs[g][0][...] = (x * r) * extra_refs[2][...]
        else:
            sc_ref, sh_ref = extra_refs[2:]
            outs[g][0][...] = x
            outs[g][1][...] = ((x * r) * (1.0 + sc_ref[...]) + sh_ref[...]).astype(outs[g][1].dtype)

    pl.when(step >= n_stage)(lambda: run(0))
    if len(counts) > 1:
        pl.when(step == n_stage)(lambda: run(1))


def _mm_res_norm(gs, xss, w, l, ress, i_gate, half, nxt, tms):
    k, d = w.shape[1:]
    ks = [x.shape[1] for x in xss[0]]
    row0 = tuple(sum(ks[:i]) for i in range(len(ks)))
    assert sum(ks) == k and all(r % 16 == 0 for r in row0)
    final = not isinstance(nxt, tuple)
    dc = d // 2 if d % 256 == 0 else d
    rc = next(r for r in range(k, 0, -16) if k % r == 0 and r % 16 == 0 and _nbytes((r, dc), F32) <= 3 * 2**20)
    n_rows = k // rc
    n_stage = n_rows * (d // dc)
    in_specs, args, out_specs, out_shapes, counts = [], [], [], [], []
    vmem = _nbytes((k, d), BF16) + 2 * _nbytes((rc, dc), F32)
    for gi, (g, xs, res, tm) in enumerate(zip(gs, xss, ress, tms)):
        row = (lambda s: 0) if gi else (lambda s: jnp.maximum(s - n_stage, 0))
        blk = lambda w_: pl.BlockSpec((tm, w_), lambda s, row=row: (row(s), 0))
        mod = lambda ll, i, g=g, tm=tm, row=row: _mod_spec(g._replace(tm=tm), ll, i, d, d, lambda s: (row(s), 0))
        specs = [blk(x.shape[1]) for x in xs] + [blk(d), mod(l, i_gate)]
        ops = list(xs) + [res, g.mod]
        if final:
            specs.append(pl.BlockSpec((1, d), lambda s: (0, 0)))
            ops.append(nxt.reshape(1, d))
            outs = [jax.ShapeDtypeStruct(res.shape, F32)]
        else:
            specs += [mod(nxt[0], nxt[1]), mod(nxt[0], nxt[2])]
            ops += [g.mod, g.mod]
            outs = [jax.ShapeDtypeStruct(res.shape, F32), jax.ShapeDtypeStruct(res.shape, BF16)]
        in_specs += specs
        args += ops
        out_specs += [blk(d)] * len(outs)
        out_shapes += outs
        counts.append((len(xs), len(specs) - len(xs), len(outs)))
        vmem += 2 * sum(_nbytes((tm, x.shape[1]), BF16) for x in xs) + (2 * 2 + 2 + 4) * _nbytes((tm, d), F32)
    stage = lambda s: jnp.minimum(s, n_stage - 1)
    in_specs.append(pl.BlockSpec((None, rc, dc), lambda s: (l, stage(s) % n_rows, stage(s) // n_rows)))
    args.append(w)
    return pl.pallas_call(
        functools.partial(_res_norm_kernel, counts=tuple(counts), row0=row0, n_stage=n_stage, n_rows=n_rows,
                          half=half, final=final),
        grid=(n_stage + xss[0][0].shape[0] // tms[0],),
        in_specs=in_specs,
        out_specs=out_specs,
        out_shape=out_shapes,
        scratch_shapes=[pltpu.VMEM((k, d), BF16)],
        compiler_params=_params(("arbitrary",), vmem + 4 * 2**20),
        name="mm_res_norm",
    )(*args)


def _proj_epilogue(act, accs, extra_refs, out_refs):
    acc = accs[0]
    if act == "sigmoid":
        acc = jax.nn.sigmoid(acc)
    if len(out_refs) == 2:
        out_refs[0][...] = acc.reshape(out_refs[0].shape)
        out_refs[1][...] = acc.astype(out_refs[1].dtype)
    else:
        out_refs[0][...] = acc.astype(out_refs[0].dtype)


def _mm_proj(hs, w_t, l, col_off, ncols, tms, dtypes, act=None, stacked=None):
    k = hs[0].shape[1]
    assert col_off % SUBLANES == 0
    tn = _largest_divisor(ncols, (1024, 512, 256, 128))
    w_spec = pl.BlockSpec((pl.Squeezed(), pl.Element(tn), pl.Element(k)),
                          lambda j, i: (l, pl.multiple_of(col_off + j * tn, SUBLANES), 0))
    groups = []
    for gi, (h, tm, dt) in enumerate(zip(hs, tms, dtypes)):
        extras = ()
        outs = ((jax.ShapeDtypeStruct((h.shape[0], ncols), dt), _tile_spec(tm, tn, gi > 0)),)
        if gi == 0 and stacked is not None:
            buf, heads, rows_per_batch = stacked
            assert tn == ncols and rows_per_batch % tm == 0
            bpb = rows_per_batch // tm
            extras = ((buf, pl.BlockSpec(memory_space=pl.ANY)),)
            outs = ((jax.ShapeDtypeStruct(buf.shape, buf.dtype),
                     pl.BlockSpec((None, None, tm, heads, ncols // heads),
                                  lambda j, i: (l, i // bpb, i % bpb, 0, 0))),) + outs
        groups.append(Rows((h,), extras, outs, functools.partial(_proj_epilogue, act), tm))
    return _mm("mm_proj", [Weight(w_t, w_spec, (k, tn), 0)], groups, ncols // tn, transposed=True,
               alias=(0, 0) if stacked is not None else None)


def _merge_epilogue(accs, extra_refs, out_refs):
    acc = None
    for part, g_ref in zip(accs, extra_refs):
        part = g_ref[...].astype(F32) * part
        acc = part if acc is None else acc + part
    out_refs[0][...] = acc.astype(out_refs[0].dtype)


def _mm_merge(xss, ws, l, gatess, tms):
    d = ws[0].shape[2]
    tn = _largest_divisor(d, (512, 256, 128))
    nb = d // tn
    weights = [Weight(w, pl.BlockSpec((None, w.shape[1], tn), lambda j, i: (l, 0, j)), (w.shape[1], tn), i)
               for i, w in enumerate(ws)]
    groups = []
    for gi, (xs, gates, tm) in enumerate(zip(xss, gatess, tms)):
        g_spec = lambda s, gi=gi, tm=tm: pl.BlockSpec(
            (tm, tn), (lambda j, i: (0, s * nb + j)) if gi else (lambda j, i: (i, s * nb + j)))
        groups.append(Rows(tuple(xs), tuple((gates, g_spec(s)) for s in range(len(ws))),
                           ((jax.ShapeDtypeStruct((xs[0].shape[0], d), BF16), _tile_spec(tm, tn, gi > 0)),),
                           _merge_epilogue, tm))
    return _mm("mm_merge", weights, groups, nb)


def _lambda_value(lam_ref, lam_init):
    lv = lam_ref[...]
    e1 = jnp.exp(jnp.sum(lv[0:1] * lv[1:2], axis=-1, keepdims=True))
    e2 = jnp.exp(jnp.sum(lv[2:3] * lv[3:4], axis=-1, keepdims=True))
    return e1 - e2 + lam_init


def _subln(o, sub_ref, lam_init):
    r = lax.rsqrt(jnp.mean(o * o, axis=-1, keepdims=True) + EPS)
    return ((o * r) * sub_ref[...]) * (1.0 - lam_init)


def _attn_prompt_kernel(lam_ref, sub_ref, q_ref, k_ref, v_ref, o_ref, *, tq, dk, lam_init):
    t = q_ref.shape[0]
    c = (dk ** -0.5) * math.log2(math.e)
    lam = _lambda_value(lam_ref, lam_init)
    dims = (((1,), (1,)), ((), ()))
    below = (lax.broadcasted_iota(jnp.int32, (tq, tq), 1) <= lax.broadcasted_iota(jnp.int32, (tq, tq), 0))
    for qi in range(t // tq):
        lo, hi = qi * tq, (qi + 1) * tq
        q = q_ref[lo:hi, :]
        maps = []
        for n in range(2):
            cols = slice(n * dk, (n + 1) * dk)
            s_own = jnp.where(below, lax.dot_general(q[:, cols], k_ref[lo:hi, cols], dims,
                                                     preferred_element_type=F32), NEG)
            m = jnp.max(s_own, axis=-1, keepdims=True)
            if qi:
                s_past = lax.dot_general(q[:, cols], k_ref[0:lo, cols], dims, preferred_element_type=F32)
                m = jnp.maximum(m, jnp.max(s_past, axis=-1, keepdims=True))
            e = jnp.exp2((s_own - m) * c)
            l = jnp.sum(e, axis=-1, keepdims=True)
            o = jnp.dot(e.astype(BF16), v_ref[lo:hi, :], preferred_element_type=F32)
            if qi:
                e = jnp.exp2((s_past - m) * c)
                l = l + jnp.sum(e, axis=-1, keepdims=True)
                o = o + jnp.dot(e.astype(BF16), v_ref[0:lo, :], preferred_element_type=F32)
            maps.append(o * (1.0 / l))
        o = maps[0] - lam * maps[1]
        o_ref[lo:hi, :] = _subln(o, sub_ref, lam_init).astype(o_ref.dtype)


def _attn_prompt(q, k, v, lam_vecs, subln, l, nb, t, heads, dk, dv, lam_init):
    tq = _largest_divisor(t, (256, 128, 64, 32, 16, 8))
    blk = lambda w: pl.BlockSpec((t, w), lambda b, h: (b, h))
    vmem = 2 * (2 * _nbytes((t, 2 * dk), BF16) + 2 * _nbytes((t, dv), BF16)) + 8 * _nbytes((tq, t), F32)
    return pl.pallas_call(
        functools.partial(_attn_prompt_kernel, tq=tq, dk=dk, lam_init=lam_init),
        grid=(nb, heads),
        in_specs=[pl.BlockSpec((None, 4, dk), lambda b, h: (l, 0, 0)),
                  pl.BlockSpec((None, 1, dv), lambda b, h: (l, 0, 0)),
                  blk(2 * dk), blk(2 * dk), blk(dv)],
        out_specs=blk(dv),
        out_shape=jax.ShapeDtypeStruct((nb * t, heads * dv), BF16),
        compiler_params=_params(("arbitrary", "arbitrary"), vmem + 8 * 2**20),
        name="attn_prompt",
    )(lam_vecs, subln, q, k, v)


def _attn_sample_kernel(pt_ref, lam_ref, sub_ref, q_ref, kn_ref, vn_ref, *rest,
                        n_pages, heads, dk, lam_init):
    k_refs = rest[:n_pages]
    v_refs = rest[n_pages:2 * n_pages]
    o_ref, m_ref, l_ref, acc_ref, mask_ref = rest[2 * n_pages:]
    p = pl.program_id(1)
    scale = dk ** -0.5
    nq = q_ref.shape[0]
    dims = (((1,), (1,)), ((), ()))
    q = q_ref[...].astype(BF16)

    def scores(keys):
        s = [lax.dot_general(q[:, n * dk:(n + 1) * dk], keys[:, n * dk:(n + 1) * dk], dims,
                             preferred_element_type=F32) for n in range(2)]
        return jnp.concatenate(s, axis=0) * scale

    @pl.when(p == 0)
    def _():
        n_cols = mask_ref.shape[1]
        r = lax.broadcasted_iota(jnp.int32, (2 * nq, n_cols), 0)
        c = lax.broadcasted_iota(jnp.int32, (2 * nq, n_cols), 1)
        mask_ref[...] = jnp.where((r % heads) == (c % heads), 1.0, 0.0)
        r = lax.broadcasted_iota(jnp.int32, (2 * nq, nq), 0) % nq
        c = lax.broadcasted_iota(jnp.int32, (2 * nq, nq), 1)
        ok = ((r % heads) == (c % heads)) & ((c // heads) <= (r // heads))
        s = jnp.where(ok, scores(kn_ref[...].astype(BF16)), NEG)
        m = jnp.max(s, axis=-1, keepdims=True)
        e = jnp.exp(s - m)
        m_ref[...] = m
        l_ref[...] = jnp.sum(e, axis=-1, keepdims=True)
        acc_ref[...] = jnp.dot(e.astype(BF16), vn_ref[...].astype(BF16), preferred_element_type=F32)

    rows = k_refs[0].shape[0] * k_refs[0].shape[1]
    keys = jnp.concatenate([r[...].reshape(rows, 2 * dk).astype(BF16) for r in k_refs], axis=0)
    vals = jnp.concatenate([r[...].reshape(rows, v_refs[0].shape[2]).astype(BF16) for r in v_refs], axis=0)
    s = jnp.where(mask_ref[...] > 0.5, scores(keys), NEG)
    m_old = m_ref[...]
    m_new = jnp.maximum(m_old, jnp.max(s, axis=-1, keepdims=True))
    alpha = jnp.exp(m_old - m_new)
    e = jnp.exp(s - m_new)
    m_ref[...] = m_new
    l_ref[...] = alpha * l_ref[...] + jnp.sum(e, axis=-1, keepdims=True)
    acc_ref[...] = alpha * acc_ref[...] + jnp.dot(e.astype(BF16), vals, preferred_element_type=F32)

    @pl.when(p == pl.num_programs(1) - 1)
    def _():
        o = acc_ref[...] * (1.0 / l_ref[...])
        o = o[0:nq] - _lambda_value(lam_ref, lam_init) * o[nq:2 * nq]
        o_ref[...] = _subln(o, sub_ref, lam_init).astype(o_ref.dtype)


def _attn_sample(q, kn, vn, cache_k, cache_v, page_table, lam_vecs, subln, l, lam_init):
    nb, nq, _ = q.shape
    _, _, page, heads, dk2 = cache_k.shape
    dv = cache_v.shape[4]
    dk = dk2 // 2
    n_used = page_table.shape[1]
    g = _largest_divisor(n_used, (PAGES_PER_STEP, 2, 1))
    page_spec = lambda i, w: pl.BlockSpec((None, None, page, heads, w),
                                          lambda b, p, pt: (l, pt[b, p * g + i], 0, 0, 0))
    row_spec = lambda w: pl.BlockSpec((None, nq, w), lambda b, p, pt: (b, 0, 0))
    vmem = 2 * g * 2 * (_nbytes((page, SUBLANES, max(dk2, dv)), F32))
    vmem += 6 * _nbytes((2 * nq, g * page * heads), F32) + 2 * _nbytes((g * page * heads, dk2 + dv), BF16)
    return pl.pallas_call(
        functools.partial(_attn_sample_kernel, n_pages=g, heads=heads, dk=dk, lam_init=lam_init),
        grid_spec=pltpu.PrefetchScalarGridSpec(
            num_scalar_prefetch=1,
            grid=(nb, n_used // g),
            in_specs=[pl.BlockSpec((None, 4, dk), lambda b, p, pt: (l, 0, 0)),
                      pl.BlockSpec((None, 1, dv), lambda b, p, pt: (l, 0, 0)),
                      row_spec(dk2), row_spec(dk2), row_spec(dv)]
                     + [page_spec(i, dk2) for i in range(g)] + [page_spec(i, dv) for i in range(g)],
            out_specs=row_spec(dv),
            scratch_shapes=[pltpu.VMEM((2 * nq, 1), F32), pltpu.VMEM((2 * nq, 1), F32),
                            pltpu.VMEM((2 * nq, dv), F32), pltpu.VMEM((2 * nq, g * page * heads), F32)]),
        out_shape=jax.ShapeDtypeStruct((nb, nq, dv), BF16),
        compiler_params=_params(("arbitrary", "arbitrary"), vmem + 8 * 2**20),
        name="attn_sample",
    )(page_table, lam_vecs, subln, q, kn, vn, *([cache_k] * g), *([cache_v] * g))


def _split3(a):
    hi = a.astype(BF16)
    r1 = a - hi.astype(F32)
    mid = r1.astype(BF16)
    lo = (r1 - mid.astype(F32)).astype(BF16)
    return hi, mid, lo


def _exact_dot(sel, a, dims):
    out = None
    for part in _split3(a):
        d = lax.dot_general(sel, part, dims, preferred_element_type=F32)
        out = d if out is None else out + d
    return out


def _exact_dot_t(a, ones, dims):
    out = None
    for part in _split3(a):
        d = lax.dot_general(part, ones, dims, preferred_element_type=F32)
        out = d if out is None else out + d
    return out


def _rec_kernel(*refs, chunk, heads, dk, dv, is_gla, has_state):
    it = iter(refs)
    q_ref, k_ref, v_ref, r_ref = next(it), next(it), next(it), next(it)
    if is_gla:
        gl_ref, wg_ref, bg_ref, norm_ref = next(it), next(it), next(it), next(it)
    else:
        cos_ref, sin_ref, loga_ref = next(it), next(it), next(it)
    s0_ref = next(it) if has_state else None
    o_ref, s_out_ref, s_ref = next(it), next(it), next(it)
    t = pl.program_id(1)
    rows = q_ref.shape[0]
    n_chunks = rows // chunk

    @pl.when(t == 0)
    def _():
        s_ref[...] = s0_ref[...] if has_state else jnp.zeros(s_ref.shape, F32)

    q, k, v = q_ref[...], k_ref[...], v_ref[...]
    if is_gla:
        gl = gl_ref[:, 0:wg_ref.shape[0]]
        z = jnp.dot(gl.astype(BF16), wg_ref[...].astype(BF16), preferred_element_type=F32) + bg_ref[...]
        a = jax.nn.log_sigmoid(z) / GLA_TAU
        q = q * (dk ** -0.5)
    else:
        a = jnp.broadcast_to(loga_ref[...], q.shape)
        half = dk // 2
        width = q.shape[1]
        lane = lax.broadcasted_iota(jnp.int32, q.shape, 1)
        first_half = (lane % dk) < half
        cos, sin = cos_ref[...], sin_ref[...]

        def rot(x):
            other = jnp.where(first_half, pltpu.roll(x, width - half, 1), pltpu.roll(x, half, 1))
            return x * cos + other * sin

        q = rot(q)
        k = rot(k) * (dk ** -0.5)

    ri = lax.broadcasted_iota(jnp.int32, (rows, rows), 0)
    ci = lax.broadcasted_iota(jnp.int32, (rows, rows), 1)
    same_chunk = (ri // chunk) == (ci // chunk)
    causal = same_chunk & (ci <= ri)
    mm = (((1,), (0,)), ((), ()))
    b = _exact_dot(jnp.where(causal, 1.0, 0.0).astype(BF16), a, mm)
    b_last = _exact_dot(jnp.where(same_chunk, 1.0, 0.0).astype(BF16), a, mm)
    q_t = q * jnp.exp(b)
    k_t = k * jnp.exp(-b)
    k_end = k * jnp.exp(b_last - b)
    ones = jnp.ones((chunk, dv), BF16)
    tt = (((0,), (0,)), ((), ()))
    nt = (((1,), (1,)), ((), ()))
    outs = []
    for h in range(heads):
        sk = slice(h * dk, (h + 1) * dk)
        sv = slice(h * dv, (h + 1) * dv)
        qh, kh, keh, vh = q_t[:, sk].astype(BF16), k_t[:, sk].astype(BF16), k_end[:, sk].astype(BF16), v[:, sv].astype(BF16)
        attn = jnp.where(causal, lax.dot_general(qh, kh, nt, preferred_element_type=F32), 0.0)
        o = jnp.dot(attn.astype(BF16), vh, preferred_element_type=F32)
        state = s_ref[h]
        inter = []
        for c in range(n_chunks):
            rs = slice(c * chunk, (c + 1) * chunk)
            inter.append(jnp.dot(qh[rs], state.astype(BF16), preferred_element_type=F32))
            decay = jnp.exp(_exact_dot_t(a[rs, sk], ones, tt))
            state = decay * state + lax.dot_general(keh[rs], vh[rs], tt, preferred_element_type=F32)
        s_ref[h] = state
        o = o + (inter[0] if n_chunks == 1 else jnp.concatenate(inter, axis=0))
        o = o * lax.rsqrt(jnp.mean(o * o, axis=-1, keepdims=True) + EPS)
        if is_gla:
            o = o * norm_ref[...]
        gate = r_ref[:, sv]
        outs.append(o * (gate * jax.nn.sigmoid(gate)))
    o_ref[...] = jnp.concatenate(outs, axis=1).astype(o_ref.dtype)

    @pl.when(t == pl.num_programs(1) - 1)
    def _():
        s_out_ref[...] = s_ref[...]


def _recurrence(seg, nb, t, heads, dk, dv, l, *, gla=None, ret=None, state=None):
    is_gla = gla is not None
    chunk = min(CHUNK, t)
    rows = _largest_divisor(t, (4 * chunk, 2 * chunk, chunk))
    nblk = t // rows
    wq, wv = heads * dk, heads * dv
    assert wv == 2 * wq
    row_blk = lambda w, j: pl.BlockSpec((rows, w), lambda b, i: (b * nblk + i, j))
    in_specs = [row_blk(wq, 0), row_blk(wq, 1), row_blk(wv, 1), row_blk(wv, 2)]
    args = [seg, seg, seg, seg]
    if is_gla:
        gl, w_gate, b_gate, norm = gla
        rank = w_gate.shape[1]
        in_specs += [pl.BlockSpec((rows, gl.shape[1]), lambda b, i: (b * nblk + i, 0)),
                     pl.BlockSpec((None, rank, wq), lambda b, i: (l, 0, 0)),
                     pl.BlockSpec((None, 1, wq), lambda b, i: (l, 0, 0)),
                     pl.BlockSpec((None, 1, dv), lambda b, i: (l, 0, 0))]
        args += [gl, w_gate, b_gate, norm]
    else:
        cos, sin, loga = ret
        in_specs += [pl.BlockSpec((rows, wq), lambda b, i: (i, 0)),
                     pl.BlockSpec((rows, wq), lambda b, i: (i, 0)),
                     pl.BlockSpec((1, wq), lambda b, i: (0, 0))]
        args += [cos, sin, loga]
    if state is not None:
        in_specs.append(pl.BlockSpec((None, None, heads, dk, dv), lambda b, i: (l, b, 0, 0, 0)))
        args.append(state)
    vmem = 2 * (2 * _nbytes((rows, wq), F32) + 2 * _nbytes((rows, wv), F32) + _nbytes((rows, wv), BF16))
    vmem += 16 * _nbytes((rows, wq), F32) + 8 * _nbytes((rows, rows), F32)
    return pl.pallas_call(
        functools.partial(_rec_kernel, chunk=chunk, heads=heads, dk=dk, dv=dv, is_gla=is_gla,
                          has_state=state is not None),
        grid=(nb, nblk),
        in_specs=in_specs,
        out_specs=[pl.BlockSpec((rows, wv), lambda b, i: (b * nblk + i, 0)),
                   pl.BlockSpec((None, heads, dk, dv), lambda b, i: (b, 0, 0, 0))],
        out_shape=[jax.ShapeDtypeStruct((nb * t, wv), BF16),
                   jax.ShapeDtypeStruct((nb, heads, dk, dv), F32)],
        scratch_shapes=[pltpu.VMEM((heads, dk, dv), F32)],
        compiler_params=_params(("arbitrary", "arbitrary"), vmem + 8 * 2**20),
        name="gla" if is_gla else "retention",
    )(*args)


def _rope_tables(pos, heads, dk):
    half = dk // 2
    freq = 1.0 / (ROPE_BASE ** jnp.linspace(0.0, 1.0, half, dtype=F32))
    ang = pos.astype(F32)[:, None] * freq[None, :]
    cos, sin = jnp.cos(ang), jnp.sin(ang)
    cos = jnp.tile(jnp.concatenate([cos, cos], axis=-1), (1, heads))
    sin = jnp.tile(jnp.concatenate([-sin, sin], axis=-1), (1, heads))
    return cos, sin


def kernel(x_prompt, x_sample, cache_k, cache_v, state_gla, state_ret, page_table, c_prompt, c_sample,
           w_ada, b_ada, w_ffn1_up, w_ffn1_down, w_in, lambda_q1, lambda_k1, lambda_q2, lambda_k2,
           attn_subln, w_gla_gate, b_gla_gate, gla_norm, w_branch_a, w_branch_b, w_branch_c, w_out,
           w_ffn2_up, w_ffn2_down, final_norm):
    nb_p, t_p, d = x_prompt.shape
    nb_s, t_s, _ = x_sample.shape
    n_layers = w_ada.shape[0]
    a_heads, a_dk2 = cache_k.shape[3:]
    a_dk, a_dv = a_dk2 // 2, cache_v.shape[4]
    b_heads, b_dk, b_dv = state_gla.shape[2:]
    c_heads, c_dk, c_dv = state_ret.shape[2:]
    rank = w_gla_gate.shape[1]
    n_pages = page_table.shape[1]
    past_len = n_pages * cache_k.shape[2]

    w_a = a_heads * a_dk2
    w_av = a_heads * a_dv
    w_b = 2 * b_heads * b_dk + 2 * b_heads * b_dv
    w_c = 2 * c_heads * c_dk + 2 * c_heads * c_dv
    off_b = 2 * w_a + w_av
    off_gl = off_b + w_b
    off_c = off_gl + rank
    off_m = off_c + w_c
    assert off_m + 3 * d == w_in.shape[2]
    assert rank <= 128 and off_gl + 128 <= w_in.shape[2]
    w_in_t = jnp.swapaxes(w_in, 1, 2)

    n_c = nb_p + nb_s
    c_rows = jnp.concatenate([c_prompt, c_sample, jnp.zeros((-n_c % SUBLANES, d), F32)], axis=0)
    mod = _ada_all(c_rows, w_ada, b_ada)
    m_p, m_s = nb_p * t_p, nb_s * t_s
    grp_p = Group(mod[:, :nb_p].reshape(n_layers, nb_p, 1, N_MOD * d), False, m_p, t_p,
                  _largest_divisor(t_p, (1024, 512, 256, 128, 64, 32, 16)))
    grp_s = Group(jnp.repeat(mod[:, nb_p:n_c], t_s, axis=1), True, m_s, t_s, m_s)
    grps = (grp_p, grp_s)
    tms = (grp_p.tm, m_s)
    tms_narrow = (min(grp_p.tm, 512), m_s)

    lam_vecs = jnp.stack([lambda_q1, lambda_k1, lambda_q2, lambda_k2], axis=1)
    subln = attn_subln.reshape(n_layers, 1, a_dv)
    b_gate = b_gla_gate.reshape(n_layers, 1, b_heads * b_dk)
    norm_b = gla_norm.reshape(n_layers, 1, b_dv)
    log_gamma = jnp.log(1.0 - jnp.power(2.0, -5.0 - jnp.arange(c_heads, dtype=F32)))
    loga_c = jnp.repeat(log_gamma, c_dk).reshape(1, c_heads * c_dk)
    rope_p = _rope_tables(jnp.arange(t_p, dtype=jnp.int32), c_heads, c_dk)
    rope_s = _rope_tables(past_len + jnp.arange(t_s, dtype=jnp.int32), c_heads, c_dk)

    tms_down = (min(grp_p.tm, 256), m_s)

    def ffn(xs, hs, l, w_up, w_down, i_gate, nxt):
        acts = _mm_swiglu(hs, w_up, l, tms)
        return _mm_res_norm(grps, [(act,) for act in acts], w_down, l, xs, i_gate, True, nxt, tms_down)

    def layer(xs, hs, l, kv_p):
        lam_init = 0.8 - 0.6 * math.exp(-0.3 * l)
        x_p, h_p, x_s, h_s = ffn(xs, hs, l, w_ffn1_up, w_ffn1_down, 2, (l, 4, 3))
        xs, hs = [x_p, x_s], [h_p, h_s]
        proj = lambda off, n, dts=(F32, F32), act=None, stacked=None: _mm_proj(
            hs, w_in_t, l, off, n, tms, dts, act, stacked)
        q_p, q_s = proj(0, w_a, (BF16, F32))
        k_buf, k_p, k_s = proj(w_a, w_a, (BF16, F32), stacked=(kv_p[0], a_heads, t_p))
        v_buf, v_p, v_s = proj(2 * w_a, w_av, (BF16, F32), stacked=(kv_p[1], a_heads, t_p))
        seg_b = proj(off_b, w_b)
        gl = proj(off_gl, 128)
        seg_c = proj(off_c, w_c)
        gates = proj(off_m, 3 * d, (BF16, BF16), "sigmoid")

        o_a_p = _attn_prompt(q_p, k_p, v_p, lam_vecs, subln, l, nb_p, t_p, a_heads, a_dk, a_dv, lam_init)
        rows = lambda z: z.reshape(nb_s, t_s * a_heads, z.shape[1] // a_heads)
        o_a_s = _attn_sample(rows(q_s), rows(k_s), rows(v_s), cache_k, cache_v, page_table,
                             lam_vecs, subln, l, lam_init).reshape(m_s, w_av)
        o_b_p, gla_p = _recurrence(seg_b[0], nb_p, t_p, b_heads, b_dk, b_dv, l,
                                   gla=(gl[0], w_gla_gate, b_gate, norm_b))
        o_b_s, gla_s = _recurrence(seg_b[1], nb_s, t_s, b_heads, b_dk, b_dv, l,
                                   gla=(gl[1], w_gla_gate, b_gate, norm_b), state=state_gla)
        o_c_p, ret_p = _recurrence(seg_c[0], nb_p, t_p, c_heads, c_dk, c_dv, l, ret=rope_p + (loga_c,))
        o_c_s, ret_s = _recurrence(seg_c[1], nb_s, t_s, c_heads, c_dk, c_dv, l, ret=rope_s + (loga_c,),
                                   state=state_ret)

        merged = _mm_merge([(o_a_p, o_b_p, o_c_p), (o_a_s, o_b_s, o_c_s)],
                           (w_branch_a, w_branch_b, w_branch_c), l, gates, tms)
        x_p, h_p, x_s, h_s = _mm_res_norm(grps, [(merged[0],), (merged[1],)], w_out, l, xs, 5, False,
                                          (l, 7, 6), tms_narrow)
        last = l == n_layers - 1
        out = ffn([x_p, x_s], [h_p, h_s], l, w_ffn2_up, w_ffn2_down, 8, final_norm if last else (l + 1, 1, 0))
        return out, (k_buf, v_buf), (k_s, v_s, gla_p, gla_s, ret_p, ret_s)

    xs = [x_prompt.reshape(m_p, d), x_sample.reshape(m_s, d)]
    hs = [_rmsmod(g, x, 0, 0, 1) for g, x in zip(grps, xs)]
    kv_p = (jnp.zeros((n_layers, nb_p, t_p, a_heads, a_dk2), F32),
            jnp.zeros((n_layers, nb_p, t_p, a_heads, a_dv), F32))
    per_layer = []
    for l in range(n_layers):
        out, kv_p, new = layer(xs, hs, l, kv_p)
        per_layer.append(new)
        if l < n_layers - 1:
            xs, hs = [out[0], out[2]], [out[1], out[3]]
    y_p = out[0].reshape(nb_p, t_p, d)
    y_s = out[1].reshape(nb_s, t_s, d)

    def stacked(i, shape):
        return jnp.stack([new[i] for new in per_layer]).reshape((n_layers,) + shape)

    return (y_p, y_s, kv_p[0], kv_p[1],
            stacked(0, (nb_s, t_s, a_heads, a_dk2)), stacked(1, (nb_s, t_s, a_heads, a_dv)),
            stacked(2, (nb_p, b_heads, b_dk, b_dv)), stacked(3, (nb_s, b_heads, b_dk, b_dv)),
            stacked(4, (nb_p, c_heads, c_dk, c_dv)), stacked(5, (nb_s, c_heads, c_dk, c_dv)))
```

```python
import functools
import math
from typing import Callable, NamedTuple

import jax
import jax.numpy as jnp
from jax import lax
from jax.experimental import pallas as pl
from jax.experimental.pallas import tpu as pltpu

EPS = 1e-6
CHUNK = 64
GLA_TAU = 16.0
ROPE_BASE = 10000.0
N_MOD = 9
BF16 = jnp.bfloat16
F32 = jnp.float32
NEG = float(jnp.finfo(jnp.float32).min)

V7X_VMEM_BYTES = 64 * 2**20
VMEM_CEILING = V7X_VMEM_BYTES - 6 * 2**20
SUBLANES = 8
PAGES_PER_STEP = 16


def _params(sem, vmem_bytes):
    assert vmem_bytes <= V7X_VMEM_BYTES
    return pltpu.CompilerParams(dimension_semantics=sem, vmem_limit_bytes=VMEM_CEILING)


def _largest_divisor(n, cands):
    for c in cands:
        if n % c == 0:
            return c
    raise ValueError(f"no tile in {cands} divides {n}")


def _nbytes(shape, dtype):
    return math.prod(shape) * jnp.dtype(dtype).itemsize


class Group(NamedTuple):
    mod: jax.Array
    per_row: bool
    rows: int
    rows_per_batch: int
    tm: int


def _mod_spec(g, l, i, d_model, tn, mn):
    nb = d_model // tn
    if g.per_row:
        return pl.BlockSpec((None, g.tm, tn), lambda *a: (l, mn(*a)[0], i * nb + mn(*a)[1]))
    blocks_per_batch = g.rows_per_batch // g.tm
    return pl.BlockSpec((None, None, 1, tn),
                        lambda *a: (l, mn(*a)[0] // blocks_per_batch, 0, i * nb + mn(*a)[1]))


def _ada_kernel(c_ref, w_ref, b_ref, o_ref):
    c = c_ref[...]
    s = (c * jax.nn.sigmoid(c)).astype(BF16)
    o_ref[...] = jnp.dot(s, w_ref[...].astype(BF16), preferred_element_type=F32) + b_ref[...]


def _ada_all(c_rows, w_ada, b_ada):
    n_layers, d_model, n = w_ada.shape
    rows = c_rows.shape[0]
    tn = _largest_divisor(n, (1024, 512, 256, 128))
    vmem = 2 * (_nbytes((d_model, tn), F32) + _nbytes((rows, d_model), F32)) + _nbytes((d_model, tn), F32)
    return pl.pallas_call(
        _ada_kernel,
        grid=(n_layers, n // tn),
        in_specs=[pl.BlockSpec((rows, d_model), lambda l, j: (0, 0)),
                  pl.BlockSpec((None, d_model, tn), lambda l, j: (l, 0, j)),
                  pl.BlockSpec((None, 1, tn), lambda l, j: (l, 0, j))],
        out_specs=pl.BlockSpec((None, rows, tn), lambda l, j: (l, 0, j)),
        out_shape=jax.ShapeDtypeStruct((n_layers, rows, n), F32),
        compiler_params=_params(("arbitrary", "arbitrary"), vmem + 8 * 2**20),
        name="adaln",
    )(c_rows, w_ada, b_ada.reshape(n_layers, 1, n))


def _rmsmod_kernel(x_ref, sc_ref, sh_ref, o_ref):
    x = x_ref[...]
    r = lax.rsqrt(jnp.mean(x * x, axis=-1, keepdims=True) + EPS)
    o_ref[...] = ((x * r) * (1.0 + sc_ref[...]) + sh_ref[...]).astype(o_ref.dtype)


def _rmsmod(g, x, l, i_shift, i_scale):
    m, d = x.shape
    tm = min(g.tm, 512)
    gg = g._replace(tm=tm)
    mn = lambda i: (i, 0)
    return pl.pallas_call(
        _rmsmod_kernel,
        grid=(m // tm,),
        in_specs=[pl.BlockSpec((tm, d), lambda i: (i, 0)),
                  _mod_spec(gg, l, i_scale, d, d, mn),
                  _mod_spec(gg, l, i_shift, d, d, mn)],
        out_specs=pl.BlockSpec((tm, d), lambda i: (i, 0)),
        out_shape=jax.ShapeDtypeStruct((m, d), BF16),
        compiler_params=_params(("arbitrary",), 6 * _nbytes((tm, d), F32) + 8 * 2**20),
        name="rmsmod",
    )(x, g.mod, g.mod)


class Rows(NamedTuple):
    xs: tuple
    extras: tuple
    outs: tuple
    epilogue: Callable
    tm: int


class Weight(NamedTuple):
    array: jax.Array
    spec: pl.BlockSpec
    tile: tuple
    x_index: int


def _tile_spec(tm, tn, rider):
    return pl.BlockSpec((tm, tn), (lambda j, i: (0, j)) if rider else (lambda j, i: (i, j)))


def _grid_mn(rider):
    return (lambda j, i: (0, j)) if rider else (lambda j, i: (i, j))


def _block_bytes(spec, dtype):
    if spec.block_shape is None:
        return 0
    dims = [1 if d is None or isinstance(d, pl.Squeezed) else getattr(d, "block_size", d) for d in spec.block_shape]
    return _nbytes(dims, dtype)


def _cast_weight(w_ref, wb_ref, transposed):
    if transposed:
        for c in range(w_ref.shape[0] // 128):
            wb_ref[:, c * 128:(c + 1) * 128] = w_ref[c * 128:(c + 1) * 128, :].T.astype(BF16)
        return
    rows = w_ref.shape[0]
    rb = _largest_divisor(rows, (512, 256, 128, 64, 32, 16))

    def body(i, c):
        r = pl.multiple_of(i * rb, rb)
        wb_ref[pl.ds(r, rb), :] = w_ref[pl.ds(r, rb), :].astype(BF16)
        return c

    lax.fori_loop(0, rows // rb, body, 0)


def _mm_kernel(*refs, counts, x_index, transposed, epilogues):
    it = iter(refs)
    take = lambda n: [next(it) for _ in range(n)]
    ins = [(take(nx), take(ne)) for nx, ne, _ in counts]
    w_refs = take(len(x_index))
    outs = [take(no) for _, _, no in counts]
    wb_refs = take(len(x_index))
    first_row_tile = pl.program_id(1) == 0

    @pl.when(first_row_tile)
    def _():
        for w_ref, wb_ref in zip(w_refs, wb_refs):
            _cast_weight(w_ref, wb_ref, transposed)

    def run(g):
        x_refs, extra_refs = ins[g]
        accs = [jnp.dot(x_refs[xi][...], wb_ref[...], preferred_element_type=F32)
                for xi, wb_ref in zip(x_index, wb_refs)]
        epilogues[g](accs, extra_refs, outs[g])

    run(0)
    if len(counts) > 1:
        pl.when(first_row_tile)(lambda: run(1))


def _mm(name, weights, groups, n_tiles, transposed=False, alias=None):
    primary = groups[0]
    in_specs, args, out_specs, out_shapes, counts = [], [], [], [], []
    vmem = 0
    for gi, g in enumerate(groups):
        for x in g.xs:
            spec = pl.BlockSpec((g.tm, x.shape[1]), (lambda j, i: (0, 0)) if gi else (lambda j, i: (i, 0)))
            in_specs.append(spec)
            args.append(x)
            vmem += 2 * _block_bytes(spec, x.dtype)
        if gi == 0 and alias is not None:
            aliases = {len(args) + alias[0]: alias[1]}
        for a, spec in g.extras:
            in_specs.append(spec)
            args.append(a)
            vmem += 2 * _block_bytes(spec, a.dtype)
        counts.append((len(g.xs), len(g.extras), len(g.outs)))
    for w in weights:
        in_specs.append(w.spec)
        args.append(w.array)
        vmem += 2 * _nbytes(w.tile, F32) + _nbytes(w.tile, BF16)
    for g in groups:
        for shape, spec in g.outs:
            out_specs.append(spec)
            out_shapes.append(shape)
            vmem += 2 * _block_bytes(spec, shape.dtype)
    tn = weights[0].tile[1]
    vmem += (len(weights) + 2) * _nbytes((primary.tm, tn), F32)
    m = primary.xs[0].shape[0]
    return pl.pallas_call(
        functools.partial(_mm_kernel, counts=tuple(counts), x_index=tuple(w.x_index for w in weights),
                          transposed=transposed, epilogues=tuple(g.epilogue for g in groups)),
        grid=(n_tiles, m // primary.tm),
        in_specs=in_specs,
        out_specs=out_specs,
        out_shape=out_shapes,
        scratch_shapes=[pltpu.VMEM(w.tile, BF16) for w in weights],
        input_output_aliases=aliases if alias is not None else {},
        compiler_params=_params(("arbitrary", "arbitrary"), vmem + 4 * 2**20),
        name=name,
    )(*args)


def _swiglu_epilogue(tail, accs, extra_refs, out_refs):
    a, b = accs
    if tail:
        tn = b.shape[1]
        shifted = jnp.concatenate([b[:, tn - tail:], b[:, :tn - tail]], axis=1)
        b = jnp.where(pl.program_id(0) == pl.num_programs(0) - 1, shifted, b)
    out_refs[0][...] = ((a * jax.nn.sigmoid(a)) * b).astype(out_refs[0].dtype)


def _mm_swiglu(hs, w_up, l, tms):
    k = hs[0].shape[1]
    f = w_up.shape[2] // 2
    tn = min(512, f)
    nt = pl.cdiv(f, tn)
    tail = f % tn
    assert tn % 128 == 0 and tail % 128 == 0 and nt * tn <= 2 * f
    back = lambda j: jnp.where(j == nt - 1, tn - tail, 0) if tail else 0
    w_spec = lambda base, moved: pl.BlockSpec(
        (pl.Squeezed(), pl.Element(k), pl.Element(tn)),
        lambda j, i: (l, 0, pl.multiple_of(base + j * tn - (back(j) if moved else 0), 128)))
    weights = [Weight(w_up, w_spec(0, False), (k, tn), 0), Weight(w_up, w_spec(f, True), (k, tn), 0)]
    groups = [Rows((h,), (), ((jax.ShapeDtypeStruct((h.shape[0], f), BF16), _tile_spec(tm, tn, gi > 0)),),
                   functools.partial(_swiglu_epilogue, tail), tm) for gi, (h, tm) in enumerate(zip(hs, tms))]
    return _mm("mm_swiglu", weights, groups, nt)


def _res_norm_kernel(*refs, counts, row0, n_stage, n_rows, half, final):
    it = iter(refs)
    take = lambda n: [next(it) for _ in range(n)]
    ins = [(take(nx), take(ne)) for nx, ne, _ in counts]
    w_ref = next(it)
    outs = [take(no) for _, _, no in counts]
    wb_ref = next(it)
    step = pl.program_id(0)

    rc, dc = w_ref.shape
    for c in range(n_stage // n_rows):
        @pl.when((step >= c * n_rows) & (step < (c + 1) * n_rows))
        def _():
            r = pl.multiple_of((step - c * n_rows) * rc, rc)
            wb_ref[pl.ds(r, rc), c * dc:(c + 1) * dc] = w_ref[...].astype(BF16)

    def run(g):
        x_refs, extra_refs = ins[g]
        acc = None
        for x_ref, r0 in zip(x_refs, row0):
            part = jnp.dot(x_ref[...], wb_ref[r0:r0 + x_ref.shape[1], :], preferred_element_type=F32)
            acc = part if acc is None else acc + part
        res_ref, g_ref = extra_refs[:2]
        gate = 1.0 + g_ref[...]
        if half:
            gate = 0.5 * gate
        x = res_ref[...] + gate * acc
        r = lax.rsqrt(jnp.mean(x * x, axis=-1, keepdims=True) + EPS)
        if final:
            outs[g][0][...] = (x * r) * extra_refs[2][...]
        else:
            sc_ref, sh_ref = extra_refs[2:]
            outs[g][0][...] = x
            outs[g][1][...] = ((x * r) * (1.0 + sc_ref[...]) + sh_ref[...]).astype(outs[g][1].dtype)

    pl.when(step >= n_stage)(lambda: run(0))
    if len(counts) > 1:
        pl.when(step == n_stage)(lambda: run(1))


def _mm_res_norm(gs, xss, w, l, ress, i_gate, half, nxt, tms):
    k, d = w.shape[1:]
    ks = [x.shape[1] for x in xss[0]]
    row0 = tuple(sum(ks[:i]) for i in range(len(ks)))
    assert sum(ks) == k and all(r % 16 == 0 for r in row0)
    final = not isinstance(nxt, tuple)
    dc = d // 2 if d % 256 == 0 else d
    rc = next(r for r in range(k, 0, -16) if k % r == 0 and r % 16 == 0 and _nbytes((r, dc), F32) <= 3 * 2**20)
    n_rows = k // rc
    n_stage = n_rows * (d // dc)
    in_specs, args, out_specs, out_shapes, counts = [], [], [], [], []
    vmem = _nbytes((k, d), BF16) + 2 * _nbytes((rc, dc), F32)
    for gi, (g, xs, res, tm) in enumerate(zip(gs, xss, ress, tms)):
        row = (lambda s: 0) if gi else (lambda s: jnp.maximum(s - n_stage, 0))
        blk = lambda w_: pl.BlockSpec((tm, w_), lambda s, row=row: (row(s), 0))
        mod = lambda ll, i, g=g, tm=tm, row=row: _mod_spec(g._replace(tm=tm), ll, i, d, d, lambda s: (row(s), 0))
        specs = [blk(x.shape[1]) for x in xs] + [blk(d), mod(l, i_gate)]
        ops = list(xs) + [res, g.mod]
        if final:
            specs.append(pl.BlockSpec((1, d), lambda s: (0, 0)))
            ops.append(nxt.reshape(1, d))
            outs = [jax.ShapeDtypeStruct(res.shape, F32)]
        else:
            specs += [mod(nxt[0], nxt[1]), mod(nxt[0], nxt[2])]
            ops += [g.mod, g.mod]
            outs = [jax.ShapeDtypeStruct(res.shape, F32), jax.ShapeDtypeStruct(res.shape, BF16)]
        in_specs += specs
        args += ops
        out_specs += [blk(d)] * len(outs)
        out_shapes += outs
        counts.append((len(xs), len(specs) - len(xs), len(outs)))
        vmem += 2 * sum(_nbytes((tm, x.shape[1]), BF16) for x in xs) + (2 + 2 + 1 + 3) * _nbytes((tm, d), F32)
    stage = lambda s: jnp.minimum(s, n_stage - 1)
    in_specs.append(pl.BlockSpec((None, rc, dc), lambda s: (l, stage(s) % n_rows, stage(s) // n_rows)))
    args.append(w)
    return pl.pallas_call(
        functools.partial(_res_norm_kernel, counts=tuple(counts), row0=row0, n_stage=n_stage, n_rows=n_rows,
                          half=half, final=final),
        grid=(n_stage + xss[0][0].shape[0] // tms[0],),
        in_specs=in_specs,
        out_specs=out_specs,
        out_shape=out_shapes,
        scratch_shapes=[pltpu.VMEM((k, d), BF16)],
        compiler_params=_params(("arbitrary",), vmem + 4 * 2**20),
        name="mm_res_norm",
    )(*args)


def _proj_epilogue(act, accs, extra_refs, out_refs):
    acc = accs[0]
    if act == "sigmoid":
        acc = jax.nn.sigmoid(acc)
    if len(out_refs) == 2:
        out_refs[0][...] = acc.reshape(out_refs[0].shape)
        out_refs[1][...] = acc.astype(out_refs[1].dtype)
    else:
        out_refs[0][...] = acc.astype(out_refs[0].dtype)


def _mm_proj(hs, w_t, l, col_off, ncols, tms, dtypes, act=None, stacked=None):
    k = hs[0].shape[1]
    assert col_off % SUBLANES == 0
    tn = _largest_divisor(ncols, (1024, 512, 256, 128))
    w_spec = pl.BlockSpec((pl.Squeezed(), pl.Element(tn), pl.Element(k)),
                          lambda j, i: (l, pl.multiple_of(col_off + j * tn, SUBLANES), 0))
    groups = []
    for gi, (h, tm, dt) in enumerate(zip(hs, tms, dtypes)):
        extras = ()
        outs = ((jax.ShapeDtypeStruct((h.shape[0], ncols), dt), _tile_spec(tm, tn, gi > 0)),)
        if gi == 0 and stacked is not None:
            buf, heads, rows_per_batch = stacked
            assert tn == ncols and rows_per_batch % tm == 0
            bpb = rows_per_batch // tm
            extras = ((buf, pl.BlockSpec(memory_space=pl.ANY)),)
            outs = ((jax.ShapeDtypeStruct(buf.shape, buf.dtype),
                     pl.BlockSpec((None, None, tm, heads, ncols // heads),
                                  lambda j, i: (l, i // bpb, i % bpb, 0, 0))),) + outs
        groups.append(Rows((h,), extras, outs, functools.partial(_proj_epilogue, act), tm))
    return _mm("mm_proj", [Weight(w_t, w_spec, (k, tn), 0)], groups, ncols // tn, transposed=True,
               alias=(0, 0) if stacked is not None else None)


def _merge_epilogue(accs, extra_refs, out_refs):
    acc = None
    for part, g_ref in zip(accs, extra_refs):
        part = g_ref[...].astype(F32) * part
        acc = part if acc is None else acc + part
    out_refs[0][...] = acc.astype(out_refs[0].dtype)


def _mm_merge(xss, ws, l, gatess, tms):
    d = ws[0].shape[2]
    tn = _largest_divisor(d, (512, 256, 128))
    nb = d // tn
    weights = [Weight(w, pl.BlockSpec((None, w.shape[1], tn), lambda j, i: (l, 0, j)), (w.shape[1], tn), i)
               for i, w in enumerate(ws)]
    groups = []
    for gi, (xs, gates, tm) in enumerate(zip(xss, gatess, tms)):
        g_spec = lambda s, gi=gi, tm=tm: pl.BlockSpec(
            (tm, tn), (lambda j, i: (0, s * nb + j)) if gi else (lambda j, i: (i, s * nb + j)))
        groups.append(Rows(tuple(xs), tuple((gates, g_spec(s)) for s in range(len(ws))),
                           ((jax.ShapeDtypeStruct((xs[0].shape[0], d), BF16), _tile_spec(tm, tn, gi > 0)),),
                           _merge_epilogue, tm))
    return _mm("mm_merge", weights, groups, nb)


def _lambda_value(lam_ref, lam_init):
    lv = lam_ref[...]
    e1 = jnp.exp(jnp.sum(lv[0:1] * lv[1:2], axis=-1, keepdims=True))
    e2 = jnp.exp(jnp.sum(lv[2:3] * lv[3:4], axis=-1, keepdims=True))
    return e1 - e2 + lam_init


def _subln(o, sub_ref, lam_init):
    r = lax.rsqrt(jnp.mean(o * o, axis=-1, keepdims=True) + EPS)
    return ((o * r) * sub_ref[...]) * (1.0 - lam_init)


def _attn_prompt_kernel(lam_ref, sub_ref, q_ref, k_ref, v_ref, o_ref, *, tq, dk, lam_init):
    t = q_ref.shape[0]
    c = (dk ** -0.5) * math.log2(math.e)
    lam = _lambda_value(lam_ref, lam_init)
    dims = (((1,), (1,)), ((), ()))
    below = (lax.broadcasted_iota(jnp.int32, (tq, tq), 1) <= lax.broadcasted_iota(jnp.int32, (tq, tq), 0))
    for qi in range(t // tq):
        lo, hi = qi * tq, (qi + 1) * tq
        q = q_ref[lo:hi, :]
        maps = []
        for n in range(2):
            cols = slice(n * dk, (n + 1) * dk)
            s_own = jnp.where(below, lax.dot_general(q[:, cols], k_ref[lo:hi, cols], dims,
                                                     preferred_element_type=F32), NEG)
            m = jnp.max(s_own, axis=-1, keepdims=True)
            if qi:
                s_past = lax.dot_general(q[:, cols], k_ref[0:lo, cols], dims, preferred_element_type=F32)
                m = jnp.maximum(m, jnp.max(s_past, axis=-1, keepdims=True))
            e = jnp.exp2((s_own - m) * c)
            l = jnp.sum(e, axis=-1, keepdims=True)
            o = jnp.dot(e.astype(BF16), v_ref[lo:hi, :], preferred_element_type=F32)
            if qi:
                e = jnp.exp2((s_past - m) * c)
                l = l + jnp.sum(e, axis=-1, keepdims=True)
                o = o + jnp.dot(e.astype(BF16), v_ref[0:lo, :], preferred_element_type=F32)
            maps.append(o * (1.0 / l))
        o = maps[0] - lam * maps[1]
        o_ref[lo:hi, :] = _subln(o, sub_ref, lam_init).astype(o_ref.dtype)


def _attn_prompt(q, k, v, lam_vecs, subln, l, nb, t, heads, dk, dv, lam_init):
    tq = _largest_divisor(t, (256, 128, 64, 32, 16, 8))
    blk = lambda w: pl.BlockSpec((t, w), lambda b, h: (b, h))
    vmem = 2 * (2 * _nbytes((t, 2 * dk), BF16) + 2 * _nbytes((t, dv), BF16)) + 8 * _nbytes((tq, t), F32)
    return pl.pallas_call(
        functools.partial(_attn_prompt_kernel, tq=tq, dk=dk, lam_init=lam_init),
        grid=(nb, heads),
        in_specs=[pl.BlockSpec((None, 4, dk), lambda b, h: (l, 0, 0)),
                  pl.BlockSpec((None, 1, dv), lambda b, h: (l, 0, 0)),
                  blk(2 * dk), blk(2 * dk), blk(dv)],
        out_specs=blk(dv),
        out_shape=jax.ShapeDtypeStruct((nb * t, heads * dv), BF16),
        compiler_params=_params(("arbitrary", "arbitrary"), vmem + 8 * 2**20),
        name="attn_prompt",
    )(lam_vecs, subln, q, k, v)


def _attn_sample_kernel(pt_ref, lam_ref, sub_ref, q_ref, kn_ref, vn_ref, *rest,
                        n_pages, heads, dk, lam_init):
    k_refs = rest[:n_pages]
    v_refs = rest[n_pages:2 * n_pages]
    o_ref, m_ref, l_ref, acc_ref, mask_ref = rest[2 * n_pages:]
    p = pl.program_id(1)
    scale = dk ** -0.5
    nq = q_ref.shape[0]
    dims = (((1,), (1,)), ((), ()))
    q = q_ref[...].astype(BF16)

    def scores(keys):
        s = [lax.dot_general(q[:, n * dk:(n + 1) * dk], keys[:, n * dk:(n + 1) * dk], dims,
                             preferred_element_type=F32) for n in range(2)]
        return jnp.concatenate(s, axis=0) * scale

    @pl.when(p == 0)
    def _():
        n_cols = mask_ref.shape[1]
        r = lax.broadcasted_iota(jnp.int32, (2 * nq, n_cols), 0)
        c = lax.broadcasted_iota(jnp.int32, (2 * nq, n_cols), 1)
        mask_ref[...] = jnp.where((r % heads) == (c % heads), 1.0, 0.0)
        r = lax.broadcasted_iota(jnp.int32, (2 * nq, nq), 0) % nq
        c = lax.broadcasted_iota(jnp.int32, (2 * nq, nq), 1)
        ok = ((r % heads) == (c % heads)) & ((c // heads) <= (r // heads))
        s = jnp.where(ok, scores(kn_ref[...].astype(BF16)), NEG)
        m = jnp.max(s, axis=-1, keepdims=True)
        e = jnp.exp(s - m)
        m_ref[...] = m
        l_ref[...] = jnp.sum(e, axis=-1, keepdims=True)
        acc_ref[...] = jnp.dot(e.astype(BF16), vn_ref[...].astype(BF16), preferred_element_type=F32)

    rows = k_refs[0].shape[0] * k_refs[0].shape[1]
    keys = jnp.concatenate([r[...].reshape(rows, 2 * dk).astype(BF16) for r in k_refs], axis=0)
    vals = jnp.concatenate([r[...].reshape(rows, v_refs[0].shape[2]).astype(BF16) for r in v_refs], axis=0)
    s = jnp.where(mask_ref[...] > 0.5, scores(keys), NEG)
    m_old = m_ref[...]
    m_new = jnp.maximum(m_old, jnp.max(s, axis=-1, keepdims=True))
    alpha = jnp.exp(m_old - m_new)
    e = jnp.exp(s - m_new)
    m_ref[...] = m_new
    l_ref[...] = alpha * l_ref[...] + jnp.sum(e, axis=-1, keepdims=True)
    acc_ref[...] = alpha * acc_ref[...] + jnp.dot(e.astype(BF16), vals, preferred_element_type=F32)

    @pl.when(p == pl.num_programs(1) - 1)
    def _():
        o = acc_ref[...] * (1.0 / l_ref[...])
        o = o[0:nq] - _lambda_value(lam_ref, lam_init) * o[nq:2 * nq]
        o_ref[...] = _subln(o, sub_ref, lam_init).astype(o_ref.dtype)


def _attn_sample(q, kn, vn, cache_k, cache_v, page_table, lam_vecs, subln, l, lam_init):
    nb, nq, _ = q.shape
    _, _, page, heads, dk2 = cache_k.shape
    dv = cache_v.shape[4]
    dk = dk2 // 2
    n_used = page_table.shape[1]
    g = _largest_divisor(n_used, (PAGES_PER_STEP, 2, 1))
    page_spec = lambda i, w: pl.BlockSpec((None, None, page, heads, w),
                                          lambda b, p, pt: (l, pt[b, p * g + i], 0, 0, 0))
    row_spec = lambda w: pl.BlockSpec((None, nq, w), lambda b, p, pt: (b, 0, 0))
    vmem = 2 * g * _nbytes((page, heads, dk2 + dv), F32)
    vmem += 4 * _nbytes((2 * nq, g * page * heads), F32) + _nbytes((g * page * heads, dk2 + dv), BF16)
    return pl.pallas_call(
        functools.partial(_attn_sample_kernel, n_pages=g, heads=heads, dk=dk, lam_init=lam_init),
        grid_spec=pltpu.PrefetchScalarGridSpec(
            num_scalar_prefetch=1,
            grid=(nb, n_used // g),
            in_specs=[pl.BlockSpec((None, 4, dk), lambda b, p, pt: (l, 0, 0)),
                      pl.BlockSpec((None, 1, dv), lambda b, p, pt: (l, 0, 0)),
                      row_spec(dk2), row_spec(dk2), row_spec(dv)]
                     + [page_spec(i, dk2) for i in range(g)] + [page_spec(i, dv) for i in range(g)],
            out_specs=row_spec(dv),
            scratch_shapes=[pltpu.VMEM((2 * nq, 1), F32), pltpu.VMEM((2 * nq, 1), F32),
                            pltpu.VMEM((2 * nq, dv), F32), pltpu.VMEM((2 * nq, g * page * heads), F32)]),
        out_shape=jax.ShapeDtypeStruct((nb, nq, dv), BF16),
        compiler_params=_params(("arbitrary", "arbitrary"), vmem + 8 * 2**20),
        name="attn_sample",
    )(page_table, lam_vecs, subln, q, kn, vn, *([cache_k] * g), *([cache_v] * g))


def _split3(a):
    hi = a.astype(BF16)
    r1 = a - hi.astype(F32)
    mid = r1.astype(BF16)
    lo = (r1 - mid.astype(F32)).astype(BF16)
    return hi, mid, lo


def _exact_dot(sel, a, dims):
    out = None
    for part in _split3(a):
        d = lax.dot_general(sel, part, dims, preferred_element_type=F32)
        out = d if out is None else out + d
    return out


def _exact_dot_t(a, ones, dims):
    out = None
    for part in _split3(a):
        d = lax.dot_general(part, ones, dims, preferred_element_type=F32)
        out = d if out is None else out + d
    return out


def _rec_kernel(*refs, chunk, heads, dk, dv, is_gla, has_state):
    it = iter(refs)
    q_ref, k_ref, v_ref, r_ref = next(it), next(it), next(it), next(it)
    if is_gla:
        gl_ref, wg_ref, bg_ref, norm_ref = next(it), next(it), next(it), next(it)
    else:
        cos_ref, sin_ref, loga_ref = next(it), next(it), next(it)
    s0_ref = next(it) if has_state else None
    o_ref, s_out_ref, s_ref = next(it), next(it), next(it)
    t = pl.program_id(1)
    rows = q_ref.shape[0]
    n_chunks = rows // chunk

    @pl.when(t == 0)
    def _():
        s_ref[...] = s0_ref[...] if has_state else jnp.zeros(s_ref.shape, F32)

    q, k, v = q_ref[...], k_ref[...], v_ref[...]
    if is_gla:
        gl = gl_ref[:, 0:wg_ref.shape[0]]
        z = jnp.dot(gl.astype(BF16), wg_ref[...].astype(BF16), preferred_element_type=F32) + bg_ref[...]
        a = jax.nn.log_sigmoid(z) / GLA_TAU
        q = q * (dk ** -0.5)
    else:
        a = jnp.broadcast_to(loga_ref[...], q.shape)
        half = dk // 2
        width = q.shape[1]
        lane = lax.broadcasted_iota(jnp.int32, q.shape, 1)
        first_half = (lane % dk) < half
        cos, sin = cos_ref[...], sin_ref[...]

        def rot(x):
            other = jnp.where(first_half, pltpu.roll(x, width - half, 1), pltpu.roll(x, half, 1))
            return x * cos + other * sin

        q = rot(q)
        k = rot(k) * (dk ** -0.5)

    ri = lax.broadcasted_iota(jnp.int32, (rows, rows), 0)
    ci = lax.broadcasted_iota(jnp.int32, (rows, rows), 1)
    same_chunk = (ri // chunk) == (ci // chunk)
    causal = same_chunk & (ci <= ri)
    mm = (((1,), (0,)), ((), ()))
    b = _exact_dot(jnp.where(causal, 1.0, 0.0).astype(BF16), a, mm)
    b_last = _exact_dot(jnp.where(same_chunk, 1.0, 0.0).astype(BF16), a, mm)
    q_t = q * jnp.exp(b)
    k_t = k * jnp.exp(-b)
    k_end = k * jnp.exp(b_last - b)
    ones = jnp.ones((chunk, dv), BF16)
    tt = (((0,), (0,)), ((), ()))
    nt = (((1,), (1,)), ((), ()))
    outs = []
    for h in range(heads):
        sk = slice(h * dk, (h + 1) * dk)
        sv = slice(h * dv, (h + 1) * dv)
        qh, kh, keh, vh = q_t[:, sk].astype(BF16), k_t[:, sk].astype(BF16), k_end[:, sk].astype(BF16), v[:, sv].astype(BF16)
        attn = jnp.where(causal, lax.dot_general(qh, kh, nt, preferred_element_type=F32), 0.0)
        o = jnp.dot(attn.astype(BF16), vh, preferred_element_type=F32)
        state = s_ref[h]
        inter = []
        for c in range(n_chunks):
            rs = slice(c * chunk, (c + 1) * chunk)
            inter.append(jnp.dot(qh[rs], state.astype(BF16), preferred_element_type=F32))
            decay = jnp.exp(_exact_dot_t(a[rs, sk], ones, tt))
            state = decay * state + lax.dot_general(keh[rs], vh[rs], tt, preferred_element_type=F32)
        s_ref[h] = state
        o = o + (inter[0] if n_chunks == 1 else jnp.concatenate(inter, axis=0))
        o = o * lax.rsqrt(jnp.mean(o * o, axis=-1, keepdims=True) + EPS)
        if is_gla:
            o = o * norm_ref[...]
        gate = r_ref[:, sv]
        outs.append(o * (gate * jax.nn.sigmoid(gate)))
    o_ref[...] = jnp.concatenate(outs, axis=1).astype(o_ref.dtype)

    @pl.when(t == pl.num_programs(1) - 1)
    def _():
        s_out_ref[...] = s_ref[...]


def _recurrence(seg, nb, t, heads, dk, dv, l, *, gla=None, ret=None, state=None):
    is_gla = gla is not None
    chunk = min(CHUNK, t)
    rows = _largest_divisor(t, (4 * chunk, 2 * chunk, chunk))
    nblk = t // rows
    wq, wv = heads * dk, heads * dv
    assert wv == 2 * wq
    row_blk = lambda w, j: pl.BlockSpec((rows, w), lambda b, i: (b * nblk + i, j))
    in_specs = [row_blk(wq, 0), row_blk(wq, 1), row_blk(wv, 1), row_blk(wv, 2)]
    args = [seg, seg, seg, seg]
    if is_gla:
        gl, w_gate, b_gate, norm = gla
        rank = w_gate.shape[1]
        in_specs += [pl.BlockSpec((rows, gl.shape[1]), lambda b, i: (b * nblk + i, 0)),
                     pl.BlockSpec((None, rank, wq), lambda b, i: (l, 0, 0)),
                     pl.BlockSpec((None, 1, wq), lambda b, i: (l, 0, 0)),
                     pl.BlockSpec((None, 1, dv), lambda b, i: (l, 0, 0))]
        args += [gl, w_gate, b_gate, norm]
    else:
        cos, sin, loga = ret
        in_specs += [pl.BlockSpec((rows, wq), lambda b, i: (i, 0)),
                     pl.BlockSpec((rows, wq), lambda b, i: (i, 0)),
                     pl.BlockSpec((1, wq), lambda b, i: (0, 0))]
        args += [cos, sin, loga]
    if state is not None:
        in_specs.append(pl.BlockSpec((None, None, heads, dk, dv), lambda b, i: (l, b, 0, 0, 0)))
        args.append(state)
    vmem = 2 * (2 * _nbytes((rows, wq), F32) + 2 * _nbytes((rows, wv), F32) + _nbytes((rows, wv), BF16))
    vmem += 16 * _nbytes((rows, wq), F32) + 8 * _nbytes((rows, rows), F32)
    return pl.pallas_call(
        functools.partial(_rec_kernel, chunk=chunk, heads=heads, dk=dk, dv=dv, is_gla=is_gla,
                          has_state=state is not None),
        grid=(nb, nblk),
        in_specs=in_specs,
        out_specs=[pl.BlockSpec((rows, wv), lambda b, i: (b * nblk + i, 0)),
                   pl.BlockSpec((None, heads, dk, dv), lambda b, i: (b, 0, 0, 0))],
        out_shape=[jax.ShapeDtypeStruct((nb * t, wv), BF16),
                   jax.ShapeDtypeStruct((nb, heads, dk, dv), F32)],
        scratch_shapes=[pltpu.VMEM((heads, dk, dv), F32)],
        compiler_params=_params(("arbitrary", "arbitrary"), vmem + 8 * 2**20),
        name="gla" if is_gla else "retention",
    )(*args)


def _rope_tables(pos, heads, dk):
    half = dk // 2
    freq = 1.0 / (ROPE_BASE ** jnp.linspace(0.0, 1.0, half, dtype=F32))
    ang = pos.astype(F32)[:, None] * freq[None, :]
    cos, sin = jnp.cos(ang), jnp.sin(ang)
    cos = jnp.tile(jnp.concatenate([cos, cos], axis=-1), (1, heads))
    sin = jnp.tile(jnp.concatenate([-sin, sin], axis=-1), (1, heads))
    return cos, sin


def kernel(x_prompt, x_sample, cache_k, cache_v, state_gla, state_ret, page_table, c_prompt, c_sample,
           w_ada, b_ada, w_ffn1_up, w_ffn1_down, w_in, lambda_q1, lambda_k1, lambda_q2, lambda_k2,
           attn_subln, w_gla_gate, b_gla_gate, gla_norm, w_branch_a, w_branch_b, w_branch_c, w_out,
           w_ffn2_up, w_ffn2_down, final_norm):
    nb_p, t_p, d = x_prompt.shape
    nb_s, t_s, _ = x_sample.shape
    n_layers = w_ada.shape[0]
    a_heads, a_dk2 = cache_k.shape[3:]
    a_dk, a_dv = a_dk2 // 2, cache_v.shape[4]
    b_heads, b_dk, b_dv = state_gla.shape[2:]
    c_heads, c_dk, c_dv = state_ret.shape[2:]
    rank = w_gla_gate.shape[1]
    n_pages = page_table.shape[1]
    past_len = n_pages * cache_k.shape[2]

    w_a = a_heads * a_dk2
    w_av = a_heads * a_dv
    w_b = 2 * b_heads * b_dk + 2 * b_heads * b_dv
    w_c = 2 * c_heads * c_dk + 2 * c_heads * c_dv
    off_b = 2 * w_a + w_av
    off_gl = off_b + w_b
    off_c = off_gl + rank
    off_m = off_c + w_c
    assert off_m + 3 * d == w_in.shape[2]
    assert rank <= 128 and off_gl + 128 <= w_in.shape[2]
    w_in_t = jnp.swapaxes(w_in, 1, 2)

    n_c = nb_p + nb_s
    c_rows = jnp.concatenate([c_prompt, c_sample, jnp.zeros((-n_c % SUBLANES, d), F32)], axis=0)
    mod = _ada_all(c_rows, w_ada, b_ada)
    m_p, m_s = nb_p * t_p, nb_s * t_s
    grp_p = Group(mod[:, :nb_p].reshape(n_layers, nb_p, 1, N_MOD * d), False, m_p, t_p,
                  _largest_divisor(t_p, (1024, 512, 256, 128, 64, 32, 16)))
    grp_s = Group(jnp.repeat(mod[:, nb_p:n_c], t_s, axis=1), True, m_s, t_s, m_s)
    grps = (grp_p, grp_s)
    tms = (grp_p.tm, m_s)
    tms_narrow = (min(grp_p.tm, 512), m_s)

    lam_vecs = jnp.stack([lambda_q1, lambda_k1, lambda_q2, lambda_k2], axis=1)
    subln = attn_subln.reshape(n_layers, 1, a_dv)
    b_gate = b_gla_gate.reshape(n_layers, 1, b_heads * b_dk)
    norm_b = gla_norm.reshape(n_layers, 1, b_dv)
    log_gamma = jnp.log(1.0 - jnp.power(2.0, -5.0 - jnp.arange(c_heads, dtype=F32)))
    loga_c = jnp.repeat(log_gamma, c_dk).reshape(1, c_heads * c_dk)
    rope_p = _rope_tables(jnp.arange(t_p, dtype=jnp.int32), c_heads, c_dk)
    rope_s = _rope_tables(past_len + jnp.arange(t_s, dtype=jnp.int32), c_heads, c_dk)

    tms_down = (min(grp_p.tm, 256), m_s)

    def ffn(xs, hs, l, w_up, w_down, i_gate, nxt):
        acts = _mm_swiglu(hs, w_up, l, tms)
        return _mm_res_norm(grps, [(act,) for act in acts], w_down, l, xs, i_gate, True, nxt, tms_down)

    def layer(xs, hs, l, kv_p):
        lam_init = 0.8 - 0.6 * math.exp(-0.3 * l)
        x_p, h_p, x_s, h_s = ffn(xs, hs, l, w_ffn1_up, w_ffn1_down, 2, (l, 4, 3))
        xs, hs = [x_p, x_s], [h_p, h_s]
        proj = lambda off, n, dts=(F32, F32), act=None, stacked=None: _mm_proj(
            hs, w_in_t, l, off, n, tms, dts, act, stacked)
        q_p, q_s = proj(0, w_a, (BF16, F32))
        k_buf, k_p, k_s = proj(w_a, w_a, (BF16, F32), stacked=(kv_p[0], a_heads, t_p))
        v_buf, v_p, v_s = proj(2 * w_a, w_av, (BF16, F32), stacked=(kv_p[1], a_heads, t_p))
        seg_b = proj(off_b, w_b)
        gl = proj(off_gl, 128)
        seg_c = proj(off_c, w_c)
        gates = proj(off_m, 3 * d, (BF16, BF16), "sigmoid")

        o_a_p = _attn_prompt(q_p, k_p, v_p, lam_vecs, subln, l, nb_p, t_p, a_heads, a_dk, a_dv, lam_init)
        rows = lambda z: z.reshape(nb_s, t_s * a_heads, z.shape[1] // a_heads)
        o_a_s = _attn_sample(rows(q_s), rows(k_s), rows(v_s), cache_k, cache_v, page_table,
                             lam_vecs, subln, l, lam_init).reshape(m_s, w_av)
        o_b_p, gla_p = _recurrence(seg_b[0], nb_p, t_p, b_heads, b_dk, b_dv, l,
                                   gla=(gl[0], w_gla_gate, b_gate, norm_b))
        o_b_s, gla_s = _recurrence(seg_b[1], nb_s, t_s, b_heads, b_dk, b_dv, l,
                                   gla=(gl[1], w_gla_gate, b_gate, norm_b), state=state_gla)
        o_c_p, ret_p = _recurrence(seg_c[0], nb_p, t_p, c_heads, c_dk, c_dv, l, ret=rope_p + (loga_c,))
        o_c_s, ret_s = _recurrence(seg_c[1], nb_s, t_s, c_heads, c_dk, c_dv, l, ret=rope_s + (loga_c,),
                                   state=state_ret)

        merged = _mm_merge([(o_a_p, o_b_p, o_c_p), (o_a_s, o_b_s, o_c_s)],
                           (w_branch_a, w_branch_b, w_branch_c), l, gates, tms)
        x_p, h_p, x_s, h_s = _mm_res_norm(grps, [(merged[0],), (merged[1],)], w_out, l, xs, 5, False,
                                          (l, 7, 6), tms_narrow)
        last = l == n_layers - 1
        out = ffn([x_p, x_s], [h_p, h_s], l, w_ffn2_up, w_ffn2_down, 8, final_norm if last else (l + 1, 1, 0))
        return out, (k_buf, v_buf), (k_s, v_s, gla_p, gla_s, ret_p, ret_s)

    xs = [x_prompt.reshape(m_p, d), x_sample.reshape(m_s, d)]
    hs = [_rmsmod(g, x, 0, 0, 1) for g, x in zip(grps, xs)]
    kv_p = (jnp.zeros((n_layers, nb_p, t_p, a_heads, a_dk2), F32),
            jnp.zeros((n_layers, nb_p, t_p, a_heads, a_dv), F32))
    per_layer = []
    for l in range(n_layers):
        out, kv_p, new = layer(xs, hs, l, kv_p)
        per_layer.append(new)
        if l < n_layers - 1:
            xs, hs = [out[0], out[2]], [out[1], out[3]]
    y_p = out[0].reshape(nb_p, t_p, d)
    y_s = out[1].reshape(nb_s, t_s, d)

    def stacked(i, shape):
        return jnp.stack([new[i] for new in per_layer]).reshape((n_layers,) + shape)

    return (y_p, y_s, kv_p[0], kv_p[1],
            stacked(0, (nb_s, t_s, a_heads, a_dk2)), stacked(1, (nb_s, t_s, a_heads, a_dv)),
            stacked(2, (nb_p, b_heads, b_dk, b_dv)), stacked(3, (nb_s, b_heads, b_dk, b_dv)),
            stacked(4, (nb_p, c_heads, c_dk, c_dv)), stacked(5, (nb_s, c_heads, c_dk, c_dv)))
```

```python
import functools
import math
from typing import Callable, NamedTuple

import jax
import jax.numpy as jnp
from jax import lax
from jax.experimental import pallas as pl
from jax.experimental.pallas import tpu as pltpu

EPS = 1e-6
CHUNK = 64
GLA_TAU = 16.0
ROPE_BASE = 10000.0
N_MOD = 9
BF16 = jnp.bfloat16
F32 = jnp.float32
NEG = float(jnp.finfo(jnp.float32).min)

V7X_VMEM_BYTES = 64 * 2**20
VMEM_CEILING = V7X_VMEM_BYTES - 6 * 2**20
SUBLANES = 8
PAGES_PER_STEP = 16


def _params(sem, vmem_bytes):
    assert vmem_bytes <= V7X_VMEM_BYTES
    return pltpu.CompilerParams(dimension_semantics=sem, vmem_limit_bytes=VMEM_CEILING)


def _largest_divisor(n, cands):
    for c in cands:
        if n % c == 0:
            return c
    raise ValueError(f"no tile in {cands} divides {n}")


def _nbytes(shape, dtype):
    return math.prod(shape) * jnp.dtype(dtype).itemsize


class Group(NamedTuple):
    mod: jax.Array
    per_row: bool
    rows: int
    rows_per_batch: int
    tm: int


def _mod_spec(g, l, i, d_model, tn, mn):
    nb = d_model // tn
    if g.per_row:
        return pl.BlockSpec((None, g.tm, tn), lambda *a: (l, mn(*a)[0], i * nb + mn(*a)[1]))
    blocks_per_batch = g.rows_per_batch // g.tm
    return pl.BlockSpec((None, None, 1, tn),
                        lambda *a: (l, mn(*a)[0] // blocks_per_batch, 0, i * nb + mn(*a)[1]))


def _ada_kernel(c_ref, w_ref, b_ref, o_ref):
    c = c_ref[...]
    s = (c * jax.nn.sigmoid(c)).astype(BF16)
    o_ref[...] = jnp.dot(s, w_ref[...].astype(BF16), preferred_element_type=F32) + b_ref[...]


def _ada_all(c_rows, w_ada, b_ada):
    n_layers, d_model, n = w_ada.shape
    rows = c_rows.shape[0]
    tn = _largest_divisor(n, (2048, 1024, 512, 256, 128))
    vmem = 2 * (_nbytes((d_model, tn), F32) + _nbytes((rows, d_model), F32)) + _nbytes((d_model, tn), F32)
    return pl.pallas_call(
        _ada_kernel,
        grid=(n_layers, n // tn),
        in_specs=[pl.BlockSpec((rows, d_model), lambda l, j: (0, 0)),
                  pl.BlockSpec((None, d_model, tn), lambda l, j: (l, 0, j)),
                  pl.BlockSpec((None, 1, tn), lambda l, j: (l, 0, j))],
        out_specs=pl.BlockSpec((None, rows, tn), lambda l, j: (l, 0, j)),
        out_shape=jax.ShapeDtypeStruct((n_layers, rows, n), F32),
        compiler_params=_params(("arbitrary", "arbitrary"), vmem + 8 * 2**20),
        name="adaln",
    )(c_rows, w_ada, b_ada.reshape(n_layers, 1, n))


def _rmsmod_kernel(x_ref, sc_ref, sh_ref, o_ref):
    x = x_ref[...]
    r = lax.rsqrt(jnp.mean(x * x, axis=-1, keepdims=True) + EPS)
    o_ref[...] = ((x * r) * (1.0 + sc_ref[...]) + sh_ref[...]).astype(o_ref.dtype)


def _rmsmod(g, x, l, i_shift, i_scale):
    m, d = x.shape
    tm = min(g.tm, 512)
    gg = g._replace(tm=tm)
    mn = lambda i: (i, 0)
    return pl.pallas_call(
        _rmsmod_kernel,
        grid=(m // tm,),
        in_specs=[pl.BlockSpec((tm, d), lambda i: (i, 0)),
                  _mod_spec(gg, l, i_scale, d, d, mn),
                  _mod_spec(gg, l, i_shift, d, d, mn)],
        out_specs=pl.BlockSpec((tm, d), lambda i: (i, 0)),
        out_shape=jax.ShapeDtypeStruct((m, d), BF16),
        compiler_params=_params(("arbitrary",), 6 * _nbytes((tm, d), F32) + 8 * 2**20),
        name="rmsmod",
    )(x, g.mod, g.mod)


class Rows(NamedTuple):
    xs: tuple
    extras: tuple
    outs: tuple
    epilogue: Callable
    tm: int


class Weight(NamedTuple):
    array: jax.Array
    spec: pl.BlockSpec
    tile: tuple
    x_index: int


def _tile_spec(tm, tn, rider):
    return pl.BlockSpec((tm, tn), (lambda j, i: (0, j)) if rider else (lambda j, i: (i, j)))


def _block_bytes(spec, dtype):
    if spec.block_shape is None:
        return 0
    dims = [1 if d is None or isinstance(d, pl.Squeezed) else getattr(d, "block_size", d) for d in spec.block_shape]
    return _nbytes(dims, dtype)


def _cast_weight(w_ref, wb_ref, transposed):
    if transposed:
        for c in range(w_ref.shape[0] // 128):
            wb_ref[:, c * 128:(c + 1) * 128] = w_ref[c * 128:(c + 1) * 128, :].T.astype(BF16)
        return
    rows = w_ref.shape[0]
    rb = _largest_divisor(rows, (512, 256, 128, 64, 32, 16))

    def body(i, c):
        r = pl.multiple_of(i * rb, rb)
        wb_ref[pl.ds(r, rb), :] = w_ref[pl.ds(r, rb), :].astype(BF16)
        return c

    lax.fori_loop(0, rows // rb, body, 0)


def _mm_kernel(*refs, counts, x_index, transposed, epilogues):
    it = iter(refs)
    take = lambda n: [next(it) for _ in range(n)]
    ins = [(take(nx), take(ne)) for nx, ne, _ in counts]
    w_refs = take(len(x_index))
    outs = [take(no) for _, _, no in counts]
    wb_refs = take(len(x_index))
    first_row_tile = pl.program_id(1) == 0

    @pl.when(first_row_tile)
    def _():
        for w_ref, wb_ref in zip(w_refs, wb_refs):
            _cast_weight(w_ref, wb_ref, transposed)

    def run(g):
        x_refs, extra_refs = ins[g]
        accs = [jnp.dot(x_refs[xi][...], wb_ref[...], preferred_element_type=F32)
                for xi, wb_ref in zip(x_index, wb_refs)]
        epilogues[g](accs, extra_refs, outs[g])

    run(0)
    if len(counts) > 1:
        pl.when(first_row_tile)(lambda: run(1))


def _mm(name, weights, groups, n_tiles, transposed=False, alias=None):
    primary = groups[0]
    in_specs, args, out_specs, out_shapes, counts = [], [], [], [], []
    vmem = 0
    for gi, g in enumerate(groups):
        for x in g.xs:
            spec = pl.BlockSpec((g.tm, x.shape[1]), (lambda j, i: (0, 0)) if gi else (lambda j, i: (i, 0)))
            in_specs.append(spec)
            args.append(x)
            vmem += 2 * _block_bytes(spec, x.dtype)
        if gi == 0 and alias is not None:
            aliases = {len(args) + alias[0]: alias[1]}
        for a, spec in g.extras:
            in_specs.append(spec)
            args.append(a)
            vmem += 2 * _block_bytes(spec, a.dtype)
        counts.append((len(g.xs), len(g.extras), len(g.outs)))
    for w in weights:
        in_specs.append(w.spec)
        args.append(w.array)
        vmem += 2 * _nbytes(w.tile, F32) + _nbytes(w.tile, BF16)
    for g in groups:
        for shape, spec in g.outs:
            out_specs.append(spec)
            out_shapes.append(shape)
            vmem += 2 * _block_bytes(spec, shape.dtype)
    tn = weights[0].tile[1]
    vmem += (len(weights) + 2) * _nbytes((primary.tm, tn), F32)
    m = primary.xs[0].shape[0]
    return pl.pallas_call(
        functools.partial(_mm_kernel, counts=tuple(counts), x_index=tuple(w.x_index for w in weights),
                          transposed=transposed, epilogues=tuple(g.epilogue for g in groups)),
        grid=(n_tiles, m // primary.tm),
        in_specs=in_specs,
        out_specs=out_specs,
        out_shape=out_shapes,
        scratch_shapes=[pltpu.VMEM(w.tile, BF16) for w in weights],
        input_output_aliases=aliases if alias is not None else {},
        compiler_params=_params(("arbitrary", "arbitrary"), vmem + 4 * 2**20),
        name=name,
    )(*args)


def _swiglu_epilogue(tail, accs, extra_refs, out_refs):
    a, b = accs
    if tail:
        tn = b.shape[1]
        shifted = jnp.concatenate([b[:, tn - tail:], b[:, :tn - tail]], axis=1)
        b = jnp.where(pl.program_id(0) == pl.num_programs(0) - 1, shifted, b)
    out_refs[0][...] = ((a * jax.nn.sigmoid(a)) * b).astype(out_refs[0].dtype)


def _mm_swiglu(hs, w_up, l, tms):
    k = hs[0].shape[1]
    f = w_up.shape[2] // 2
    tn = min(512, f)
    nt = pl.cdiv(f, tn)
    tail = f % tn
    assert tn % 128 == 0 and tail % 128 == 0 and nt * tn <= 2 * f
    back = lambda j: jnp.where(j == nt - 1, tn - tail, 0) if tail else 0
    w_spec = lambda base, moved: pl.BlockSpec(
        (pl.Squeezed(), pl.Element(k), pl.Element(tn)),
        lambda j, i: (l, 0, pl.multiple_of(base + j * tn - (back(j) if moved else 0), 128)))
    weights = [Weight(w_up, w_spec(0, False), (k, tn), 0), Weight(w_up, w_spec(f, True), (k, tn), 0)]
    groups = [Rows((h,), (), ((jax.ShapeDtypeStruct((h.shape[0], f), BF16), _tile_spec(tm, tn, gi > 0)),),
                   functools.partial(_swiglu_epilogue, tail), tm) for gi, (h, tm) in enumerate(zip(hs, tms))]
    return _mm("mm_swiglu", weights, groups, nt)


def _res_norm_kernel(*refs, counts, row0, n_stage, n_rows, half, final):
    it = iter(refs)
    take = lambda n: [next(it) for _ in range(n)]
    ins = [(take(nx), take(ne)) for nx, ne, _ in counts]
    w_ref = next(it)
    outs = [take(no) for _, _, no in counts]
    wb_ref = next(it)
    step = pl.program_id(0)

    rc, dc = w_ref.shape
    for c in range(n_stage // n_rows):
        @pl.when((step >= c * n_rows) & (step < (c + 1) * n_rows))
        def _():
            r = pl.multiple_of((step - c * n_rows) * rc, rc)
            wb_ref[pl.ds(r, rc), c * dc:(c + 1) * dc] = w_ref[...].astype(BF16)

    def run(g):
        x_refs, extra_refs = ins[g]
        acc = None
        for x_ref, r0 in zip(x_refs, row0):
            part = jnp.dot(x_ref[...], wb_ref[r0:r0 + x_ref.shape[1], :], preferred_element_type=F32)
            acc = part if acc is None else acc + part
        res_ref, g_ref = extra_refs[:2]
        gate = 1.0 + g_ref[...]
        if half:
            gate = 0.5 * gate
        x = res_ref[...] + gate * acc
        r = lax.rsqrt(jnp.mean(x * x, axis=-1, keepdims=True) + EPS)
        if final:
            outs[g][0][...] = (x * r) * extra_refs[2][...]
        else:
            sc_ref, sh_ref = extra_refs[2:]
            outs[g][0][...] = x
            outs[g][1][...] = ((x * r) * (1.0 + sc_ref[...]) + sh_ref[...]).astype(outs[g][1].dtype)

    pl.when(step >= n_stage)(lambda: run(0))
    if len(counts) > 1:
        pl.when(step == n_stage)(lambda: run(1))


def _mm_res_norm(gs, xss, w, l, ress, i_gate, half, nxt, tms):
    k, d = w.shape[1:]
    ks = [x.shape[1] for x in xss[0]]
    row0 = tuple(sum(ks[:i]) for i in range(len(ks)))
    assert sum(ks) == k and all(r % 16 == 0 for r in row0)
    final = not isinstance(nxt, tuple)
    dc = d // 2 if d % 256 == 0 else d
    rc = next(r for r in range(k, 0, -16) if k % r == 0 and r % 16 == 0 and _nbytes((r, dc), F32) <= 3 * 2**20)
    n_rows = k // rc
    n_stage = n_rows * (d // dc)
    in_specs, args, out_specs, out_shapes, counts = [], [], [], [], []
    vmem = _nbytes((k, d), BF16) + 2 * _nbytes((rc, dc), F32)
    for gi, (g, xs, res, tm) in enumerate(zip(gs, xss, ress, tms)):
        row = (lambda s: 0) if gi else (lambda s: jnp.maximum(s - n_stage, 0))
        blk = lambda w_: pl.BlockSpec((tm, w_), lambda s, row=row: (row(s), 0))
        mod = lambda ll, i, g=g, tm=tm, row=row: _mod_spec(g._replace(tm=tm), ll, i, d, d, lambda s: (row(s), 0))
        specs = [blk(x.shape[1]) for x in xs] + [blk(d), mod(l, i_gate)]
        ops = list(xs) + [res, g.mod]
        if final:
            specs.append(pl.BlockSpec((1, d), lambda s: (0, 0)))
            ops.append(nxt.reshape(1, d))
            outs = [jax.ShapeDtypeStruct(res.shape, F32)]
        else:
            specs += [mod(nxt[0], nxt[1]), mod(nxt[0], nxt[2])]
            ops += [g.mod, g.mod]
            outs = [jax.ShapeDtypeStruct(res.shape, F32), jax.ShapeDtypeStruct(res.shape, BF16)]
        in_specs += specs
        args += ops
        out_specs += [blk(d)] * len(outs)
        out_shapes += outs
        counts.append((len(xs), len(specs) - len(xs), len(outs)))
        vmem += 2 * sum(_nbytes((tm, x.shape[1]), BF16) for x in xs) + (2 + 2 + 1 + 3) * _nbytes((tm, d), F32)
    stage = lambda s: jnp.minimum(s, n_stage - 1)
    in_specs.append(pl.BlockSpec((None, rc, dc), lambda s: (l, stage(s) % n_rows, stage(s) // n_rows)))
    args.append(w)
    return pl.pallas_call(
        functools.partial(_res_norm_kernel, counts=tuple(counts), row0=row0, n_stage=n_stage, n_rows=n_rows,
                          half=half, final=final),
        grid=(n_stage + xss[0][0].shape[0] // tms[0],),
        in_specs=in_specs,
        out_specs=out_specs,
        out_shape=out_shapes,
        scratch_shapes=[pltpu.VMEM((k, d), BF16)],
        compiler_params=_params(("arbitrary",), vmem + 4 * 2**20),
        name="mm_res_norm",
    )(*args)


def _proj_epilogue(act, accs, extra_refs, out_refs):
    acc = accs[0]
    if act == "sigmoid":
        acc = jax.nn.sigmoid(acc)
    if len(out_refs) == 2:
        out_refs[0][...] = acc.reshape(out_refs[0].shape)
        out_refs[1][...] = acc.astype(out_refs[1].dtype)
    else:
        out_refs[0][...] = acc.astype(out_refs[0].dtype)


def _mm_proj(hs, w_t, l, col_off, ncols, tms, dtypes, act=None, stacked=None):
    k = hs[0].shape[1]
    assert col_off % SUBLANES == 0
    tn = _largest_divisor(ncols, (1024, 768, 512, 256, 128))
    w_spec = pl.BlockSpec((pl.Squeezed(), pl.Element(tn), pl.Element(k)),
                          lambda j, i: (l, pl.multiple_of(col_off + j * tn, SUBLANES), 0))
    groups = []
    for gi, (h, tm, dt) in enumerate(zip(hs, tms, dtypes)):
        extras = ()
        outs = ((jax.ShapeDtypeStruct((h.shape[0], ncols), dt), _tile_spec(tm, tn, gi > 0)),)
        if gi == 0 and stacked is not None:
            buf, heads, rows_per_batch = stacked
            assert tn == ncols and rows_per_batch % tm == 0
            bpb = rows_per_batch // tm
            extras = ((buf, pl.BlockSpec(memory_space=pl.ANY)),)
            outs = ((jax.ShapeDtypeStruct(buf.shape, buf.dtype),
                     pl.BlockSpec((None, None, tm, heads, ncols // heads),
                                  lambda j, i: (l, i // bpb, i % bpb, 0, 0))),) + outs
        groups.append(Rows((h,), extras, outs, functools.partial(_proj_epilogue, act), tm))
    return _mm("mm_proj", [Weight(w_t, w_spec, (k, tn), 0)], groups, ncols // tn, transposed=True,
               alias=(0, 0) if stacked is not None else None)


def _merge_epilogue(accs, extra_refs, out_refs):
    acc = None
    for part, g_ref in zip(accs, extra_refs):
        part = g_ref[...].astype(F32) * part
        acc = part if acc is None else acc + part
    out_refs[0][...] = acc.astype(out_refs[0].dtype)


def _mm_merge(xss, ws, l, gatess, tms):
    d = ws[0].shape[2]
    tn = _largest_divisor(d, (512, 256, 128))
    nb = d // tn
    weights = [Weight(w, pl.BlockSpec((None, w.shape[1], tn), lambda j, i: (l, 0, j)), (w.shape[1], tn), i)
               for i, w in enumerate(ws)]
    groups = []
    for gi, (xs, gates, tm) in enumerate(zip(xss, gatess, tms)):
        g_spec = lambda s, gi=gi, tm=tm: pl.BlockSpec(
            (tm, tn), (lambda j, i: (0, s * nb + j)) if gi else (lambda j, i: (i, s * nb + j)))
        groups.append(Rows(tuple(xs), tuple((gates, g_spec(s)) for s in range(len(ws))),
                           ((jax.ShapeDtypeStruct((xs[0].shape[0], d), BF16), _tile_spec(tm, tn, gi > 0)),),
                           _merge_epilogue, tm))
    return _mm("mm_merge", weights, groups, nb)


def _lambda_value(lam_ref, lam_init):
    lv = lam_ref[...]
    e1 = jnp.exp(jnp.sum(lv[0:1] * lv[1:2], axis=-1, keepdims=True))
    e2 = jnp.exp(jnp.sum(lv[2:3] * lv[3:4], axis=-1, keepdims=True))
    return e1 - e2 + lam_init


def _subln(o, sub_ref, lam_init):
    r = lax.rsqrt(jnp.mean(o * o, axis=-1, keepdims=True) + EPS)
    return ((o * r) * sub_ref[...]) * (1.0 - lam_init)


def _attn_prompt_kernel(lam_ref, sub_ref, q_ref, k_ref, v_ref, o_ref, *, tq, dk, lam_init):
    t = q_ref.shape[0]
    c = (dk ** -0.5) * math.log2(math.e)
    lam = _lambda_value(lam_ref, lam_init)
    dims = (((1,), (1,)), ((), ()))
    below = (lax.broadcasted_iota(jnp.int32, (tq, tq), 1) <= lax.broadcasted_iota(jnp.int32, (tq, tq), 0))
    for qi in range(t // tq):
        lo, hi = qi * tq, (qi + 1) * tq
        q = q_ref[lo:hi, :]
        maps = []
        for n in range(2):
            cols = slice(n * dk, (n + 1) * dk)
            s_own = jnp.where(below, lax.dot_general(q[:, cols], k_ref[lo:hi, cols], dims,
                                                     preferred_element_type=F32), NEG)
            m = jnp.max(s_own, axis=-1, keepdims=True)
            if qi:
                s_past = lax.dot_general(q[:, cols], k_ref[0:lo, cols], dims, preferred_element_type=F32)
                m = jnp.maximum(m, jnp.max(s_past, axis=-1, keepdims=True))
            e = jnp.exp2((s_own - m) * c)
            l = jnp.sum(e, axis=-1, keepdims=True)
            o = jnp.dot(e.astype(BF16), v_ref[lo:hi, :], preferred_element_type=F32)
            if qi:
                e = jnp.exp2((s_past - m) * c)
                l = l + jnp.sum(e, axis=-1, keepdims=True)
                o = o + jnp.dot(e.astype(BF16), v_ref[0:lo, :], preferred_element_type=F32)
            maps.append(o * (1.0 / l))
        o = maps[0] - lam * maps[1]
        o_ref[lo:hi, :] = _subln(o, sub_ref, lam_init).astype(o_ref.dtype)


def _attn_prompt(q, k, v, lam_vecs, subln, l, nb, t, heads, dk, dv, lam_init):
    tq = _largest_divisor(t, (256, 128, 64, 32, 16, 8))
    blk = lambda w: pl.BlockSpec((t, w), lambda b, h: (b, h))
    vmem = 2 * (2 * _nbytes((t, 2 * dk), BF16) + 2 * _nbytes((t, dv), BF16)) + 8 * _nbytes((tq, t), F32)
    return pl.pallas_call(
        functools.partial(_attn_prompt_kernel, tq=tq, dk=dk, lam_init=lam_init),
        grid=(nb, heads),
        in_specs=[pl.BlockSpec((None, 4, dk), lambda b, h: (l, 0, 0)),
                  pl.BlockSpec((None, 1, dv), lambda b, h: (l, 0, 0)),
                  blk(2 * dk), blk(2 * dk), blk(dv)],
        out_specs=blk(dv),
        out_shape=jax.ShapeDtypeStruct((nb * t, heads * dv), BF16),
        compiler_params=_params(("arbitrary", "arbitrary"), vmem + 8 * 2**20),
        name="attn_prompt",
    )(lam_vecs, subln, q, k, v)


def _attn_sample_kernel(pt_ref, lam_ref, sub_ref, q_ref, kn_ref, vn_ref, *rest,
                        n_pages, heads, dk, lam_init):
    k_refs = rest[:n_pages]
    v_refs = rest[n_pages:2 * n_pages]
    o_ref, m_ref, l_ref, acc_ref, mask_ref = rest[2 * n_pages:]
    p = pl.program_id(1)
    scale = dk ** -0.5
    nq = q_ref.shape[0]
    dims = (((1,), (1,)), ((), ()))
    q = q_ref[...].astype(BF16)

    def scores(keys):
        s = [lax.dot_general(q[:, n * dk:(n + 1) * dk], keys[:, n * dk:(n + 1) * dk], dims,
                             preferred_element_type=F32) for n in range(2)]
        return jnp.concatenate(s, axis=0) * scale

    @pl.when(p == 0)
    def _():
        n_cols = mask_ref.shape[1]
        r = lax.broadcasted_iota(jnp.int32, (2 * nq, n_cols), 0)
        c = lax.broadcasted_iota(jnp.int32, (2 * nq, n_cols), 1)
        mask_ref[...] = jnp.where((r % heads) == (c % heads), 1.0, 0.0)
        r = lax.broadcasted_iota(jnp.int32, (2 * nq, nq), 0) % nq
        c = lax.broadcasted_iota(jnp.int32, (2 * nq, nq), 1)
        ok = ((r % heads) == (c % heads)) & ((c // heads) <= (r // heads))
        s = jnp.where(ok, scores(kn_ref[...].astype(BF16)), NEG)
        m = jnp.max(s, axis=-1, keepdims=True)
        e = jnp.exp(s - m)
        m_ref[...] = m
        l_ref[...] = jnp.sum(e, axis=-1, keepdims=True)
        acc_ref[...] = jnp.dot(e.astype(BF16), vn_ref[...].astype(BF16), preferred_element_type=F32)

    rows = k_refs[0].shape[0] * k_refs[0].shape[1]
    keys = jnp.concatenate([r[...].reshape(rows, 2 * dk).astype(BF16) for r in k_refs], axis=0)
    vals = jnp.concatenate([r[...].reshape(rows, v_refs[0].shape[2]).astype(BF16) for r in v_refs], axis=0)
    s = jnp.where(mask_ref[...] > 0.5, scores(keys), NEG)
    m_old = m_ref[...]
    m_new = jnp.maximum(m_old, jnp.max(s, axis=-1, keepdims=True))
    alpha = jnp.exp(m_old - m_new)
    e = jnp.exp(s - m_new)
    m_ref[...] = m_new
    l_ref[...] = alpha * l_ref[...] + jnp.sum(e, axis=-1, keepdims=True)
    acc_ref[...] = alpha * acc_ref[...] + jnp.dot(e.astype(BF16), vals, preferred_element_type=F32)

    @pl.when(p == pl.num_programs(1) - 1)
    def _():
        o = acc_ref[...] * (1.0 / l_ref[...])
        o = o[0:nq] - _lambda_value(lam_ref, lam_init) * o[nq:2 * nq]
        o_ref[...] = _subln(o, sub_ref, lam_init).astype(o_ref.dtype)


def _attn_sample(q, kn, vn, cache_k, cache_v, page_table, lam_vecs, subln, l, lam_init):
    nb, nq, _ = q.shape
    _, _, page, heads, dk2 = cache_k.shape
    dv = cache_v.shape[4]
    dk = dk2 // 2
    n_used = page_table.shape[1]
    g = _largest_divisor(n_used, (PAGES_PER_STEP, 2, 1))
    page_spec = lambda i, w: pl.BlockSpec((None, None, page, heads, w),
                                          lambda b, p, pt: (l, pt[b, p * g + i], 0, 0, 0))
    row_spec = lambda w: pl.BlockSpec((None, nq, w), lambda b, p, pt: (b, 0, 0))
    vmem = 2 * g * _nbytes((page, heads, dk2 + dv), F32)
    vmem += 4 * _nbytes((2 * nq, g * page * heads), F32) + _nbytes((g * page * heads, dk2 + dv), BF16)
    return pl.pallas_call(
        functools.partial(_attn_sample_kernel, n_pages=g, heads=heads, dk=dk, lam_init=lam_init),
        grid_spec=pltpu.PrefetchScalarGridSpec(
            num_scalar_prefetch=1,
            grid=(nb, n_used // g),
            in_specs=[pl.BlockSpec((None, 4, dk), lambda b, p, pt: (l, 0, 0)),
                      pl.BlockSpec((None, 1, dv), lambda b, p, pt: (l, 0, 0)),
                      row_spec(dk2), row_spec(dk2), row_spec(dv)]
                     + [page_spec(i, dk2) for i in range(g)] + [page_spec(i, dv) for i in range(g)],
            out_specs=row_spec(dv),
            scratch_shapes=[pltpu.VMEM((2 * nq, 1), F32), pltpu.VMEM((2 * nq, 1), F32),
                            pltpu.VMEM((2 * nq, dv), F32), pltpu.VMEM((2 * nq, g * page * heads), F32)]),
        out_shape=jax.ShapeDtypeStruct((nb, nq, dv), BF16),
        compiler_params=_params(("arbitrary", "arbitrary"), vmem + 8 * 2**20),
        name="attn_sample",
    )(page_table, lam_vecs, subln, q, kn, vn, *([cache_k] * g), *([cache_v] * g))


def _split3(a):
    hi = a.astype(BF16)
    r1 = a - hi.astype(F32)
    mid = r1.astype(BF16)
    lo = (r1 - mid.astype(F32)).astype(BF16)
    return hi, mid, lo


def _exact_dot(sel, a, dims):
    out = None
    for part in _split3(a):
        d = lax.dot_general(sel, part, dims, preferred_element_type=F32)
        out = d if out is None else out + d
    return out


def _exact_dot_t(a, ones, dims):
    out = None
    for part in _split3(a):
        d = lax.dot_general(part, ones, dims, preferred_element_type=F32)
        out = d if out is None else out + d
    return out


def _rec_kernel(*refs, chunk, heads, dk, dv, is_gla, has_state):
    it = iter(refs)
    q_ref, k_ref, v_ref, r_ref = next(it), next(it), next(it), next(it)
    if is_gla:
        gl_ref, wg_ref, bg_ref, norm_ref = next(it), next(it), next(it), next(it)
    else:
        cos_ref, sin_ref, loga_ref = next(it), next(it), next(it)
    s0_ref = next(it) if has_state else None
    o_ref, s_out_ref, s_ref = next(it), next(it), next(it)
    t = pl.program_id(1)
    rows = q_ref.shape[0]
    n_chunks = rows // chunk

    @pl.when(t == 0)
    def _():
        s_ref[...] = s0_ref[...] if has_state else jnp.zeros(s_ref.shape, F32)

    q, k, v = q_ref[...], k_ref[...], v_ref[...]
    if is_gla:
        gl = gl_ref[:, 0:wg_ref.shape[0]]
        z = jnp.dot(gl.astype(BF16), wg_ref[...].astype(BF16), preferred_element_type=F32) + bg_ref[...]
        a = jax.nn.log_sigmoid(z) / GLA_TAU
        q = q * (dk ** -0.5)
    else:
        a = jnp.broadcast_to(loga_ref[...], q.shape)
        half = dk // 2
        width = q.shape[1]
        lane = lax.broadcasted_iota(jnp.int32, q.shape, 1)
        first_half = (lane % dk) < half
        cos, sin = cos_ref[...], sin_ref[...]

        def rot(x):
            other = jnp.where(first_half, pltpu.roll(x, width - half, 1), pltpu.roll(x, half, 1))
            return x * cos + other * sin

        q = rot(q)
        k = rot(k) * (dk ** -0.5)

    ri = lax.broadcasted_iota(jnp.int32, (rows, rows), 0)
    ci = lax.broadcasted_iota(jnp.int32, (rows, rows), 1)
    same_chunk = (ri // chunk) == (ci // chunk)
    causal = same_chunk & (ci <= ri)
    mm = (((1,), (0,)), ((), ()))
    b = _exact_dot(jnp.where(causal, 1.0, 0.0).astype(BF16), a, mm)
    b_last = _exact_dot(jnp.where(same_chunk, 1.0, 0.0).astype(BF16), a, mm)
    q_t = q * jnp.exp(b)
    k_t = k * jnp.exp(-b)
    k_end = k * jnp.exp(b_last - b)
    ones = jnp.ones((chunk, dv), BF16)
    tt = (((0,), (0,)), ((), ()))
    nt = (((1,), (1,)), ((), ()))
    outs = []
    for h in range(heads):
        sk = slice(h * dk, (h + 1) * dk)
        sv = slice(h * dv, (h + 1) * dv)
        qh, kh, keh, vh = q_t[:, sk].astype(BF16), k_t[:, sk].astype(BF16), k_end[:, sk].astype(BF16), v[:, sv].astype(BF16)
        attn = jnp.where(causal, lax.dot_general(qh, kh, nt, preferred_element_type=F32), 0.0)
        o = jnp.dot(attn.astype(BF16), vh, preferred_element_type=F32)
        state = s_ref[h]
        inter = []
        for c in range(n_chunks):
            rs = slice(c * chunk, (c + 1) * chunk)
            inter.append(jnp.dot(qh[rs], state.astype(BF16), preferred_element_type=F32))
            decay = jnp.exp(_exact_dot_t(a[rs, sk], ones, tt))
            state = decay * state + lax.dot_general(keh[rs], vh[rs], tt, preferred_element_type=F32)
        s_ref[h] = state
        o = o + (inter[0] if n_chunks == 1 else jnp.concatenate(inter, axis=0))
        o = o * lax.rsqrt(jnp.mean(o * o, axis=-1, keepdims=True) + EPS)
        if is_gla:
            o = o * norm_ref[...]
        gate = r_ref[:, sv]
        outs.append(o * (gate * jax.nn.sigmoid(gate)))
    o_ref[...] = jnp.concatenate(outs, axis=1).astype(o_ref.dtype)

    @pl.when(t == pl.num_programs(1) - 1)
    def _():
        s_out_ref[...] = s_ref[...]


def _recurrence(seg, nb, t, heads, dk, dv, l, *, gla=None, ret=None, state=None):
    is_gla = gla is not None
    chunk = min(CHUNK, t)
    rows = _largest_divisor(t, (4 * chunk, 2 * chunk, chunk))
    nblk = t // rows
    wq, wv = heads * dk, heads * dv
    assert wv == 2 * wq
    row_blk = lambda w, j: pl.BlockSpec((rows, w), lambda b, i: (b * nblk + i, j))
    in_specs = [row_blk(wq, 0), row_blk(wq, 1), row_blk(wv, 1), row_blk(wv, 2)]
    args = [seg, seg, seg, seg]
    if is_gla:
        gl, w_gate, b_gate, norm = gla
        rank = w_gate.shape[1]
        in_specs += [pl.BlockSpec((rows, gl.shape[1]), lambda b, i: (b * nblk + i, 0)),
                     pl.BlockSpec((None, rank, wq), lambda b, i: (l, 0, 0)),
                     pl.BlockSpec((None, 1, wq), lambda b, i: (l, 0, 0)),
                     pl.BlockSpec((None, 1, dv), lambda b, i: (l, 0, 0))]
        args += [gl, w_gate, b_gate, norm]
    else:
        cos, sin, loga = ret
        in_specs += [pl.BlockSpec((rows, wq), lambda b, i: (i, 0)),
                     pl.BlockSpec((rows, wq), lambda b, i: (i, 0)),
                     pl.BlockSpec((1, wq), lambda b, i: (0, 0))]
        args += [cos, sin, loga]
    if state is not None:
        in_specs.append(pl.BlockSpec((None, None, heads, dk, dv), lambda b, i: (l, b, 0, 0, 0)))
        args.append(state)
    vmem = 2 * (2 * _nbytes((rows, wq), F32) + 2 * _nbytes((rows, wv), F32) + _nbytes((rows, wv), BF16))
    vmem += 16 * _nbytes((rows, wq), F32) + 8 * _nbytes((rows, rows), F32)
    return pl.pallas_call(
        functools.partial(_rec_kernel, chunk=chunk, heads=heads, dk=dk, dv=dv, is_gla=is_gla,
                          has_state=state is not None),
        grid=(nb, nblk),
        in_specs=in_specs,
        out_specs=[pl.BlockSpec((rows, wv), lambda b, i: (b * nblk + i, 0)),
                   pl.BlockSpec((None, heads, dk, dv), lambda b, i: (b, 0, 0, 0))],
        out_shape=[jax.ShapeDtypeStruct((nb * t, wv), BF16),
                   jax.ShapeDtypeStruct((nb, heads, dk, dv), F32)],
        scratch_shapes=[pltpu.VMEM((heads, dk, dv), F32)],
        compiler_params=_params(("arbitrary", "arbitrary"), vmem + 8 * 2**20),
        name="gla" if is_gla else "retention",
    )(*args)


def _rope_tables(pos, heads, dk):
    half = dk // 2
    freq = 1.0 / (ROPE_BASE ** jnp.linspace(0.0, 1.0, half, dtype=F32))
    ang = pos.astype(F32)[:, None] * freq[None, :]
    cos, sin = jnp.cos(ang), jnp.sin(ang)
    cos = jnp.tile(jnp.concatenate([cos, cos], axis=-1), (1, heads))
    sin = jnp.tile(jnp.concatenate([-sin, sin], axis=-1), (1, heads))
    return cos, sin


def kernel(x_prompt, x_sample, cache_k, cache_v, state_gla, state_ret, page_table, c_prompt, c_sample,
           w_ada, b_ada, w_ffn1_up, w_ffn1_down, w_in, lambda_q1, lambda_k1, lambda_q2, lambda_k2,
           attn_subln, w_gla_gate, b_gla_gate, gla_norm, w_branch_a, w_branch_b, w_branch_c, w_out,
           w_ffn2_up, w_ffn2_down, final_norm):
    nb_p, t_p, d = x_prompt.shape
    nb_s, t_s, _ = x_sample.shape
    n_layers = w_ada.shape[0]
    a_heads, a_dk2 = cache_k.shape[3:]
    a_dk, a_dv = a_dk2 // 2, cache_v.shape[4]
    b_heads, b_dk, b_dv = state_gla.shape[2:]
    c_heads, c_dk, c_dv = state_ret.shape[2:]
    rank = w_gla_gate.shape[1]
    n_pages = page_table.shape[1]
    past_len = n_pages * cache_k.shape[2]

    w_a = a_heads * a_dk2
    w_av = a_heads * a_dv
    w_b = 2 * b_heads * b_dk + 2 * b_heads * b_dv
    w_c = 2 * c_heads * c_dk + 2 * c_heads * c_dv
    off_b = 2 * w_a + w_av
    off_gl = off_b + w_b
    off_c = off_gl + rank
    off_m = off_c + w_c
    assert off_m + 3 * d == w_in.shape[2]
    assert rank <= 128 and off_gl + 128 <= w_in.shape[2]
    w_in_t = jnp.swapaxes(w_in, 1, 2)

    n_c = nb_p + nb_s
    c_rows = jnp.concatenate([c_prompt, c_sample, jnp.zeros((-n_c % SUBLANES, d), F32)], axis=0)
    mod = _ada_all(c_rows, w_ada, b_ada)
    m_p, m_s = nb_p * t_p, nb_s * t_s
    grp_p = Group(mod[:, :nb_p].reshape(n_layers, nb_p, 1, N_MOD * d), False, m_p, t_p,
                  _largest_divisor(t_p, (1024, 512, 256, 128, 64, 32, 16)))
    grp_s = Group(jnp.repeat(mod[:, nb_p:n_c], t_s, axis=1), True, m_s, t_s, m_s)
    grps = (grp_p, grp_s)
    tms = (grp_p.tm, m_s)
    tms_narrow = (min(grp_p.tm, 512), m_s)

    lam_vecs = jnp.stack([lambda_q1, lambda_k1, lambda_q2, lambda_k2], axis=1)
    subln = attn_subln.reshape(n_layers, 1, a_dv)
    b_gate = b_gla_gate.reshape(n_layers, 1, b_heads * b_dk)
    norm_b = gla_norm.reshape(n_layers, 1, b_dv)
    log_gamma = jnp.log(1.0 - jnp.power(2.0, -5.0 - jnp.arange(c_heads, dtype=F32)))
    loga_c = jnp.repeat(log_gamma, c_dk).reshape(1, c_heads * c_dk)
    rope_p = _rope_tables(jnp.arange(t_p, dtype=jnp.int32), c_heads, c_dk)
    rope_s = _rope_tables(past_len + jnp.arange(t_s, dtype=jnp.int32), c_heads, c_dk)

    tms_down = (min(grp_p.tm, 256), m_s)

    def ffn(xs, hs, l, w_up, w_down, i_gate, nxt):
        acts = _mm_swiglu(hs, w_up, l, tms)
        return _mm_res_norm(grps, [(act,) for act in acts], w_down, l, xs, i_gate, True, nxt, tms_down)

    def layer(xs, hs, l, kv_p):
        lam_init = 0.8 - 0.6 * math.exp(-0.3 * l)
        x_p, h_p, x_s, h_s = ffn(xs, hs, l, w_ffn1_up, w_ffn1_down, 2, (l, 4, 3))
        xs, hs = [x_p, x_s], [h_p, h_s]
        proj = lambda off, n, dts=(F32, F32), act=None, stacked=None: _mm_proj(
            hs, w_in_t, l, off, n, tms, dts, act, stacked)
        q_p, q_s = proj(0, w_a, (BF16, F32))
        k_buf, k_p, k_s = proj(w_a, w_a, (BF16, F32), stacked=(kv_p[0], a_heads, t_p))
        v_buf, v_p, v_s = proj(2 * w_a, w_av, (BF16, F32), stacked=(kv_p[1], a_heads, t_p))
        seg_b = proj(off_b, w_b)
        gl = proj(off_gl, 128)
        seg_c = proj(off_c, w_c)
        gates = proj(off_m, 3 * d, (BF16, BF16), "sigmoid")

        o_a_p = _attn_prompt(q_p, k_p, v_p, lam_vecs, subln, l, nb_p, t_p, a_heads, a_dk, a_dv, lam_init)
        rows = lambda z: z.reshape(nb_s, t_s * a_heads, z.shape[1] // a_heads)
        o_a_s = _attn_sample(rows(q_s), rows(k_s), rows(v_s), cache_k, cache_v, page_table,
                             lam_vecs, subln, l, lam_init).reshape(m_s, w_av)
        o_b_p, gla_p = _recurrence(seg_b[0], nb_p, t_p, b_heads, b_dk, b_dv, l,
                                   gla=(gl[0], w_gla_gate, b_gate, norm_b))
        o_b_s, gla_s = _recurrence(seg_b[1], nb_s, t_s, b_heads, b_dk, b_dv, l,
                                   gla=(gl[1], w_gla_gate, b_gate, norm_b), state=state_gla)
        o_c_p, ret_p = _recurrence(seg_c[0], nb_p, t_p, c_heads, c_dk, c_dv, l, ret=rope_p + (loga_c,))
        o_c_s, ret_s = _recurrence(seg_c[1], nb_s, t_s, c_heads, c_dk, c_dv, l, ret=rope_s + (loga_c,),
                                   state=state_ret)

        merged = _mm_merge([(o_a_p, o_b_p, o_c_p), (o_a_s, o_b_s, o_c_s)],
                           (w_branch_a, w_branch_b, w_branch_c), l, gates, tms)
        x_p, h_p, x_s, h_s = _mm_res_norm(grps, [(merged[0],), (merged[1],)], w_out, l, xs, 5, False,
                                          (l, 7, 6), tms_narrow)
        last = l == n_layers - 1
        out = ffn([x_p, x_s], [h_p, h_s], l, w_ffn2_up, w_ffn2_down, 8, final_norm if last else (l + 1, 1, 0))
        return out, (k_buf, v_buf), (k_s, v_s, gla_p, gla_s, ret_p, ret_s)

    xs = [x_prompt.reshape(m_p, d), x_sample.reshape(m_s, d)]
    hs = [_rmsmod(g, x, 0, 0, 1) for g, x in zip(grps, xs)]
    kv_p = (jnp.zeros((n_layers, nb_p, t_p, a_heads, a_dk2), F32),
            jnp.zeros((n_layers, nb_p, t_p, a_heads, a_dv), F32))
    per_layer = []
    for l in range(n_layers):
        out, kv_p, new = layer(xs, hs, l, kv_p)
        per_layer.append(new)
        if l < n_layers - 1:
            xs, hs = [out[0], out[2]], [out[1], out[3]]
    y_p = out[0].reshape(nb_p, t_p, d)
    y_s = out[1].reshape(nb_s, t_s, d)

    def stacked(i, shape):
        return jnp.stack([new[i] for new in per_layer]).reshape((n_layers,) + shape)

    return (y_p, y_s, kv_p[0], kv_p[1],
            stacked(0, (nb_s, t_s, a_heads, a_dk2)), stacked(1, (nb_s, t_s, a_heads, a_dv)),
            stacked(2, (nb_p, b_heads, b_dk, b_dv)), stacked(3, (nb_s, b_heads, b_dk, b_dv)),
            stacked(4, (nb_p, c_heads, c_dk, c_dv)), stacked(5, (nb_s, c_heads, c_dk, c_dv)))
```

```python
import functools
import math
from typing import Callable, NamedTuple

import jax
import jax.numpy as jnp
from jax import lax
from jax.experimental import pallas as pl
from jax.experimental.pallas import tpu as pltpu

EPS = 1e-6
CHUNK = 64
GLA_TAU = 16.0
ROPE_BASE = 10000.0
N_MOD = 9
BF16 = jnp.bfloat16
F32 = jnp.float32
NEG = float(jnp.finfo(jnp.float32).min)

V7X_VMEM_BYTES = 64 * 2**20
VMEM_CEILING = V7X_VMEM_BYTES - 6 * 2**20
SUBLANES = 8
PAGES_PER_STEP = 16


def _params(sem, vmem_bytes):
    assert vmem_bytes <= V7X_VMEM_BYTES
    return pltpu.CompilerParams(dimension_semantics=sem, vmem_limit_bytes=VMEM_CEILING)


def _largest_divisor(n, cands):
    for c in cands:
        if n % c == 0:
            return c
    raise ValueError(f"no tile in {cands} divides {n}")


def _nbytes(shape, dtype):
    return math.prod(shape) * jnp.dtype(dtype).itemsize


class Group(NamedTuple):
    mod: jax.Array
    per_row: bool
    rows: int
    rows_per_batch: int
    tm: int


def _mod_spec(g, l, i, d_model, tn, mn):
    nb = d_model // tn
    if g.per_row:
        return pl.BlockSpec((None, g.tm, tn), lambda *a: (l, mn(*a)[0], i * nb + mn(*a)[1]))
    blocks_per_batch = g.rows_per_batch // g.tm
    return pl.BlockSpec((None, None, 1, tn),
                        lambda *a: (l, mn(*a)[0] // blocks_per_batch, 0, i * nb + mn(*a)[1]))


def _ada_kernel(c_ref, w_ref, b_ref, o_ref):
    c = c_ref[...]
    s = (c * jax.nn.sigmoid(c)).astype(BF16)
    o_ref[...] = jnp.dot(s, w_ref[...].astype(BF16), preferred_element_type=F32) + b_ref[...]


def _ada_all(c_rows, w_ada, b_ada):
    n_layers, d_model, n = w_ada.shape
    rows = c_rows.shape[0]
    tn = _largest_divisor(n, (2048, 1024, 512, 256, 128))
    vmem = 2 * (_nbytes((d_model, tn), F32) + _nbytes((rows, d_model), F32)) + _nbytes((d_model, tn), F32)
    return pl.pallas_call(
        _ada_kernel,
        grid=(n_layers, n // tn),
        in_specs=[pl.BlockSpec((rows, d_model), lambda l, j: (0, 0)),
                  pl.BlockSpec((None, d_model, tn), lambda l, j: (l, 0, j)),
                  pl.BlockSpec((None, 1, tn), lambda l, j: (l, 0, j))],
        out_specs=pl.BlockSpec((None, rows, tn), lambda l, j: (l, 0, j)),
        out_shape=jax.ShapeDtypeStruct((n_layers, rows, n), F32),
        compiler_params=_params(("arbitrary", "arbitrary"), vmem + 8 * 2**20),
        name="adaln",
    )(c_rows, w_ada, b_ada.reshape(n_layers, 1, n))


def _rmsmod_kernel(x_ref, sc_ref, sh_ref, o_ref):
    x = x_ref[...]
    r = lax.rsqrt(jnp.mean(x * x, axis=-1, keepdims=True) + EPS)
    o_ref[...] = ((x * r) * (1.0 + sc_ref[...]) + sh_ref[...]).astype(o_ref.dtype)


def _rmsmod(g, x, l, i_shift, i_scale):
    m, d = x.shape
    tm = min(g.tm, 512)
    gg = g._replace(tm=tm)
    mn = lambda i: (i, 0)
    return pl.pallas_call(
        _rmsmod_kernel,
        grid=(m // tm,),
        in_specs=[pl.BlockSpec((tm, d), lambda i: (i, 0)),
                  _mod_spec(gg, l, i_scale, d, d, mn),
                  _mod_spec(gg, l, i_shift, d, d, mn)],
        out_specs=pl.BlockSpec((tm, d), lambda i: (i, 0)),
        out_shape=jax.ShapeDtypeStruct((m, d), BF16),
        compiler_params=_params(("arbitrary",), 6 * _nbytes((tm, d), F32) + 8 * 2**20),
        name="rmsmod",
    )(x, g.mod, g.mod)


class Rows(NamedTuple):
    xs: tuple
    extras: tuple
    outs: tuple
    epilogue: Callable
    tm: int


class Weight(NamedTuple):
    array: jax.Array
    spec: pl.BlockSpec
    tile: tuple
    x_index: int


def _tile_spec(tm, tn, rider):
    return pl.BlockSpec((tm, tn), (lambda j, i: (0, j)) if rider else (lambda j, i: (i, j)))


def _block_bytes(spec, dtype):
    if spec.block_shape is None:
        return 0
    dims = [1 if d is None or isinstance(d, pl.Squeezed) else getattr(d, "block_size", d) for d in spec.block_shape]
    return _nbytes(dims, dtype)


def _cast_weight(w_ref, wb_ref, transposed):
    if transposed:
        for c in range(w_ref.shape[0] // 128):
            wb_ref[:, c * 128:(c + 1) * 128] = w_ref[c * 128:(c + 1) * 128, :].T.astype(BF16)
        return
    rows = w_ref.shape[0]
    rb = _largest_divisor(rows, (512, 256, 128, 64, 32, 16))

    def body(i, c):
        r = pl.multiple_of(i * rb, rb)
        wb_ref[pl.ds(r, rb), :] = w_ref[pl.ds(r, rb), :].astype(BF16)
        return c

    lax.fori_loop(0, rows // rb, body, 0)


def _mm_kernel(*refs, counts, x_index, transposed, epilogues):
    it = iter(refs)
    take = lambda n: [next(it) for _ in range(n)]
    ins = [(take(nx), take(ne)) for nx, ne, _ in counts]
    w_refs = take(len(x_index))
    outs = [take(no) for _, _, no in counts]
    wb_refs = take(len(x_index))
    first_row_tile = pl.program_id(1) == 0

    @pl.when(first_row_tile)
    def _():
        for w_ref, wb_ref in zip(w_refs, wb_refs):
            _cast_weight(w_ref, wb_ref, transposed)

    def run(g):
        x_refs, extra_refs = ins[g]
        accs = [jnp.dot(x_refs[xi][...], wb_ref[...], preferred_element_type=F32)
                for xi, wb_ref in zip(x_index, wb_refs)]
        epilogues[g](accs, extra_refs, outs[g])

    run(0)
    if len(counts) > 1:
        pl.when(first_row_tile)(lambda: run(1))


def _mm(name, weights, groups, n_tiles, transposed=False, alias=None):
    primary = groups[0]
    in_specs, args, out_specs, out_shapes, counts = [], [], [], [], []
    vmem = 0
    for gi, g in enumerate(groups):
        for x in g.xs:
            spec = pl.BlockSpec((g.tm, x.shape[1]), (lambda j, i: (0, 0)) if gi else (lambda j, i: (i, 0)))
            in_specs.append(spec)
            args.append(x)
            vmem += 2 * _block_bytes(spec, x.dtype)
        if gi == 0 and alias is not None:
            aliases = {len(args) + alias[0]: alias[1]}
        for a, spec in g.extras:
            in_specs.append(spec)
            args.append(a)
            vmem += 2 * _block_bytes(spec, a.dtype)
        counts.append((len(g.xs), len(g.extras), len(g.outs)))
    for w in weights:
        in_specs.append(w.spec)
        args.append(w.array)
        vmem += 2 * _nbytes(w.tile, F32) + _nbytes(w.tile, BF16)
    for g in groups:
        for shape, spec in g.outs:
            out_specs.append(spec)
            out_shapes.append(shape)
            vmem += 2 * _block_bytes(spec, shape.dtype)
    tn = weights[0].tile[1]
    vmem += (len(weights) + 2) * _nbytes((primary.tm, tn), F32)
    m = primary.xs[0].shape[0]
    return pl.pallas_call(
        functools.partial(_mm_kernel, counts=tuple(counts), x_index=tuple(w.x_index for w in weights),
                          transposed=transposed, epilogues=tuple(g.epilogue for g in groups)),
        grid=(n_tiles, m // primary.tm),
        in_specs=in_specs,
        out_specs=out_specs,
        out_shape=out_shapes,
        scratch_shapes=[pltpu.VMEM(w.tile, BF16) for w in weights],
        input_output_aliases=aliases if alias is not None else {},
        compiler_params=_params(("arbitrary", "arbitrary"), vmem + 4 * 2**20),
        name=name,
    )(*args)


def _swiglu_epilogue(tail, accs, extra_refs, out_refs):
    a, b = accs
    if tail:
        tn = b.shape[1]
        shifted = jnp.concatenate([b[:, tn - tail:], b[:, :tn - tail]], axis=1)
        b = jnp.where(pl.program_id(0) == pl.num_programs(0) - 1, shifted, b)
    out_refs[0][...] = ((a * jax.nn.sigmoid(a)) * b).astype(out_refs[0].dtype)


def _mm_swiglu(hs, w_up, l, tms):
    k = hs[0].shape[1]
    f = w_up.shape[2] // 2
    tn = min(512, f)
    nt = pl.cdiv(f, tn)
    tail = f % tn
    assert tn % 128 == 0 and tail % 128 == 0 and nt * tn <= 2 * f
    back = lambda j: jnp.where(j == nt - 1, tn - tail, 0) if tail else 0
    w_spec = lambda base, moved: pl.BlockSpec(
        (pl.Squeezed(), pl.Element(k), pl.Element(tn)),
        lambda j, i: (l, 0, pl.multiple_of(base + j * tn - (back(j) if moved else 0), 128)))
    weights = [Weight(w_up, w_spec(0, False), (k, tn), 0), Weight(w_up, w_spec(f, True), (k, tn), 0)]
    groups = [Rows((h,), (), ((jax.ShapeDtypeStruct((h.shape[0], f), BF16), _tile_spec(tm, tn, gi > 0)),),
                   functools.partial(_swiglu_epilogue, tail), tm) for gi, (h, tm) in enumerate(zip(hs, tms))]
    return _mm("mm_swiglu", weights, groups, nt)


def _res_norm_kernel(*refs, counts, row0, n_stage, n_rows, half, final):
    it = iter(refs)
    take = lambda n: [next(it) for _ in range(n)]
    ins = [(take(nx), take(ne)) for nx, ne, _ in counts]
    w_ref = next(it)
    outs = [take(no) for _, _, no in counts]
    wb_ref = next(it)
    step = pl.program_id(0)

    rc, dc = w_ref.shape
    for c in range(n_stage // n_rows):
        @pl.when((step >= c * n_rows) & (step < (c + 1) * n_rows))
        def _():
            r = pl.multiple_of((step - c * n_rows) * rc, rc)
            wb_ref[pl.ds(r, rc), c * dc:(c + 1) * dc] = w_ref[...].astype(BF16)

    def run(g):
        x_refs, extra_refs = ins[g]
        acc = None
        for x_ref, r0 in zip(x_refs, row0):
            part = jnp.dot(x_ref[...], wb_ref[r0:r0 + x_ref.shape[1], :], preferred_element_type=F32)
            acc = part if acc is None else acc + part
        res_ref, g_ref = extra_refs[:2]
        gate = 1.0 + g_ref[...]
        if half:
            gate = 0.5 * gate
        x = res_ref[...] + gate * acc
        r = lax.rsqrt(jnp.mean(x * x, axis=-1, keepdims=True) + EPS)
        if final:
            outs[g][0][...] = (x * r) * extra_refs[2][...]
        else:
            sc_ref, sh_ref = extra_refs[2:]
            outs[g][0][...] = x
            outs[g][1][...] = ((x * r) * (1.0 + sc_ref[...]) + sh_ref[...]).astype(outs[g][1].dtype)

    pl.when(step >= n_stage)(lambda: run(0))
    if len(counts) > 1:
        pl.when(step == n_stage)(lambda: run(1))


def _mm_res_norm(gs, xss, w, l, ress, i_gate, half, nxt, tms):
    k, d = w.shape[1:]
    ks = [x.shape[1] for x in xss[0]]
    row0 = tuple(sum(ks[:i]) for i in range(len(ks)))
    assert sum(ks) == k and all(r % 16 == 0 for r in row0)
    final = not isinstance(nxt, tuple)
    dc = d // 2 if d % 256 == 0 else d
    rc = next(r for r in range(k, 0, -16) if k % r == 0 and r % 16 == 0 and _nbytes((r, dc), F32) <= 3 * 2**20)
    n_rows = k // rc
    n_stage = n_rows * (d // dc)
    in_specs, args, out_specs, out_shapes, counts = [], [], [], [], []
    vmem = _nbytes((k, d), BF16) + 2 * _nbytes((rc, dc), F32)
    for gi, (g, xs, res, tm) in enumerate(zip(gs, xss, ress, tms)):
        row = (lambda s: 0) if gi else (lambda s: jnp.maximum(s - n_stage, 0))
        blk = lambda w_: pl.BlockSpec((tm, w_), lambda s, row=row: (row(s), 0))
        mod = lambda ll, i, g=g, tm=tm, row=row: _mod_spec(g._replace(tm=tm), ll, i, d, d, lambda s: (row(s), 0))
        specs = [blk(x.shape[1]) for x in xs] + [blk(d), mod(l, i_gate)]
        ops = list(xs) + [res, g.mod]
        if final:
            specs.append(pl.BlockSpec((1, d), lambda s: (0, 0)))
            ops.append(nxt.reshape(1, d))
            outs = [jax.ShapeDtypeStruct(res.shape, F32)]
        else:
            specs += [mod(nxt[0], nxt[1]), mod(nxt[0], nxt[2])]
            ops += [g.mod, g.mod]
            outs = [jax.ShapeDtypeStruct(res.shape, F32), jax.ShapeDtypeStruct(res.shape, BF16)]
        in_specs += specs
        args += ops
        out_specs += [blk(d)] * len(outs)
        out_shapes += outs
        counts.append((len(xs), len(specs) - len(xs), len(outs)))
        vmem += 2 * sum(_nbytes((tm, x.shape[1]), BF16) for x in xs) + (2 + 2 + 1 + 3) * _nbytes((tm, d), F32)
    stage = lambda s: jnp.minimum(s, n_stage - 1)
    in_specs.append(pl.BlockSpec((None, rc, dc), lambda s: (l, stage(s) % n_rows, stage(s) // n_rows)))
    args.append(w)
    return pl.pallas_call(
        functools.partial(_res_norm_kernel, counts=tuple(counts), row0=row0, n_stage=n_stage, n_rows=n_rows,
                          half=half, final=final),
        grid=(n_stage + xss[0][0].shape[0] // tms[0],),
        in_specs=in_specs,
        out_specs=out_specs,
        out_shape=out_shapes,
        scratch_shapes=[pltpu.VMEM((k, d), BF16)],
        compiler_params=_params(("arbitrary",), vmem + 4 * 2**20),
        name="mm_res_norm",
    )(*args)


def _proj_epilogue(act, accs, extra_refs, out_refs):
    acc = accs[0]
    if act == "sigmoid":
        acc = jax.nn.sigmoid(acc)
    if len(out_refs) == 2:
        out_refs[0][...] = acc.reshape(out_refs[0].shape)
        out_refs[1][...] = acc.astype(out_refs[1].dtype)
    else:
        out_refs[0][...] = acc.astype(out_refs[0].dtype)


def _mm_proj(hs, w_t, l, col_off, ncols, tms, dtypes, act=None, stacked=None):
    k = hs[0].shape[1]
    assert col_off % SUBLANES == 0
    tn = _largest_divisor(ncols, (1024, 768, 512, 256, 128))
    w_spec = pl.BlockSpec((pl.Squeezed(), pl.Element(tn), pl.Element(k)),
                          lambda j, i: (l, pl.multiple_of(col_off + j * tn, SUBLANES), 0))
    groups = []
    for gi, (h, tm, dt) in enumerate(zip(hs, tms, dtypes)):
        extras = ()
        outs = ((jax.ShapeDtypeStruct((h.shape[0], ncols), dt), _tile_spec(tm, tn, gi > 0)),)
        if gi == 0 and stacked is not None:
            buf, heads, rows_per_batch = stacked
            assert tn == ncols and rows_per_batch % tm == 0
            bpb = rows_per_batch // tm
            extras = ((buf, pl.BlockSpec(memory_space=pl.ANY)),)
            outs = ((jax.ShapeDtypeStruct(buf.shape, buf.dtype),
                     pl.BlockSpec((None, None, tm, heads, ncols // heads),
                                  lambda j, i: (l, i // bpb, i % bpb, 0, 0))),) + outs
        groups.append(Rows((h,), extras, outs, functools.partial(_proj_epilogue, act), tm))
    return _mm("mm_proj", [Weight(w_t, w_spec, (k, tn), 0)], groups, ncols // tn, transposed=True,
               alias=(0, 0) if stacked is not None else None)


def _merge_epilogue(accs, extra_refs, out_refs):
    acc = None
    for part, g_ref in zip(accs, extra_refs):
        part = g_ref[...].astype(F32) * part
        acc = part if acc is None else acc + part
    out_refs[0][...] = acc.astype(out_refs[0].dtype)


def _mm_merge(xss, ws, l, gatess, tms):
    d = ws[0].shape[2]
    tn = _largest_divisor(d, (512, 256, 128))
    nb = d // tn
    weights = [Weight(w, pl.BlockSpec((None, w.shape[1], tn), lambda j, i: (l, 0, j)), (w.shape[1], tn), i)
               for i, w in enumerate(ws)]
    groups = []
    for gi, (xs, gates, tm) in enumerate(zip(xss, gatess, tms)):
        g_spec = lambda s, gi=gi, tm=tm: pl.BlockSpec(
            (tm, tn), (lambda j, i: (0, s * nb + j)) if gi else (lambda j, i: (i, s * nb + j)))
        groups.append(Rows(tuple(xs), tuple((gates, g_spec(s)) for s in range(len(ws))),
                           ((jax.ShapeDtypeStruct((xs[0].shape[0], d), BF16), _tile_spec(tm, tn, gi > 0)),),
                           _merge_epilogue, tm))
    return _mm("mm_merge", weights, groups, nb)


def _lambda_value(lam_ref, lam_init):
    lv = lam_ref[...]
    e1 = jnp.exp(jnp.sum(lv[0:1] * lv[1:2], axis=-1, keepdims=True))
    e2 = jnp.exp(jnp.sum(lv[2:3] * lv[3:4], axis=-1, keepdims=True))
    return e1 - e2 + lam_init


def _subln(o, sub_ref, lam_init):
    r = lax.rsqrt(jnp.mean(o * o, axis=-1, keepdims=True) + EPS)
    return ((o * r) * sub_ref[...]) * (1.0 - lam_init)


def _attn_prompt_kernel(lam_ref, sub_ref, q_ref, k_ref, v_ref, o_ref, *, tq, dk, lam_init):
    t = q_ref.shape[0]
    c = (dk ** -0.5) * math.log2(math.e)
    lam = _lambda_value(lam_ref, lam_init)
    dims = (((1,), (1,)), ((), ()))
    below = (lax.broadcasted_iota(jnp.int32, (tq, tq), 1) <= lax.broadcasted_iota(jnp.int32, (tq, tq), 0))
    for qi in range(t // tq):
        lo, hi = qi * tq, (qi + 1) * tq
        q = q_ref[lo:hi, :]
        maps = []
        for n in range(2):
            cols = slice(n * dk, (n + 1) * dk)
            s_own = jnp.where(below, lax.dot_general(q[:, cols], k_ref[lo:hi, cols], dims,
                                                     preferred_element_type=F32), NEG)
            m = jnp.max(s_own, axis=-1, keepdims=True)
            if qi:
                s_past = lax.dot_general(q[:, cols], k_ref[0:lo, cols], dims, preferred_element_type=F32)
                m = jnp.maximum(m, jnp.max(s_past, axis=-1, keepdims=True))
            e = jnp.exp2((s_own - m) * c)
            l = jnp.sum(e, axis=-1, keepdims=True)
            o = jnp.dot(e.astype(BF16), v_ref[lo:hi, :], preferred_element_type=F32)
            if qi:
                e = jnp.exp2((s_past - m) * c)
                l = l + jnp.sum(e, axis=-1, keepdims=True)
                o = o + jnp.dot(e.astype(BF16), v_ref[0:lo, :], preferred_element_type=F32)
            maps.append(o * (1.0 / l))
        o = maps[0] - lam * maps[1]
        o_ref[lo:hi, :] = _subln(o, sub_ref, lam_init).astype(o_ref.dtype)


def _attn_prompt(q, k, v, lam_vecs, subln, l, nb, t, heads, dk, dv, lam_init):
    tq = _largest_divisor(t, (256, 128, 64, 32, 16, 8))
    blk = lambda w: pl.BlockSpec((t, w), lambda b, h: (b, h))
    vmem = 2 * (2 * _nbytes((t, 2 * dk), BF16) + 2 * _nbytes((t, dv), BF16)) + 8 * _nbytes((tq, t), F32)
    return pl.pallas_call(
        functools.partial(_attn_prompt_kernel, tq=tq, dk=dk, lam_init=lam_init),
        grid=(nb, heads),
        in_specs=[pl.BlockSpec((None, 4, dk), lambda b, h: (l, 0, 0)),
                  pl.BlockSpec((None, 1, dv), lambda b, h: (l, 0, 0)),
                  blk(2 * dk), blk(2 * dk), blk(dv)],
        out_specs=blk(dv),
        out_shape=jax.ShapeDtypeStruct((nb * t, heads * dv), BF16),
        compiler_params=_params(("arbitrary", "arbitrary"), vmem + 8 * 2**20),
        name="attn_prompt",
    )(lam_vecs, subln, q, k, v)


def _attn_sample_kernel(pt_ref, lam_ref, sub_ref, q_ref, kn_ref, vn_ref, *rest,
                        n_pages, heads, dk, lam_init):
    k_refs = rest[:n_pages]
    v_refs = rest[n_pages:2 * n_pages]
    o_ref, m_ref, l_ref, acc_ref, mask_ref = rest[2 * n_pages:]
    p = pl.program_id(1)
    scale = dk ** -0.5
    nq = q_ref.shape[0]
    dims = (((1,), (1,)), ((), ()))
    q = q_ref[...].astype(BF16)

    def scores(keys):
        s = [lax.dot_general(q[:, n * dk:(n + 1) * dk], keys[:, n * dk:(n + 1) * dk], dims,
                             preferred_element_type=F32) for n in range(2)]
        return jnp.concatenate(s, axis=0) * scale

    @pl.when(p == 0)
    def _():
        n_cols = mask_ref.shape[1]
        r = lax.broadcasted_iota(jnp.int32, (2 * nq, n_cols), 0)
        c = lax.broadcasted_iota(jnp.int32, (2 * nq, n_cols), 1)
        mask_ref[...] = jnp.where((r % heads) == (c % heads), 1.0, 0.0)
        r = lax.broadcasted_iota(jnp.int32, (2 * nq, nq), 0) % nq
        c = lax.broadcasted_iota(jnp.int32, (2 * nq, nq), 1)
        ok = ((r % heads) == (c % heads)) & ((c // heads) <= (r // heads))
        s = jnp.where(ok, scores(kn_ref[...].astype(BF16)), NEG)
        m = jnp.max(s, axis=-1, keepdims=True)
        e = jnp.exp(s - m)
        m_ref[...] = m
        l_ref[...] = jnp.sum(e, axis=-1, keepdims=True)
        acc_ref[...] = jnp.dot(e.astype(BF16), vn_ref[...].astype(BF16), preferred_element_type=F32)

    rows = k_refs[0].shape[0] * k_refs[0].shape[1]
    keys = jnp.concatenate([r[...].reshape(rows, 2 * dk).astype(BF16) for r in k_refs], axis=0)
    vals = jnp.concatenate([r[...].reshape(rows, v_refs[0].shape[2]).astype(BF16) for r in v_refs], axis=0)
    s = jnp.where(mask_ref[...] > 0.5, scores(keys), NEG)
    m_old = m_ref[...]
    m_new = jnp.maximum(m_old, jnp.max(s, axis=-1, keepdims=True))
    alpha = jnp.exp(m_old - m_new)
    e = jnp.exp(s - m_new)
    m_ref[...] = m_new
    l_ref[...] = alpha * l_ref[...] + jnp.sum(e, axis=-1, keepdims=True)
    acc_ref[...] = alpha * acc_ref[...] + jnp.dot(e.astype(BF16), vals, preferred_element_type=F32)

    @pl.when(p == pl.num_programs(1) - 1)
    def _():
        o = acc_ref[...] * (1.0 / l_ref[...])
        o = o[0:nq] - _lambda_value(lam_ref, lam_init) * o[nq:2 * nq]
        o_ref[...] = _subln(o, sub_ref, lam_init).astype(o_ref.dtype)


def _attn_sample(q, kn, vn, cache_k, cache_v, page_table, lam_vecs, subln, l, lam_init):
    nb, nq, _ = q.shape
    _, _, page, heads, dk2 = cache_k.shape
    dv = cache_v.shape[4]
    dk = dk2 // 2
    n_used = page_table.shape[1]
    g = _largest_divisor(n_used, (PAGES_PER_STEP, 2, 1))
    page_spec = lambda i, w: pl.BlockSpec((None, None, page, heads, w),
                                          lambda b, p, pt: (l, pt[b, p * g + i], 0, 0, 0))
    row_spec = lambda w: pl.BlockSpec((None, nq, w), lambda b, p, pt: (b, 0, 0))
    vmem = 2 * g * _nbytes((page, heads, dk2 + dv), F32)
    vmem += 4 * _nbytes((2 * nq, g * page * heads), F32) + _nbytes((g * page * heads, dk2 + dv), BF16)
    return pl.pallas_call(
        functools.partial(_attn_sample_kernel, n_pages=g, heads=heads, dk=dk, lam_init=lam_init),
        grid_spec=pltpu.PrefetchScalarGridSpec(
            num_scalar_prefetch=1,
            grid=(nb, n_used // g),
            in_specs=[pl.BlockSpec((None, 4, dk), lambda b, p, pt: (l, 0, 0)),
                      pl.BlockSpec((None, 1, dv), lambda b, p, pt: (l, 0, 0)),
                      row_spec(dk2), row_spec(dk2), row_spec(dv)]
                     + [page_spec(i, dk2) for i in range(g)] + [page_spec(i, dv) for i in range(g)],
            out_specs=row_spec(dv),
            scratch_shapes=[pltpu.VMEM((2 * nq, 1), F32), pltpu.VMEM((2 * nq, 1), F32),
                            pltpu.VMEM((2 * nq, dv), F32), pltpu.VMEM((2 * nq, g * page * heads), F32)]),
        out_shape=jax.ShapeDtypeStruct((nb, nq, dv), BF16),
        compiler_params=_params(("arbitrary", "arbitrary"), vmem + 8 * 2**20),
        name="attn_sample",
    )(page_table, lam_vecs, subln, q, kn, vn, *([cache_k] * g), *([cache_v] * g))


def _split3(a):
    hi = a.astype(BF16)
    r1 = a - hi.astype(F32)
    mid = r1.astype(BF16)
    lo = (r1 - mid.astype(F32)).astype(BF16)
    return hi, mid, lo


def _exact_dot(sel, a, dims):
    out = None
    for part in _split3(a):
        d = lax.dot_general(sel, part, dims, preferred_element_type=F32)
        out = d if out is None else out + d
    return out


def _exact_dot_t(a, ones, dims):
    out = None
    for part in _split3(a):
        d = lax.dot_general(part, ones, dims, preferred_element_type=F32)
        out = d if out is None else out + d
    return out


def _rec_kernel(*refs, chunk, heads, dk, dv, is_gla, has_state):
    it = iter(refs)
    q_ref, k_ref, v_ref, r_ref = next(it), next(it), next(it), next(it)
    if is_gla:
        gl_ref, wg_ref, bg_ref, norm_ref = next(it), next(it), next(it), next(it)
    else:
        cos_ref, sin_ref, loga_ref = next(it), next(it), next(it)
    s0_ref = next(it) if has_state else None
    o_ref, s_out_ref, s_ref = next(it), next(it), next(it)
    t = pl.program_id(1)
    rows = q_ref.shape[0]
    n_chunks = rows // chunk

    @pl.when(t == 0)
    def _():
        s_ref[...] = s0_ref[...] if has_state else jnp.zeros(s_ref.shape, F32)

    q, k, v = q_ref[...], k_ref[...], v_ref[...]
    if is_gla:
        gl = gl_ref[:, 0:wg_ref.shape[0]]
        z = jnp.dot(gl.astype(BF16), wg_ref[...].astype(BF16), preferred_element_type=F32) + bg_ref[...]
        a = jax.nn.log_sigmoid(z) / GLA_TAU
        q = q * (dk ** -0.5)
    else:
        a = jnp.broadcast_to(loga_ref[...], q.shape)
        half = dk // 2
        width = q.shape[1]
        lane = lax.broadcasted_iota(jnp.int32, q.shape, 1)
        first_half = (lane % dk) < half
        cos, sin = cos_ref[...], sin_ref[...]

        def rot(x):
            other = jnp.where(first_half, pltpu.roll(x, width - half, 1), pltpu.roll(x, half, 1))
            return x * cos + other * sin

        q = rot(q)
        k = rot(k) * (dk ** -0.5)

    ri = lax.broadcasted_iota(jnp.int32, (rows, rows), 0)
    ci = lax.broadcasted_iota(jnp.int32, (rows, rows), 1)
    same_chunk = (ri // chunk) == (ci // chunk)
    causal = same_chunk & (ci <= ri)
    mm = (((1,), (0,)), ((), ()))
    b = _exact_dot(jnp.where(causal, 1.0, 0.0).astype(BF16), a, mm)
    b_last = _exact_dot(jnp.where(same_chunk, 1.0, 0.0).astype(BF16), a, mm)
    q_t = q * jnp.exp(b)
    k_t = k * jnp.exp(-b)
    k_end = k * jnp.exp(b_last - b)
    ones = jnp.ones((chunk, dv), BF16)
    tt = (((0,), (0,)), ((), ()))
    nt = (((1,), (1,)), ((), ()))
    outs = []
    for h in range(heads):
        sk = slice(h * dk, (h + 1) * dk)
        sv = slice(h * dv, (h + 1) * dv)
        qh, kh, keh, vh = q_t[:, sk].astype(BF16), k_t[:, sk].astype(BF16), k_end[:, sk].astype(BF16), v[:, sv].astype(BF16)
        attn = jnp.where(causal, lax.dot_general(qh, kh, nt, preferred_element_type=F32), 0.0)
        o = jnp.dot(attn.astype(BF16), vh, preferred_element_type=F32)
        state = s_ref[h]
        inter = []
        for c in range(n_chunks):
            rs = slice(c * chunk, (c + 1) * chunk)
            inter.append(jnp.dot(qh[rs], state.astype(BF16), preferred_element_type=F32))
            decay = jnp.exp(_exact_dot_t(a[rs, sk], ones, tt))
            state = decay * state + lax.dot_general(keh[rs], vh[rs], tt, preferred_element_type=F32)
        s_ref[h] = state
        o = o + (inter[0] if n_chunks == 1 else jnp.concatenate(inter, axis=0))
        o = o * lax.rsqrt(jnp.mean(o * o, axis=-1, keepdims=True) + EPS)
        if is_gla:
            o = o * norm_ref[...]
        gate = r_ref[:, sv]
        outs.append(o * (gate * jax.nn.sigmoid(gate)))
    o_ref[...] = jnp.concatenate(outs, axis=1).astype(o_ref.dtype)

    @pl.when(t == pl.num_programs(1) - 1)
    def _():
        s_out_ref[...] = s_ref[...]


def _recurrence(seg, nb, t, heads, dk, dv, l, *, gla=None, ret=None, state=None):
    is_gla = gla is not None
    chunk = min(CHUNK, t)
    rows = _largest_divisor(t, (4 * chunk, 2 * chunk, chunk))
    nblk = t // rows
    wq, wv = heads * dk, heads * dv
    assert wv == 2 * wq
    row_blk = lambda w, j: pl.BlockSpec((rows, w), lambda b, i: (b * nblk + i, j))
    in_specs = [row_blk(wq, 0), row_blk(wq, 1), row_blk(wv, 1), row_blk(wv, 2)]
    args = [seg, seg, seg, seg]
    if is_gla:
        gl, w_gate, b_gate, norm = gla
        rank = w_gate.shape[1]
        in_specs += [pl.BlockSpec((rows, gl.shape[1]), lambda b, i: (b * nblk + i, 0)),
                     pl.BlockSpec((None, rank, wq), lambda b, i: (l, 0, 0)),
                     pl.BlockSpec((None, 1, wq), lambda b, i: (l, 0, 0)),
                     pl.BlockSpec((None, 1, dv), lambda b, i: (l, 0, 0))]
        args += [gl, w_gate, b_gate, norm]
    else:
        cos, sin, loga = ret
        in_specs += [pl.BlockSpec((rows, wq), lambda b, i: (i, 0)),
                     pl.BlockSpec((rows, wq), lambda b, i: (i, 0)),
                     pl.BlockSpec((1, wq), lambda b, i: (0, 0))]
        args += [cos, sin, loga]
    if state is not None:
        in_specs.append(pl.BlockSpec((None, None, heads, dk, dv), lambda b, i: (l, b, 0, 0, 0)))
        args.append(state)
    vmem = 2 * (2 * _nbytes((rows, wq), F32) + 2 * _nbytes((rows, wv), F32) + _nbytes((rows, wv), BF16))
    vmem += 16 * _nbytes((rows, wq), F32) + 8 * _nbytes((rows, rows), F32)
    return dict(
        kwargs=dict(chunk=chunk, heads=heads, dk=dk, dv=dv, is_gla=is_gla, has_state=state is not None),
        grid=(nb, nblk), in_specs=in_specs, args=args, vmem=vmem,
        out_specs=[pl.BlockSpec((rows, wv), lambda b, i: (b * nblk + i, 0)),
                   pl.BlockSpec((None, heads, dk, dv), lambda b, i: (b, 0, 0, 0))],
        out_shape=[jax.ShapeDtypeStruct((nb * t, wv), BF16),
                   jax.ShapeDtypeStruct((nb, heads, dk, dv), F32)],
        scratch=pltpu.VMEM((heads, dk, dv), F32))


def _rec_pair_kernel(*refs, n_in, kwargs):
    a_in, b_in = refs[:n_in[0]], refs[n_in[0]:n_in[0] + n_in[1]]
    o_a, s_a, o_b, s_b, scr_a, scr_b = refs[n_in[0] + n_in[1]:]
    _rec_kernel(*a_in, o_a, s_a, scr_a, **kwargs[0])
    _rec_kernel(*b_in, o_b, s_b, scr_b, **kwargs[1])


def _recurrence_pair(a, b):
    assert a["grid"] == b["grid"]
    return pl.pallas_call(
        functools.partial(_rec_pair_kernel, n_in=(len(a["args"]), len(b["args"])),
                          kwargs=(a["kwargs"], b["kwargs"])),
        grid=a["grid"],
        in_specs=a["in_specs"] + b["in_specs"],
        out_specs=a["out_specs"] + b["out_specs"],
        out_shape=a["out_shape"] + b["out_shape"],
        scratch_shapes=[a["scratch"], b["scratch"]],
        compiler_params=_params(("arbitrary", "arbitrary"), a["vmem"] + b["vmem"] + 8 * 2**20),
        name="gla_retention",
    )(*a["args"], *b["args"])


def _rope_tables(pos, heads, dk):
    half = dk // 2
    freq = 1.0 / (ROPE_BASE ** jnp.linspace(0.0, 1.0, half, dtype=F32))
    ang = pos.astype(F32)[:, None] * freq[None, :]
    cos, sin = jnp.cos(ang), jnp.sin(ang)
    cos = jnp.tile(jnp.concatenate([cos, cos], axis=-1), (1, heads))
    sin = jnp.tile(jnp.concatenate([-sin, sin], axis=-1), (1, heads))
    return cos, sin


def kernel(x_prompt, x_sample, cache_k, cache_v, state_gla, state_ret, page_table, c_prompt, c_sample,
           w_ada, b_ada, w_ffn1_up, w_ffn1_down, w_in, lambda_q1, lambda_k1, lambda_q2, lambda_k2,
           attn_subln, w_gla_gate, b_gla_gate, gla_norm, w_branch_a, w_branch_b, w_branch_c, w_out,
           w_ffn2_up, w_ffn2_down, final_norm):
    nb_p, t_p, d = x_prompt.shape
    nb_s, t_s, _ = x_sample.shape
    n_layers = w_ada.shape[0]
    a_heads, a_dk2 = cache_k.shape[3:]
    a_dk, a_dv = a_dk2 // 2, cache_v.shape[4]
    b_heads, b_dk, b_dv = state_gla.shape[2:]
    c_heads, c_dk, c_dv = state_ret.shape[2:]
    rank = w_gla_gate.shape[1]
    n_pages = page_table.shape[1]
    past_len = n_pages * cache_k.shape[2]

    w_a = a_heads * a_dk2
    w_av = a_heads * a_dv
    w_b = 2 * b_heads * b_dk + 2 * b_heads * b_dv
    w_c = 2 * c_heads * c_dk + 2 * c_heads * c_dv
    off_b = 2 * w_a + w_av
    off_gl = off_b + w_b
    off_c = off_gl + rank
    off_m = off_c + w_c
    assert off_m + 3 * d == w_in.shape[2]
    assert rank <= 128 and off_gl + 128 <= w_in.shape[2]
    w_in_t = jnp.swapaxes(w_in, 1, 2)

    n_c = nb_p + nb_s
    c_rows = jnp.concatenate([c_prompt, c_sample, jnp.zeros((-n_c % SUBLANES, d), F32)], axis=0)
    mod = _ada_all(c_rows, w_ada, b_ada)
    m_p, m_s = nb_p * t_p, nb_s * t_s
    grp_p = Group(mod[:, :nb_p].reshape(n_layers, nb_p, 1, N_MOD * d), False, m_p, t_p,
                  _largest_divisor(t_p, (1024, 512, 256, 128, 64, 32, 16)))
    grp_s = Group(jnp.repeat(mod[:, nb_p:n_c], t_s, axis=1), True, m_s, t_s, m_s)
    grps = (grp_p, grp_s)
    tms = (grp_p.tm, m_s)
    tms_narrow = (min(grp_p.tm, 512), m_s)

    lam_vecs = jnp.stack([lambda_q1, lambda_k1, lambda_q2, lambda_k2], axis=1)
    subln = attn_subln.reshape(n_layers, 1, a_dv)
    b_gate = b_gla_gate.reshape(n_layers, 1, b_heads * b_dk)
    norm_b = gla_norm.reshape(n_layers, 1, b_dv)
    log_gamma = jnp.log(1.0 - jnp.power(2.0, -5.0 - jnp.arange(c_heads, dtype=F32)))
    loga_c = jnp.repeat(log_gamma, c_dk).reshape(1, c_heads * c_dk)
    rope_p = _rope_tables(jnp.arange(t_p, dtype=jnp.int32), c_heads, c_dk)
    rope_s = _rope_tables(past_len + jnp.arange(t_s, dtype=jnp.int32), c_heads, c_dk)

    tms_down = (min(grp_p.tm, 256), m_s)

    def ffn(xs, hs, l, w_up, w_down, i_gate, nxt):
        acts = _mm_swiglu(hs, w_up, l, tms)
        return _mm_res_norm(grps, [(act,) for act in acts], w_down, l, xs, i_gate, True, nxt, tms_down)

    def layer(xs, hs, l, kv_p):
        lam_init = 0.8 - 0.6 * math.exp(-0.3 * l)
        x_p, h_p, x_s, h_s = ffn(xs, hs, l, w_ffn1_up, w_ffn1_down, 2, (l, 4, 3))
        xs, hs = [x_p, x_s], [h_p, h_s]
        proj = lambda off, n, dts=(F32, F32), act=None, stacked=None: _mm_proj(
            hs, w_in_t, l, off, n, tms, dts, act, stacked)
        q_p, q_s = proj(0, w_a, (BF16, F32))
        k_buf, k_p, k_s = proj(w_a, w_a, (BF16, F32), stacked=(kv_p[0], a_heads, t_p))
        v_buf, v_p, v_s = proj(2 * w_a, w_av, (BF16, F32), stacked=(kv_p[1], a_heads, t_p))
        seg_b = proj(off_b, w_b)
        gl = proj(off_gl, 128)
        seg_c = proj(off_c, w_c)
        gates = proj(off_m, 3 * d, (BF16, BF16), "sigmoid")

        o_a_p = _attn_prompt(q_p, k_p, v_p, lam_vecs, subln, l, nb_p, t_p, a_heads, a_dk, a_dv, lam_init)
        rows = lambda z: z.reshape(nb_s, t_s * a_heads, z.shape[1] // a_heads)
        o_a_s = _attn_sample(rows(q_s), rows(k_s), rows(v_s), cache_k, cache_v, page_table,
                             lam_vecs, subln, l, lam_init).reshape(m_s, w_av)
        o_b_p, gla_p, o_c_p, ret_p = _recurrence_pair(
            _recurrence(seg_b[0], nb_p, t_p, b_heads, b_dk, b_dv, l, gla=(gl[0], w_gla_gate, b_gate, norm_b)),
            _recurrence(seg_c[0], nb_p, t_p, c_heads, c_dk, c_dv, l, ret=rope_p + (loga_c,)))
        o_b_s, gla_s, o_c_s, ret_s = _recurrence_pair(
            _recurrence(seg_b[1], nb_s, t_s, b_heads, b_dk, b_dv, l,
                        gla=(gl[1], w_gla_gate, b_gate, norm_b), state=state_gla),
            _recurrence(seg_c[1], nb_s, t_s, c_heads, c_dk, c_dv, l, ret=rope_s + (loga_c,), state=state_ret))

        merged = _mm_merge([(o_a_p, o_b_p, o_c_p), (o_a_s, o_b_s, o_c_s)],
                           (w_branch_a, w_branch_b, w_branch_c), l, gates, tms)
        x_p, h_p, x_s, h_s = _mm_res_norm(grps, [(merged[0],), (merged[1],)], w_out, l, xs, 5, False,
                                          (l, 7, 6), tms_narrow)
        last = l == n_layers - 1
        out = ffn([x_p, x_s], [h_p, h_s], l, w_ffn2_up, w_ffn2_down, 8, final_norm if last else (l + 1, 1, 0))
        return out, (k_buf, v_buf), (k_s, v_s, gla_p, gla_s, ret_p, ret_s)

    xs = [x_prompt.reshape(m_p, d), x_sample.reshape(m_s, d)]
    hs = [_rmsmod(g, x, 0, 0, 1) for g, x in zip(grps, xs)]
    kv_p = (jnp.zeros((n_layers, nb_p, t_p, a_heads, a_dk2), F32),
            jnp.zeros((n_layers, nb_p, t_p, a_heads, a_dv), F32))
    per_layer = []
    for l in range(n_layers):
        out, kv_p, new = layer(xs, hs, l, kv_p)
        per_layer.append(new)
        if l < n_layers - 1:
            xs, hs = [out[0], out[2]], [out[1], out[3]]
    y_p = out[0].reshape(nb_p, t_p, d)
    y_s = out[1].reshape(nb_s, t_s, d)

    def stacked(i, shape):
        return jnp.stack([new[i] for new in per_layer]).reshape((n_layers,) + shape)

    return (y_p, y_s, kv_p[0], kv_p[1],
            stacked(0, (nb_s, t_s, a_heads, a_dk2)), stacked(1, (nb_s, t_s, a_heads, a_dv)),
            stacked(2, (nb_p, b_heads, b_dk, b_dv)), stacked(3, (nb_s, b_heads, b_dk, b_dv)),
            stacked(4, (nb_p, c_heads, c_dk, c_dv)), stacked(5, (nb_s, c_heads, c_dk, c_dv)))
```
